```python
import functools
import jax
import jax.numpy as jnp
from jax import lax

D_MODEL = 1024
BATCH = 16
SEQ = 2048
DEPTH = 2

GRID_W = 64
CTX_LEN = 256
N_GROUPS = 4
GROUP_W = D_MODEL // 4
MIX_W = 4 * GROUP_W
SUB_W = GROUP_W // N_GROUPS
CHUNK = 128
POOL_WINDOWS = (2, 4, 8, 16)
MLA_HEADS = 4
QK_NOPE = 64
QK_ROPE = 32
V_HEAD = 64
QK_HEAD = QK_NOPE + QK_ROPE
Q_LORA = 3 * D_MODEL // 16
KV_LORA = D_MODEL // 8
ROPE_BASE = 10000.0
Q_BLOCK = 128
D_FF = 7 * D_MODEL // 2
N_EXPERTS = 8
TOP_K = 2
N_ADA = 6
EPS = 1e-6
N_DENSE = (DEPTH + 1) // 2
N_MOE = DEPTH // 2

OFF_A = 0
OFF_B = OFF_A + 2 * GROUP_W
OFF_CQ = OFF_B + GROUP_W
OFF_CKV = OFF_CQ + Q_LORA
OFF_CKR = OFF_CKV + KV_LORA
OFF_D = OFF_CKR + QK_ROPE
IN_W = OFF_D + GROUP_W

kernel_name = 'hybrid_parallel_groups_dit_block'


def rmsnorm(x, g):
    xf = x.astype(jnp.float32)
    y = xf * lax.rsqrt(jnp.mean(xf * xf, axis=-1, keepdims=True) + EPS)
    return (y * g.astype(jnp.float32)).astype(x.dtype)


def modulate(x, g, shift, scale):
    return rmsnorm(x, g) * (1 + scale) + shift


def axial_rope_tables(length):
    rows = length // GRID_W
    row = jnp.repeat(jnp.arange(rows, dtype=jnp.float32), GRID_W)
    col = jnp.tile(jnp.arange(GRID_W, dtype=jnp.float32), rows)
    n_freq = QK_ROPE // 4
    inv_freq = ROPE_BASE ** (-jnp.arange(n_freq, dtype=jnp.float32) / n_freq)
    ang = jnp.stack([row[:, None] * inv_freq, col[:, None] * inv_freq], axis=1)
    return jnp.cos(ang), jnp.sin(ang)


def apply_axial_rope(x, cos, sin):
    b, l, h, _ = x.shape
    xr = x.astype(jnp.float32).reshape(b, l, h, 2, 2, QK_ROPE // 4)
    x1, x2 = xr[..., 0, :], xr[..., 1, :]
    cs, sn = cos[:, None], sin[:, None]
    out = jnp.stack([x1 * cs - x2 * sn, x2 * cs + x1 * sn], axis=-2)
    return out.reshape(b, l, h, QK_ROPE).astype(x.dtype)


def chunk_spatial_gate(z, norm_g, w_s, b_s):
    b, l, _ = z.shape
    uv = jax.nn.gelu(z)
    u = uv[..., :GROUP_W]
    v = rmsnorm(uv[..., GROUP_W:], norm_g).reshape(b, l // CHUNK, CHUNK, N_GROUPS, SUB_W)
    mixed = jnp.einsum('gts,bnsgc->bntgc', w_s, v) + b_s.T[None, None, :, :, None]
    return u * mixed.reshape(b, l, GROUP_W)


def multiscale_pool(z, w_p, scale):
    b, l, _ = z.shape
    zf = z.astype(jnp.float32).reshape(b, l, N_GROUPS, SUB_W)
    cs = jnp.concatenate([jnp.zeros((b, 1, N_GROUPS, SUB_W), jnp.float32), jnp.cumsum(zf, axis=1)], axis=1)
    t = jnp.arange(l)
    pooled = []
    for gi, w in enumerate(POOL_WINDOWS):
        lo = jnp.clip(t - w // 2, 0, l)
        hi = jnp.clip(t - w // 2 + w, 0, l)
        csg = cs[:, :, gi]
        s = jnp.take(csg, hi, axis=1) - jnp.take(csg, lo, axis=1)
        pooled.append(s / (hi - lo).astype(jnp.float32)[:, None])
    diff = jnp.stack(pooled, axis=2) - zf
    y = jnp.einsum('blgc,gcd->blgd', diff, w_p.astype(jnp.float32)).reshape(b, l, GROUP_W)
    return (y * scale).astype(z.dtype)


def fourier_mix(z, w_f):
    b, l, _ = z.shape
    zf = z.astype(jnp.float32).reshape(b, l, N_GROUPS, SUB_W)
    y = jnp.fft.fft2(zf, axes=(1, 3), norm='ortho').real.reshape(b, l, GROUP_W)
    return y.astype(z.dtype) @ w_f


def mla_q(p, q_norm_g, w_uq, qk_q_g, rope):
    b, l, _ = p.shape
    q = (rmsnorm(p[..., OFF_CQ:OFF_CKV], q_norm_g) @ w_uq).reshape(b, l, MLA_HEADS, QK_HEAD)
    q = rmsnorm(q, qk_q_g)
    if rope is not None:
        q = jnp.concatenate([q[..., :QK_NOPE], apply_axial_rope(q[..., QK_NOPE:], *rope)], axis=-1)
    return q.transpose(0, 2, 1, 3)


def mla_kv(p, kv_norm_g, w_ukv, qk_k_g, rope):
    b, l, _ = p.shape
    kv = (rmsnorm(p[..., OFF_CKV:OFF_CKR], kv_norm_g) @ w_ukv).reshape(b, l, MLA_HEADS, QK_NOPE + V_HEAD)
    k_rope = jnp.broadcast_to(p[:, :, None, OFF_CKR:OFF_D], (b, l, MLA_HEADS, QK_ROPE))
    k = rmsnorm(jnp.concatenate([kv[..., :QK_NOPE], k_rope], axis=-1), qk_k_g)
    if rope is not None:
        k = jnp.concatenate([k[..., :QK_NOPE], apply_axial_rope(k[..., QK_NOPE:], *rope)], axis=-1)
    return k.transpose(0, 2, 1, 3), kv[..., QK_NOPE:].transpose(0, 2, 1, 3)


def attend(q, k, v):
    s = jnp.einsum('bhqd,bhkd->bhqk', q, k).astype(jnp.float32) * (QK_HEAD ** -0.5)
    p = jax.nn.softmax(s, axis=-1).astype(v.dtype)
    return jnp.einsum('bhqk,bhkd->bhqd', p, v)


def heads_to_channels(o):
    b, h, l, d = o.shape
    return o.transpose(0, 2, 1, 3).reshape(b, l, h * d)


def blocked_attention(q, k, v):
    b, h, l, d = q.shape
    qb = q.reshape(b, h, l // Q_BLOCK, Q_BLOCK, d).transpose(2, 0, 1, 3, 4)
    out = lax.map(lambda qi: attend(qi, k, v), qb)
    return out.transpose(1, 2, 0, 3, 4).reshape(b, h, l, V_HEAD)


def merge_groups(p, attn, gmlp_norm_g, spatial_w, spatial_b, pool_w, pool_scale, fourier_w, w_out):
    ya = chunk_spatial_gate(p[..., OFF_A:OFF_B], gmlp_norm_g, spatial_w, spatial_b)
    yb = multiscale_pool(p[..., OFF_B:OFF_CQ], pool_w, pool_scale)
    yd = fourier_mix(p[..., OFF_D:IN_W], fourier_w)
    return jnp.concatenate([ya, yb, attn, yd], axis=-1) @ w_out


def swiglu(h, w_gate, w_up, w_down):
    return (jax.nn.silu(h @ w_gate) * (h @ w_up)) @ w_down


def moe_swiglu(h, router_w, w_gate, w_up, w_down):
    b, l, d = h.shape
    t = h.reshape(b * l, d)
    logits = (t @ router_w).astype(jnp.float32)
    top_val, top_idx = lax.top_k(logits, TOP_K)
    top_w = jax.nn.softmax(top_val, axis=-1)
    gate = jnp.sum(jax.nn.one_hot(top_idx, N_EXPERTS, dtype=jnp.float32) * top_w[..., None], axis=1)
    out = jnp.zeros((b * l, d), jnp.float32)
    for e in range(N_EXPERTS):
        out = out + gate[:, e:e + 1] * swiglu(t, w_gate[e], w_up[e], w_down[e]).astype(jnp.float32)
    return out.astype(h.dtype).reshape(b, l, d)


def setup_inputs(seed: int = 0) -> dict:
    key = jax.random.key(seed)
    ks = iter(jax.random.split(key, 40))

    def nrm(shape, scale):
        return jax.random.normal(next(ks), shape, jnp.float32) * scale

    def gain(shape, noise=0.02):
        return 1.0 + nrm(shape, noise)

    D = D_MODEL
    return {
        'x': nrm((BATCH, SEQ, D), 1.0),
        'c': nrm((BATCH, D), 1.0),
        'ctx': nrm((BATCH, CTX_LEN, D), 1.0),
        'c_ctx': nrm((D,), 1.0),
        'ada_w': nrm((DEPTH, D, N_ADA * D), 0.5 * D ** -0.5),
        'ada_b': nrm((DEPTH, N_ADA * D), 0.02),
        'norm1_g': gain((DEPTH, D)),
        'w_in': nrm((DEPTH, D, IN_W), D ** -0.5),
        'gmlp_norm_g': gain((DEPTH, GROUP_W)),
        'spatial_w': nrm((DEPTH, N_GROUPS, CHUNK, CHUNK), CHUNK ** -0.5),
        'spatial_b': gain((DEPTH, N_GROUPS, CHUNK)),
        'pool_w': nrm((DEPTH, N_GROUPS, SUB_W, SUB_W), SUB_W ** -0.5),
        'pool_scale': gain((DEPTH, GROUP_W), 0.1),
        'q_norm_g': gain((DEPTH, Q_LORA)),
        'w_uq': nrm((DEPTH, Q_LORA, MLA_HEADS * QK_HEAD), Q_LORA ** -0.5),
        'kv_norm_g': gain((DEPTH, KV_LORA)),
        'w_ukv': nrm((DEPTH, KV_LORA, MLA_HEADS * (QK_NOPE + V_HEAD)), KV_LORA ** -0.5),
        'qk_q_g': gain((DEPTH, QK_HEAD)),
        'qk_k_g': gain((DEPTH, QK_HEAD)),
        'fourier_w': nrm((DEPTH, GROUP_W, GROUP_W), GROUP_W ** -0.5),
        'w_out': nrm((DEPTH, MIX_W, D), MIX_W ** -0.5),
        'norm2_g': gain((DEPTH, D)),
        'ffn_w_gate': nrm((N_DENSE, D, D_FF), D ** -0.5),
        'ffn_w_up': nrm((N_DENSE, D, D_FF), D ** -0.5),
        'ffn_w_down': nrm((N_DENSE, D_FF, D), D_FF ** -0.5),
        'router_w': nrm((N_MOE, D, N_EXPERTS), D ** -0.5),
        'moe_w_gate': nrm((N_MOE, N_EXPERTS, D, D_FF), D ** -0.5),
        'moe_w_up': nrm((N_MOE, N_EXPERTS, D, D_FF), D ** -0.5),
        'moe_w_down': nrm((N_MOE, N_EXPERTS, D_FF, D), D_FF ** -0.5),
    }


def reference(x, c, ctx, c_ctx, ada_w, ada_b, norm1_g, w_in, gmlp_norm_g, spatial_w, spatial_b,
              pool_w, pool_scale, q_norm_g, w_uq, kv_norm_g, w_ukv, qk_q_g, qk_k_g, fourier_w, w_out,
              norm2_g, ffn_w_gate, ffn_w_up, ffn_w_down, router_w, moe_w_gate, moe_w_up, moe_w_down):
    rope = axial_rope_tables(x.shape[1])
    c_act = jax.nn.silu(c)
    cc_act = jax.nn.silu(c_ctx)
    xc = ctx
    for i in range(DEPTH):
        last = i == DEPTH - 1
        sh1, sc1, g1, sh2, sc2, g2 = jnp.split((c_act @ ada_w[i] + ada_b[i])[:, None, :], N_ADA, axis=-1)
        csh1, csc1, cg1, csh2, csc2, cg2 = jnp.split(cc_act @ ada_w[i] + ada_b[i], N_ADA, axis=-1)
        group_w = (gmlp_norm_g[i], spatial_w[i], spatial_b[i], pool_w[i], pool_scale[i], fourier_w[i], w_out[i])
        if i % 2 == 0:
            j = i // 2
            ffn = functools.partial(swiglu, w_gate=ffn_w_gate[j], w_up=ffn_w_up[j], w_down=ffn_w_down[j])
        else:
            j = i // 2
            ffn = functools.partial(moe_swiglu, router_w=router_w[j], w_gate=moe_w_gate[j],
                                    w_up=moe_w_up[j], w_down=moe_w_down[j])

        p = modulate(x, norm1_g[i], sh1, sc1) @ w_in[i]
        pc = modulate(xc, norm1_g[i], csh1, csc1) @ w_in[i]
        kc, vc = mla_kv(pc, kv_norm_g[i], w_ukv[i], qk_k_g[i], None)
        k, v = mla_kv(p, kv_norm_g[i], w_ukv[i], qk_k_g[i], rope)
        q = mla_q(p, q_norm_g[i], w_uq[i], qk_q_g[i], rope)
        attn = heads_to_channels(blocked_attention(q, jnp.concatenate([kc, k], axis=2),
                                                   jnp.concatenate([vc, v], axis=2)))
        x_new = x + g1 * merge_groups(p, attn, *group_w)
        x_new = x_new + g2 * ffn(modulate(x_new, norm2_g[i], sh2, sc2))

        if not last:
            qc = mla_q(pc, q_norm_g[i], w_uq[i], qk_q_g[i], None)
            attn_c = heads_to_channels(attend(qc, kc, vc))
            xc = xc + cg1 * merge_groups(pc, attn_c, *group_w)
            xc = xc + cg2 * ffn(modulate(xc, norm2_g[i], csh2, csc2))
        x = x_new
    return x
```

```python
import functools

import jax
import jax.numpy as jnp
from jax import lax
from jax.experimental import pallas as pl
from jax.experimental.pallas import tpu as pltpu

F32 = jnp.float32
BF16 = jnp.bfloat16

D_MODEL = 1024
GROUP_W = 256
SUB_W = 64
N_GROUPS = 4
CHUNK = 128
POOL_WINDOWS = (2, 4, 8, 16)
POOL_HALO = 16
MLA_HEADS = 4
QK_NOPE = 64
QK_ROPE = 32
V_HEAD = 64
QK_HEAD = QK_NOPE + QK_ROPE
Q_LORA = 192
KV_LORA = 128
GRID_W = 64
ROPE_BASE = 10000.0
D_FF = 3584
N_EXPERTS = 8
TOP_K = 2
N_ADA = 6
EPS = 1e-6

OFF_A, OFF_B, OFF_CQ, OFF_CKV, OFF_CKR, OFF_D, IN_W = 0, 512, 768, 960, 1088, 1120, 1376
P_A, P_B, P_D, P_CKV, P_CQ, P_W = 0, 512, 768, 1024, 1152, 1408

LANES = 128
SUBLANES = 8
TOK_ROWS = D_MODEL // LANES
VMEM_LIMIT = 56 * 1024 * 1024


def _cparams(*sem):
    return pltpu.CompilerParams(dimension_semantics=sem, vmem_limit_bytes=VMEM_LIMIT)


def _rms(x, width=None):
    if width is None:
        return lax.rsqrt(jnp.mean(x * x, axis=-1, keepdims=True) + EPS)
    return lax.rsqrt(jnp.sum(x * x, axis=-1, keepdims=True) / width + EPS)


def _ada_kernel(c_ref, w_ref, b_ref, o_ref):
    ca = jax.nn.silu(c_ref[...])
    o_ref[...] = jnp.dot(ca, w_ref[...], preferred_element_type=F32, precision=lax.Precision.HIGHEST) + b_ref[...]


def _ada_call(c_all, ada_w, ada_b):
    depth, d, n = ada_w.shape
    rows = c_all.shape[0]
    tn = 1024
    return pl.pallas_call(
        _ada_kernel,
        grid=(depth, n // tn),
        in_specs=[
            pl.BlockSpec((rows, d), lambda i, j: (0, 0)),
            pl.BlockSpec((None, d, tn), lambda i, j: (i, 0, j)),
            pl.BlockSpec((None, 1, tn), lambda i, j: (i, 0, j)),
        ],
        out_specs=pl.BlockSpec((None, rows, tn), lambda i, j: (i, 0, j)),
        out_shape=jax.ShapeDtypeStruct((depth, rows, n), F32),
        compiler_params=_cparams("arbitrary", "arbitrary"),
        name="ada_mod",
    )(c_all, ada_w, ada_b.reshape(depth, 1, n))


def _inproj_kernel(x_ref, mod_ref, n1g_ref, win_ref, gg_ref, wcat_ref, bfull_ref, qng_ref, wq_ref, kvng_ref,
                   wkv_ref, caq_ref, cbq_ref, cak_ref, cbk_ref,
                   ya_ref, zb_ref, zd_ref, q_ref, k_ref, v_ref):
    tm = x_ref.shape[0]
    xt = x_ref[...]
    sh1 = mod_ref[0:1, :]
    sc1 = mod_ref[1:2, :]
    h = xt * _rms(xt) * n1g_ref[...] * (1.0 + sc1) + sh1
    p = jnp.dot(h.astype(BF16), win_ref[...], preferred_element_type=F32)

    uv = jax.nn.gelu(p[:, P_A:P_A + 2 * GROUP_W])
    u = uv[:, :GROUP_W]
    vv = uv[:, GROUP_W:]
    vn = vv * _rms(vv) * gg_ref[...]
    grp = lax.broadcasted_iota(jnp.int32, (CHUNK, GROUP_W), 1) // SUB_W
    for c in range(tm // CHUNK):
        rows = slice(c * CHUNK, (c + 1) * CHUNK)
        vc = vn[rows]
        vstack = jnp.concatenate([jnp.where(grp == g, vc, 0.0).astype(BF16) for g in range(N_GROUPS)], axis=0)
        mixed = jnp.dot(wcat_ref[...], vstack, preferred_element_type=F32) + bfull_ref[...]
        ya_ref[rows, :] = (u[rows] * mixed).astype(BF16)

    zb_ref[...] = p[:, P_B:P_B + GROUP_W].astype(BF16)
    zd_ref[...] = p[:, P_D:P_D + GROUP_W].astype(BF16)

    lane2 = lax.broadcasted_iota(jnp.int32, (tm, 2 * LANES), 1)
    cq = p[:, P_CQ:P_CQ + 2 * LANES]
    cq_ss = jnp.sum(jnp.where(lane2 < Q_LORA, cq * cq, 0.0), axis=-1, keepdims=True)
    xq = cq * lax.rsqrt(cq_ss / Q_LORA + EPS) * qng_ref[...]
    qraw = jnp.dot(xq.astype(BF16), wq_ref[...], preferred_element_type=F32)
    ckv = p[:, P_CKV:P_CKV + KV_LORA]
    xkv = ckv * _rms(ckv) * kvng_ref[...]
    kvraw = jnp.dot(xkv.astype(BF16), wkv_ref[...], preferred_element_type=F32)
    krope = p[:, P_CQ + LANES:P_CQ + 2 * LANES]
    lane = lax.broadcasted_iota(jnp.int32, (tm, LANES), 1)
    caq, cbq, cak, cbk = caq_ref[...], cbq_ref[...], cak_ref[...], cbk_ref[...]
    for hd in range(MLA_HEADS):
        cols = slice(hd * LANES, (hd + 1) * LANES)
        qb = qraw[:, cols]
        rq = lax.rsqrt(jnp.sum(jnp.where(lane < QK_HEAD, qb * qb, 0.0), axis=-1, keepdims=True) / QK_HEAD + EPS)
        q_ref[:, cols] = ((qb * caq + pltpu.roll(qb, LANES - QK_ROPE, 1) * cbq) * rq).astype(BF16)
        kb = jnp.where(lane < QK_NOPE, kvraw[:, cols], krope)
        rk = lax.rsqrt(jnp.sum(jnp.where(lane < QK_HEAD, kb * kb, 0.0), axis=-1, keepdims=True) / QK_HEAD + EPS)
        k_ref[:, cols] = ((kb * cak + pltpu.roll(kb, LANES - QK_ROPE, 1) * cbk) * rk).astype(BF16)
    v_ref[...] = kvraw[:, MLA_HEADS * LANES:].astype(BF16)


def _inproj_call(x, mod, shared_mod, wts, tabs, tm):
    bs, ls, d = x.shape
    full = lambda a: pl.BlockSpec(a.shape, lambda b, j: (0,) * a.ndim)
    mod_map = (lambda b, j: (0, 0, 0)) if shared_mod else (lambda b, j: (b, 0, 0))
    tab_spec = pl.BlockSpec((tm, LANES), lambda b, j: (j, 0))
    tok = lambda w: pl.BlockSpec((None, tm, w), lambda b, j: (b, j, 0))
    widths = (GROUP_W, GROUP_W, GROUP_W, MLA_HEADS * LANES, MLA_HEADS * LANES, MLA_HEADS * V_HEAD)
    return pl.pallas_call(
        _inproj_kernel,
        grid=(bs, ls // tm),
        in_specs=[tok(d), pl.BlockSpec((None, SUBLANES, d), mod_map)] + [full(w) for w in wts] + [tab_spec] * 4,
        out_specs=[tok(w) for w in widths],
        out_shape=[jax.ShapeDtypeStruct((bs, ls, w), BF16) for w in widths],
        compiler_params=_cparams("parallel", "parallel"),
        name="inproj",
    )(x, mod, *wts, *tabs)


def _attn_kernel(*refs, n_kv):
    q_ref = refs[0]
    k_refs = refs[1:1 + n_kv]
    v_refs = refs[1 + n_kv:1 + 2 * n_kv]
    o_ref = refs[-1]
    tq = q_ref.shape[0]
    head_of_lane = lax.broadcasted_iota(jnp.int32, (tq, MLA_HEADS * V_HEAD), 1) // V_HEAD
    out = jnp.zeros((tq, MLA_HEADS * V_HEAD), F32)
    for hd in range(MLA_HEADS):
        cols = slice(hd * LANES, (hd + 1) * LANES)
        qh = q_ref[:, cols]
        ss = [lax.dot_general(qh, kr[:, cols], (((1,), (1,)), ((), ())), preferred_element_type=F32)
              for kr in k_refs]
        m = functools.reduce(jnp.maximum, [jnp.max(s, axis=-1, keepdims=True) for s in ss])
        ps = [jnp.exp(s - m) for s in ss]
        den = functools.reduce(jnp.add, [jnp.sum(pp, axis=-1, keepdims=True) for pp in ps])
        o = functools.reduce(jnp.add, [jnp.dot(pp.astype(BF16), vr[...], preferred_element_type=F32)
                                       for pp, vr in zip(ps, v_refs)])
        out = jnp.where(head_of_lane == hd, o / den, out)
    o_ref[...] = out.astype(BF16)


def _attn_call(q, ks, vs, tq):
    bs, lq, _ = q.shape
    n_kv = len(ks)
    kv_spec = lambda a: pl.BlockSpec((None,) + a.shape[1:], lambda b, j: (b, 0, 0))
    return pl.pallas_call(
        functools.partial(_attn_kernel, n_kv=n_kv),
        grid=(bs, lq // tq),
        in_specs=[pl.BlockSpec((None, tq, q.shape[2]), lambda b, j: (b, j, 0))]
        + [kv_spec(a) for a in ks] + [kv_spec(a) for a in vs],
        out_specs=pl.BlockSpec((None, tq, MLA_HEADS * V_HEAD), lambda b, j: (b, j, 0)),
        out_shape=jax.ShapeDtypeStruct((bs, lq, MLA_HEADS * V_HEAD), BF16),
        compiler_params=_cparams("parallel", "arbitrary"),
        name="attention",
    )(q, *ks, *vs)


def _fourier_kernel(cs_ref, z_ref, cc_ref, scn_ref, wf_ref, o_ref):
    tr = o_ref.shape[0]
    t = jnp.dot(cs_ref[...], z_ref[...], preferred_element_type=F32)
    y = (jnp.dot(t[:tr].astype(BF16), cc_ref[...], preferred_element_type=F32)
         + jnp.dot(t[tr:].astype(BF16), scn_ref[...], preferred_element_type=F32))
    o_ref[...] = jnp.dot(y.astype(BF16), wf_ref[...], preferred_element_type=F32).astype(BF16)


def _fourier_call(zd, cs, cc, scn, wf, tr):
    bs, ls, w = zd.shape
    full = lambda a: pl.BlockSpec(a.shape, lambda r, b: (0,) * a.ndim)
    return pl.pallas_call(
        _fourier_kernel,
        grid=(ls // tr, bs),
        in_specs=[pl.BlockSpec((None, 2 * tr, ls), lambda r, b: (r, 0, 0)),
                  pl.BlockSpec((None, ls, w), lambda r, b: (b, 0, 0)), full(cc), full(scn), full(wf)],
        out_specs=pl.BlockSpec((None, tr, w), lambda r, b: (b, r, 0)),
        out_shape=jax.ShapeDtypeStruct((bs, ls, w), BF16),
        compiler_params=_cparams("parallel", "arbitrary"),
        name="fourier",
    )(cs, zd, cc, scn, wf)


def _merge_kernel(*refs, seq_len, tm, tok_layout, router):
    (x_ref, ya_ref, zb_ref, zbp_ref, zbn_ref, at_ref, yd_ref, mod_ref, pw_ref, ps_ref, wo_ref, n2g_ref) = refs[:12]
    rest = refs[12:]
    if router:
        rw_ref, rest = rest[0], rest[1:]
    xn_ref, h_ref = rest[0], rest[1]
    j = pl.program_id(1)
    nj = pl.num_programs(1)

    zm = zb_ref[...].astype(F32)
    zp = jnp.where(j > 0, zbp_ref[...].astype(F32), 0.0)
    zn = jnp.where(j < nj - 1, zbn_ref[...].astype(F32), 0.0)
    ext = jnp.concatenate([zp, zm, zn], axis=0)
    n = tm + 2 * POOL_HALO

    def ahead(a, k):
        return pltpu.roll(a, n - k, 0)

    d2 = ext + ahead(ext, 1)
    d4 = d2 + ahead(d2, 2)
    d8 = d4 + ahead(d4, 4)
    d16 = d8 + ahead(d8, 8)
    sums = [ahead(d, POOL_HALO - w // 2)[:tm] for d, w in zip((d2, d4, d8, d16), POOL_WINDOWS)]
    tg = j * tm + lax.broadcasted_iota(jnp.int32, (tm, 1), 0)
    grp = lax.broadcasted_iota(jnp.int32, (tm, GROUP_W), 1) // SUB_W
    pooled = jnp.zeros((tm, GROUP_W), F32)
    for g, (s, w) in enumerate(zip(sums, POOL_WINDOWS)):
        cnt = jnp.minimum(tg - w // 2 + w, seq_len) - jnp.maximum(tg - w // 2, 0)
        pooled = jnp.where(grp == g, s / cnt.astype(F32), pooled)
    diff = pooled - zm
    yb = jnp.dot(diff.astype(BF16), pw_ref[...], preferred_element_type=F32) * ps_ref[...]

    acc = jnp.dot(ya_ref[...], wo_ref[0], preferred_element_type=F32)
    acc += jnp.dot(yb.astype(BF16), wo_ref[1], preferred_element_type=F32)
    acc += jnp.dot(at_ref[...], wo_ref[2], preferred_element_type=F32)
    acc += jnp.dot(yd_ref[...], wo_ref[3], preferred_element_type=F32)
    g1 = mod_ref[2:3, :]
    sh2 = mod_ref[3:4, :]
    sc2 = mod_ref[4:5, :]
    xn = x_ref[...] + g1 * acc
    xn_ref[...] = xn
    h2 = xn * _rms(xn) * n2g_ref[...] * (1.0 + sc2) + sh2
    if tok_layout:
        for jj in range(TOK_ROWS):
            h_ref[pl.ds(jj, tm, stride=TOK_ROWS), :] = h2[:, jj * LANES:(jj + 1) * LANES]
    else:
        h_ref[...] = h2.astype(BF16)
    if router:
        lg_ref = rest[2]
        lg_ref[...] = jnp.dot(h2, rw_ref[...], preferred_element_type=F32, precision=lax.Precision.HIGHEST)


def _merge_call(x, ya, zb, attn, yd, mod, shared_mod, wts, tm, tok_layout, router_w=None):
    bs, ls, d = x.shape
    nj = ls // tm
    hb = tm // POOL_HALO
    full = lambda a: pl.BlockSpec(a.shape, lambda b, j: (0,) * a.ndim)
    mod_map = (lambda b, j: (0, 0, 0)) if shared_mod else (lambda b, j: (b, 0, 0))
    tok = lambda w: pl.BlockSpec((None, tm, w), lambda b, j: (b, j, 0))
    in_specs = [
        tok(d), tok(GROUP_W), tok(GROUP_W),
        pl.BlockSpec((None, POOL_HALO, GROUP_W), lambda b, j: (b, jnp.maximum(j * hb - 1, 0), 0)),
        pl.BlockSpec((None, POOL_HALO, GROUP_W), lambda b, j: (b, jnp.minimum((j + 1) * hb, nj * hb - 1), 0)),
        tok(GROUP_W), tok(GROUP_W),
        pl.BlockSpec((None, SUBLANES, d), mod_map),
    ] + [full(w) for w in wts]
    args = [x, ya, zb, zb, zb, attn, yd, mod, *wts]
    out_specs = [tok(d)]
    out_shape = [jax.ShapeDtypeStruct((bs, ls, d), F32)]
    if tok_layout:
        out_specs.append(pl.BlockSpec((tm * TOK_ROWS, LANES), lambda b, j: (b * nj + j, 0)))
        out_shape.append(jax.ShapeDtypeStruct((bs * ls * TOK_ROWS, LANES), F32))
    else:
        out_specs.append(tok(d))
        out_shape.append(jax.ShapeDtypeStruct((bs, ls, d), BF16))
    if router_w is not None:
        in_specs.append(full(router_w))
        args.append(router_w)
        out_specs.append(tok(LANES))
        out_shape.append(jax.ShapeDtypeStruct((bs, ls, LANES), F32))
    return pl.pallas_call(
        functools.partial(_merge_kernel, seq_len=ls, tm=tm, tok_layout=tok_layout, router=router_w is not None),
        grid=(bs, nj),
        in_specs=in_specs,
        out_specs=out_specs,
        out_shape=out_shape,
        compiler_params=_cparams("parallel", "parallel"),
        name="merge",
    )(*args)


def _ffn_kernel(te_ref, tv_ref, *refs, tm, tok_in, tok_out, residual):
    x_ref, wg_ref, wu_ref, wd_ref = refs[:4]
    rest = refs[4:]
    if residual:
        res_ref, mod_ref, rest = rest[0], rest[1], rest[2:]
    o_ref, acc_ref = rest[0], rest[1]
    i = pl.program_id(0)
    f = pl.program_id(1)
    nf = pl.num_programs(1)
    valid = tv_ref[i] > 0

    def store(y):
        if tok_out:
            for jj in range(TOK_ROWS):
                o_ref[pl.ds(jj, tm, stride=TOK_ROWS), :] = y[:, jj * LANES:(jj + 1) * LANES]
        else:
            o_ref[...] = y

    @pl.when(valid)
    def _():
        if tok_in:
            xb_ref = rest[2]

            @pl.when(f == 0)
            def _():
                for jj in range(TOK_ROWS):
                    xb_ref[:, jj * LANES:(jj + 1) * LANES] = x_ref[pl.ds(jj, tm, stride=TOK_ROWS), :].astype(BF16)

            xb = xb_ref[...]
        else:
            xb = x_ref[...]
        gate = jnp.dot(xb, wg_ref[...], preferred_element_type=F32)
        up = jnp.dot(xb, wu_ref[...], preferred_element_type=F32)
        act = (jax.nn.silu(gate) * up).astype(BF16)
        part = jnp.dot(act, wd_ref[...], preferred_element_type=F32)

        @pl.when(f == 0)
        def _():
            acc_ref[...] = part

        @pl.when(f > 0)
        def _():
            acc_ref[...] += part

        @pl.when(f == nf - 1)
        def _():
            y = acc_ref[...]
            if residual:
                y = res_ref[...] + mod_ref[5:6, :] * y
            store(y)

    @pl.when(jnp.logical_and(jnp.logical_not(valid), f == nf - 1))
    def _():
        store(jnp.zeros((tm, D_MODEL), F32))


def _ffn_call(x, tile_e, tile_v, wg, wu, wd, tm, tf, tok_in, tok_out, res=None, mod=None, tiles_per_mod=None):
    n_tiles = tile_e.shape[0]
    d, dff = wg.shape[1], wg.shape[2]
    nf = dff // tf
    last = nf - 1
    fsel = lambda i, f, te, tv: jnp.where(tv[i] > 0, f, last)
    x_spec = (pl.BlockSpec((tm * TOK_ROWS, LANES), lambda i, f, te, tv: (i, 0)) if tok_in
              else pl.BlockSpec((tm, d), lambda i, f, te, tv: (i, 0)))
    in_specs = [
        x_spec,
        pl.BlockSpec((None, d, tf), lambda i, f, te, tv: (te[i], 0, fsel(i, f, te, tv))),
        pl.BlockSpec((None, d, tf), lambda i, f, te, tv: (te[i], 0, fsel(i, f, te, tv))),
        pl.BlockSpec((None, tf, d), lambda i, f, te, tv: (te[i], fsel(i, f, te, tv), 0)),
    ]
    args = [x, wg, wu, wd]
    residual = res is not None
    if residual:
        in_specs.append(pl.BlockSpec((tm, d), lambda i, f, te, tv: (i, 0)))
        in_specs.append(pl.BlockSpec((None, SUBLANES, d), lambda i, f, te, tv: (i // tiles_per_mod, 0, 0)))
        args += [res, mod]
    if tok_out:
        out_spec = pl.BlockSpec((tm * TOK_ROWS, LANES), lambda i, f, te, tv: (i, 0))
        out_shape = jax.ShapeDtypeStruct((n_tiles * tm * TOK_ROWS, LANES), F32)
    else:
        out_spec = pl.BlockSpec((tm, d), lambda i, f, te, tv: (i, 0))
        out_shape = jax.ShapeDtypeStruct((n_tiles * tm, d), F32)
    scratch = [pltpu.VMEM((tm, d), F32)]
    if tok_in:
        scratch.append(pltpu.VMEM((tm, d), BF16))
    return pl.pallas_call(
        functools.partial(_ffn_kernel, tm=tm, tok_in=tok_in, tok_out=tok_out, residual=residual),
        grid_spec=pltpu.PrefetchScalarGridSpec(
            num_scalar_prefetch=2, grid=(n_tiles, nf), in_specs=in_specs, out_specs=out_spec,
            scratch_shapes=scratch),
        out_shape=out_shape,
        compiler_params=_cparams("parallel", "arbitrary"),
        name="ffn",
    )(tile_e, tile_v, *args)


def _route_kernel(lg_ref, o_ref, cnt_ref, carry_ref):
    tm = lg_ref.shape[0]
    i = pl.program_id(0)

    @pl.when(i == 0)
    def _():
        carry_ref[...] = jnp.zeros_like(carry_ref)

    lane_i = lax.broadcasted_iota(jnp.int32, (tm, LANES), 1)
    lane = lane_i.astype(F32)
    neg = jnp.float32(-jnp.inf)
    lg = jnp.where(lane_i < N_EXPERTS, lg_ref[...], neg)
    m1 = jnp.max(lg, axis=-1, keepdims=True)
    i1 = jnp.min(jnp.where(lg == m1, lane, float(LANES)), axis=-1, keepdims=True)
    lg2 = jnp.where(lane == i1, neg, lg)
    m2 = jnp.max(lg2, axis=-1, keepdims=True)
    i2 = jnp.min(jnp.where(lg2 == m2, lane, float(LANES)), axis=-1, keepdims=True)
    e2 = jnp.exp(m2 - m1)
    w1 = 1.0 / (1.0 + e2)
    w2 = e2 / (1.0 + e2)
    hit = jnp.logical_or(lane == i1, lane == i2)
    onehot = jnp.where(hit, 1.0, 0.0).astype(BF16)
    r = lax.broadcasted_iota(jnp.int32, (tm, tm), 0)
    c = lax.broadcasted_iota(jnp.int32, (tm, tm), 1)
    before = jnp.where(c < r, 1.0, 0.0).astype(BF16)
    carry = carry_ref[0:1, :]
    cum = jnp.dot(before, onehot, preferred_element_type=F32) + carry
    rank1 = jnp.sum(jnp.where(lane == i1, cum, 0.0), axis=-1, keepdims=True)
    rank2 = jnp.sum(jnp.where(lane == i2, cum, 0.0), axis=-1, keepdims=True)
    total = carry + jnp.sum(onehot.astype(F32), axis=0, keepdims=True)
    carry_ref[...] = jnp.broadcast_to(total, carry_ref.shape)
    cnt_ref[...] = jnp.broadcast_to(total, cnt_ref.shape)
    out = jnp.zeros((tm, LANES), F32)
    for col, val in enumerate((i1, i2, w1, w2, rank1, rank2)):
        out = jnp.where(lane_i == col, val, out)
    o_ref[...] = out


def _route_call(logits, tm):
    n = logits.shape[0]
    return pl.pallas_call(
        _route_kernel,
        grid=(n // tm,),
        in_specs=[pl.BlockSpec((tm, LANES), lambda i: (i, 0))],
        out_specs=[pl.BlockSpec((tm, LANES), lambda i: (i, 0)), pl.BlockSpec((SUBLANES, LANES), lambda i: (0, 0))],
        out_shape=[jax.ShapeDtypeStruct((n, LANES), F32), jax.ShapeDtypeStruct((SUBLANES, LANES), F32)],
        scratch_shapes=[pltpu.VMEM((SUBLANES, LANES), F32)],
        compiler_params=_cparams("arbitrary"),
        name="route",
    )(logits)


def _token_rows(tok):
    start = tok * TOK_ROWS
    return pl.ds(start if isinstance(start, int) else pl.multiple_of(start, TOK_ROWS), TOK_ROWS)


def _token_copy(src, src_tok, dst, dst_tok, sem):
    return pltpu.make_async_copy(src.at[_token_rows(src_tok), :], dst.at[_token_rows(dst_tok), :], sem)


def _dispatch_kernel(pos_ref, h_hbm, zero_hbm, xs_hbm, sem, *, td):
    del zero_hbm
    i = pl.program_id(0)

    def issue(t, carry):
        for s in range(TOP_K):
            _token_copy(h_hbm, i * td + t, xs_hbm, pos_ref[0, TOP_K * t + s], sem).start()
        return carry

    lax.fori_loop(0, td, issue, 0)

    def drain(t, carry):
        _token_copy(h_hbm, 0, xs_hbm, 0, sem).wait()
        return carry

    lax.fori_loop(0, TOP_K * td, drain, 0)


def _dispatch_call(pos, h_tok, n_slots, td):
    n = pos.shape[0]
    zeros = jnp.zeros((n_slots * TOK_ROWS, LANES), F32)
    return pl.pallas_call(
        functools.partial(_dispatch_kernel, td=td),
        grid=(n // td,),
        in_specs=[pl.BlockSpec((None, 1, TOP_K * td), lambda i: (i, 0, 0), memory_space=pltpu.SMEM),
                  pl.BlockSpec(memory_space=pl.ANY), pl.BlockSpec(memory_space=pl.ANY)],
        out_specs=pl.BlockSpec(memory_space=pl.ANY),
        out_shape=jax.ShapeDtypeStruct(zeros.shape, F32),
        scratch_shapes=[pltpu.SemaphoreType.DMA],
        input_output_aliases={2: 0},
        compiler_params=_cparams("arbitrary"),
        name="dispatch",
    )(pos.reshape(n // td, 1, TOP_K * td), h_tok, zeros)


def _combine_kernel(pos_ref, y_hbm, xn_ref, route_ref, mod_ref, o_ref, buf, sem, *, tc):
    def issue(t, carry):
        for s in range(TOP_K):
            _token_copy(y_hbm, pos_ref[0, TOP_K * t + s], buf, TOP_K * t + s, sem).start()
        return carry

    lax.fori_loop(0, tc, issue, 0)

    def drain(t, carry):
        _token_copy(y_hbm, 0, buf, 0, sem).wait()
        return carry

    lax.fori_loop(0, TOP_K * tc, drain, 0)

    g2 = mod_ref[5:6, :]
    w1 = route_ref[:, 2:3]
    w2 = route_ref[:, 3:4]
    for jj in range(TOK_ROWS):
        cols = slice(jj * LANES, (jj + 1) * LANES)
        ya = buf[pl.ds(jj, tc, stride=TOP_K * TOK_ROWS), :]
        yb = buf[pl.ds(TOK_ROWS + jj, tc, stride=TOP_K * TOK_ROWS), :]
        o_ref[:, cols] = xn_ref[:, cols] + g2[:, cols] * (w1 * ya + w2 * yb)


def _combine_call(pos, y_tok, xn, route, mod, seq_len, tc):
    n, d = xn.shape
    return pl.pallas_call(
        functools.partial(_combine_kernel, tc=tc),
        grid=(n // tc,),
        in_specs=[pl.BlockSpec((None, 1, TOP_K * tc), lambda i: (i, 0, 0), memory_space=pltpu.SMEM),
                  pl.BlockSpec(memory_space=pl.ANY),
                  pl.BlockSpec((tc, d), lambda i: (i, 0)),
                  pl.BlockSpec((tc, LANES), lambda i: (i, 0)),
                  pl.BlockSpec((None, SUBLANES, d), lambda i: (i * tc // seq_len, 0, 0))],
        out_specs=pl.BlockSpec((tc, d), lambda i: (i, 0)),
        out_shape=jax.ShapeDtypeStruct((n, d), F32),
        scratch_shapes=[pltpu.VMEM((TOP_K * tc * TOK_ROWS, LANES), F32), pltpu.SemaphoreType.DMA],
        compiler_params=_cparams("arbitrary"),
        name="combine",
    )(pos.reshape(n // tc, 1, TOP_K * tc), y_tok, xn, route, mod)


def _rot(w):
    return jnp.concatenate([-w[..., 8:16], w[..., 0:8], -w[..., 24:32], w[..., 16:24]], axis=-1)


def _swap(g):
    return jnp.concatenate([g[..., 8:16], g[..., 0:8], g[..., 24:32], g[..., 16:24]], axis=-1)


def _rope_angles(length):
    rows = length // GRID_W
    row = jnp.repeat(jnp.arange(rows, dtype=F32), GRID_W)
    col = jnp.tile(jnp.arange(GRID_W, dtype=F32), rows)
    n_freq = QK_ROPE // 4
    inv_freq = ROPE_BASE ** (-jnp.arange(n_freq, dtype=F32) / n_freq)
    ar = row[:, None] * inv_freq
    ac = col[:, None] * inv_freq
    cos = jnp.concatenate([jnp.cos(ar), jnp.cos(ar), jnp.cos(ac), jnp.cos(ac)], axis=1)
    sin = jnp.concatenate([jnp.sin(ar), jnp.sin(ar), jnp.sin(ac), jnp.sin(ac)], axis=1)
    return cos, sin


def _rope_tables(gain, length, cos, sin, scale):
    g_nope = jnp.broadcast_to(gain[:QK_NOPE], (length, QK_NOPE))
    g_rope = gain[QK_NOPE:]
    pad = jnp.zeros((length, LANES - QK_HEAD), F32)
    ca = jnp.concatenate([g_nope, g_rope * cos, pad], axis=1) * scale
    cb = jnp.concatenate([jnp.zeros((length, QK_NOPE), F32), _swap(g_rope) * sin, pad], axis=1) * scale
    return ca, cb


def _dft_tables(length, tr):
    k = jnp.arange(length, dtype=jnp.int32)
    ang = ((k[:, None] * k[None, :]) % length).astype(F32) * (2.0 * jnp.pi / length)
    norm = length ** -0.5
    c = (jnp.cos(ang) * norm).astype(BF16).reshape(length // tr, tr, length)
    s = (jnp.sin(ang) * norm).astype(BF16).reshape(length // tr, tr, length)
    cs = jnp.concatenate([c, s], axis=1)
    kc = jnp.arange(SUB_W, dtype=jnp.int32)
    angc = ((kc[:, None] * kc[None, :]) % SUB_W).astype(F32) * (2.0 * jnp.pi / SUB_W)
    eye = jnp.eye(N_GROUPS, dtype=F32)
    cc = jnp.kron(eye, jnp.cos(angc) * SUB_W ** -0.5).astype(BF16)
    scn = jnp.kron(eye, -jnp.sin(angc) * SUB_W ** -0.5).astype(BF16)
    return cs, cc, scn


def _layer_weights(i, w_in, gmlp_norm_g, spatial_w, spatial_b, pool_w, pool_scale, q_norm_g, w_uq, kv_norm_g,
                   w_ukv, fourier_w, w_out, norm1_g, norm2_g):
    wi = w_in[i]
    ckr = wi[:, OFF_CKR:OFF_D]
    win = jnp.concatenate([wi[:, OFF_A:OFF_B], wi[:, OFF_B:OFF_CQ], wi[:, OFF_D:IN_W], wi[:, OFF_CKV:OFF_CKR],
                           wi[:, OFF_CQ:OFF_CKV], ckr, _rot(ckr)], axis=1).astype(BF16)
    wcat = spatial_w[i].transpose(1, 0, 2).reshape(CHUNK, N_GROUPS * CHUNK).astype(BF16)
    bfull = jnp.repeat(spatial_b[i].T, SUB_W, axis=1)
    qng = jnp.concatenate([q_norm_g[i], jnp.zeros((2 * LANES - Q_LORA,), F32)])[None]
    wq4 = w_uq[i].reshape(Q_LORA, MLA_HEADS, QK_HEAD)
    wq = jnp.concatenate([wq4, _rot(wq4[..., QK_NOPE:])], axis=-1).reshape(Q_LORA, MLA_HEADS * LANES)
    wq = jnp.concatenate([wq, jnp.zeros((2 * LANES - Q_LORA, MLA_HEADS * LANES), F32)], axis=0).astype(BF16)
    wkv4 = w_ukv[i].reshape(KV_LORA, MLA_HEADS, QK_NOPE + V_HEAD)
    wk = jnp.concatenate([wkv4[..., :QK_NOPE], jnp.zeros((KV_LORA, MLA_HEADS, LANES - QK_NOPE), F32)], axis=-1)
    wkv = jnp.concatenate([wk.reshape(KV_LORA, MLA_HEADS * LANES),
                           wkv4[..., QK_NOPE:].reshape(KV_LORA, MLA_HEADS * V_HEAD)], axis=1).astype(BF16)
    in_w = (norm1_g[i][None], win, gmlp_norm_g[i][None], wcat, bfull, qng, wq, kv_norm_g[i][None], wkv)
    pw = jnp.zeros((GROUP_W, GROUP_W), F32)
    for g in range(N_GROUPS):
        pw = pw.at[g * SUB_W:(g + 1) * SUB_W, g * SUB_W:(g + 1) * SUB_W].set(pool_w[i, g])
    merge_w = (pw.astype(BF16), pool_scale[i][None], w_out[i].reshape(N_GROUPS, GROUP_W, D_MODEL).astype(BF16),
               norm2_g[i][None])
    return in_w, merge_w, fourier_w[i].astype(BF16)


def _pad_mod(m):
    m = m.reshape(m.shape[:-1] + (N_ADA, D_MODEL))
    return jnp.concatenate([m, jnp.zeros(m.shape[:-2] + (SUBLANES - N_ADA, D_MODEL), F32)], axis=-2)


def _tiles(seq_len):
    tm = min(seq_len, 512)
    return dict(tm=tm, tq=min(seq_len, 512), tr=min(seq_len, 512))


def kernel(x, c, ctx, c_ctx, ada_w, ada_b, norm1_g, w_in, gmlp_norm_g, spatial_w, spatial_b, pool_w, pool_scale,
           q_norm_g, w_uq, kv_norm_g, w_ukv, qk_q_g, qk_k_g, fourier_w, w_out, norm2_g, ffn_w_gate, ffn_w_up,
           ffn_w_down, router_w, moe_w_gate, moe_w_up, moe_w_down):
    bsz, seq, d = x.shape
    lc = ctx.shape[1]
    depth = ada_w.shape[0]
    n_tok = bsz * seq
    tl, tcx = _tiles(seq), _tiles(lc)
    ffn_tm, ffn_tmc, ffn_tf = min(seq, 1024), min(bsz * lc, 1024), 512
    moe_tm = 512

    c_all = jnp.concatenate([c, c_ctx[None], jnp.zeros((SUBLANES - 1, d), F32)], axis=0)
    mods = _ada_call(c_all, ada_w, ada_b)
    cos, sin = _rope_angles(seq)
    ones, zeros = jnp.ones((lc, QK_ROPE), F32), jnp.zeros((lc, QK_ROPE), F32)
    dft_l = _dft_tables(seq, tl["tr"])
    dft_c = _dft_tables(lc, tcx["tr"])

    xc = ctx
    for i in range(depth):
        last = i == depth - 1
        moe = i % 2 == 1
        mod_l = _pad_mod(mods[i, :bsz])
        mod_c = _pad_mod(mods[i, bsz:bsz + 1])
        in_w, merge_w, wf = _layer_weights(i, w_in, gmlp_norm_g, spatial_w, spatial_b, pool_w, pool_scale,
                                           q_norm_g, w_uq, kv_norm_g, w_ukv, fourier_w, w_out, norm1_g, norm2_g)
        scale = QK_HEAD ** -0.5
        tabs_l = _rope_tables(qk_q_g[i], seq, cos, sin, scale) + _rope_tables(qk_k_g[i], seq, cos, sin, 1.0)
        tabs_c = _rope_tables(qk_q_g[i], lc, ones, zeros, scale) + _rope_tables(qk_k_g[i], lc, ones, zeros, 1.0)

        ya, zb, zd, q, k, v = _inproj_call(x, mod_l, False, in_w, tabs_l, tl["tm"])
        yac, zbc, zdc, qc, kc, vc = _inproj_call(xc, mod_c, True, in_w, tabs_c, tcx["tm"])
        attn = _attn_call(q, [kc, k], [vc, v], tl["tq"])
        yd = _fourier_call(zd, *dft_l, wf, tl["tr"])

        if not moe:
            j = i // 2
            wg, wu, wd = (ffn_w_gate[j:j + 1].astype(BF16), ffn_w_up[j:j + 1].astype(BF16),
                          ffn_w_down[j:j + 1].astype(BF16))
            xn, h2 = _merge_call(x, ya, zb, attn, yd, mod_l, False, merge_w, tl["tm"], False)
            n_tiles = n_tok // ffn_tm
            x = _ffn_call(h2.reshape(n_tok, d), jnp.zeros((n_tiles,), jnp.int32), jnp.ones((n_tiles,), jnp.int32),
                          wg, wu, wd, ffn_tm, ffn_tf, False, False, res=xn.reshape(n_tok, d), mod=mod_l,
                          tiles_per_mod=seq // ffn_tm).reshape(bsz, seq, d)
            if not last:
                attn_c = _attn_call(qc, [kc], [vc], tcx["tq"])
                ydc = _fourier_call(zdc, *dft_c, wf, tcx["tr"])
                xnc, h2c = _merge_call(xc, yac, zbc, attn_c, ydc, mod_c, True, merge_w, tcx["tm"], False)
                nc_tok = bsz * lc
                nct = nc_tok // ffn_tmc
                xc = _ffn_call(h2c.reshape(nc_tok, d), jnp.zeros((nct,), jnp.int32), jnp.ones((nct,), jnp.int32),
                               wg, wu, wd, ffn_tmc, ffn_tf, False, False, res=xnc.reshape(nc_tok, d), mod=mod_c,
                               tiles_per_mod=nct).reshape(bsz, lc, d)
        else:
            j = i // 2
            wg, wu, wd = moe_w_gate[j].astype(BF16), moe_w_up[j].astype(BF16), moe_w_down[j].astype(BF16)
            rw = jnp.concatenate([router_w[j], jnp.zeros((d, LANES - N_EXPERTS), F32)], axis=1)

            def moe_ffn(xs, ya_, zb_, attn_, yd_, mod_, shared, tms):
                b_, l_, _ = xs.shape
                n_ = b_ * l_
                xn, h_tok, logits = _merge_call(xs, ya_, zb_, attn_, yd_, mod_, shared, merge_w, tms, True, rw)
                route, counts = _route_call(logits.reshape(n_, LANES), min(n_, 512))
                eidx = route[:, 0:TOP_K].astype(jnp.int32)
                rank = route[:, 4:4 + TOP_K].astype(jnp.int32)
                cnt = counts[0, :N_EXPERTS].astype(jnp.int32)
                padded = (cnt + moe_tm - 1) // moe_tm * moe_tm
                ends = jnp.cumsum(padded)
                starts = ends - padded
                onehot = eidx[..., None] == jnp.arange(N_EXPERTS, dtype=jnp.int32)
                pos = jnp.sum(jnp.where(onehot, starts, 0), axis=-1) + rank
                n_tiles = TOP_K * n_ // moe_tm + N_EXPERTS
                tile_row = jnp.arange(n_tiles, dtype=jnp.int32) * moe_tm
                tile_e = jnp.minimum(jnp.sum(tile_row[:, None] >= ends[None, :], axis=1), N_EXPERTS - 1)
                tile_v = (tile_row < ends[-1]).astype(jnp.int32)
                xs_tok = _dispatch_call(pos, h_tok, n_tiles * moe_tm, min(n_, 512))
                y_tok = _ffn_call(xs_tok, tile_e.astype(jnp.int32), tile_v, wg, wu, wd, moe_tm, ffn_tf, True, True)
                mod_full = jnp.broadcast_to(mod_, (b_, SUBLANES, d)) if shared else mod_
                out = _combine_call(pos, y_tok, xn.reshape(n_, d), route, mod_full, l_, min(l_, 256))
                return out.reshape(b_, l_, d)

            x_new = moe_ffn(x, ya, zb, attn, yd, mod_l, False, tl["tm"])
            if not last:
                attn_c = _attn_call(qc, [kc], [vc], tcx["tq"])
                ydc = _fourier_call(zdc, *dft_c, wf, tcx["tr"])
                xc = moe_ffn(xc, yac, zbc, attn_c, ydc, mod_c, True, tcx["tm"])
            x = x_new
    return x
```

```python
import functools

import jax
import jax.numpy as jnp
from jax import lax
from jax.experimental import pallas as pl
from jax.experimental.pallas import tpu as pltpu

F32 = jnp.float32
BF16 = jnp.bfloat16

D_MODEL = 1024
GROUP_W = 256
SUB_W = 64
N_GROUPS = 4
CHUNK = 128
POOL_WINDOWS = (2, 4, 8, 16)
POOL_HALO = 16
MLA_HEADS = 4
QK_NOPE = 64
QK_ROPE = 32
V_HEAD = 64
QK_HEAD = QK_NOPE + QK_ROPE
Q_LORA = 192
KV_LORA = 128
GRID_W = 64
ROPE_BASE = 10000.0
D_FF = 3584
N_EXPERTS = 8
TOP_K = 2
N_ADA = 6
EPS = 1e-6

OFF_A, OFF_B, OFF_CQ, OFF_CKV, OFF_CKR, OFF_D, IN_W = 0, 512, 768, 960, 1088, 1120, 1376
P_A, P_B, P_D, P_CKV, P_CQ, P_W = 0, 512, 768, 1024, 1152, 1408

LANES = 128
SUBLANES = 8
TOK_ROWS = D_MODEL // LANES
VMEM_LIMIT = 56 * 1024 * 1024


def _cparams(*sem):
    return pltpu.CompilerParams(dimension_semantics=sem, vmem_limit_bytes=VMEM_LIMIT)


def _rms(x, width=None):
    if width is None:
        return lax.rsqrt(jnp.mean(x * x, axis=-1, keepdims=True) + EPS)
    return lax.rsqrt(jnp.sum(x * x, axis=-1, keepdims=True) / width + EPS)


def _ada_kernel(c_ref, w_ref, b_ref, o_ref):
    ca = jax.nn.silu(c_ref[...])
    o_ref[...] = jnp.dot(ca, w_ref[...], preferred_element_type=F32, precision=lax.Precision.HIGHEST) + b_ref[...]


def _ada_call(c_all, ada_w, ada_b):
    depth, d, n = ada_w.shape
    rows = c_all.shape[0]
    tn = 1024
    return pl.pallas_call(
        _ada_kernel,
        grid=(depth, n // tn),
        in_specs=[
            pl.BlockSpec((rows, d), lambda i, j: (0, 0)),
            pl.BlockSpec((None, d, tn), lambda i, j: (i, 0, j)),
            pl.BlockSpec((None, 1, tn), lambda i, j: (i, 0, j)),
        ],
        out_specs=pl.BlockSpec((None, rows, tn), lambda i, j: (i, 0, j)),
        out_shape=jax.ShapeDtypeStruct((depth, rows, n), F32),
        compiler_params=_cparams("arbitrary", "arbitrary"),
        name="ada_mod",
    )(c_all, ada_w, ada_b.reshape(depth, 1, n))


def _inproj_kernel(x_ref, mod_ref, n1g_ref, win_ref, gg_ref, wcat_ref, bfull_ref, qng_ref, wq_ref, kvng_ref,
                   wkv_ref, caq_ref, cbq_ref, cak_ref, cbk_ref,
                   ya_ref, zb_ref, zd_ref, q_ref, k_ref, v_ref):
    tm = x_ref.shape[0]
    xt = x_ref[...]
    sh1 = mod_ref[0:1, :]
    sc1 = mod_ref[1:2, :]
    h = xt * _rms(xt) * n1g_ref[...] * (1.0 + sc1) + sh1
    p = jnp.dot(h.astype(BF16), win_ref[...], preferred_element_type=F32)

    uv = jax.nn.gelu(p[:, P_A:P_A + 2 * GROUP_W])
    u = uv[:, :GROUP_W]
    vv = uv[:, GROUP_W:]
    vn = vv * _rms(vv) * gg_ref[...]
    grp = lax.broadcasted_iota(jnp.int32, (CHUNK, GROUP_W), 1) // SUB_W
    for c in range(tm // CHUNK):
        rows = slice(c * CHUNK, (c + 1) * CHUNK)
        vc = vn[rows]
        vstack = jnp.concatenate([jnp.where(grp == g, vc, 0.0).astype(BF16) for g in range(N_GROUPS)], axis=0)
        mixed = jnp.dot(wcat_ref[...], vstack, preferred_element_type=F32) + bfull_ref[...]
        ya_ref[rows, :] = (u[rows] * mixed).astype(BF16)

    zb_ref[...] = p[:, P_B:P_B + GROUP_W].astype(BF16)
    zd_ref[...] = p[:, P_D:P_D + GROUP_W].astype(BF16)

    lane2 = lax.broadcasted_iota(jnp.int32, (tm, 2 * LANES), 1)
    cq = p[:, P_CQ:P_CQ + 2 * LANES]
    cq_ss = jnp.sum(jnp.where(lane2 < Q_LORA, cq * cq, 0.0), axis=-1, keepdims=True)
    xq = cq * lax.rsqrt(cq_ss / Q_LORA + EPS) * qng_ref[...]
    qraw = jnp.dot(xq.astype(BF16), wq_ref[...], preferred_element_type=F32)
    ckv = p[:, P_CKV:P_CKV + KV_LORA]
    xkv = ckv * _rms(ckv) * kvng_ref[...]
    kvraw = jnp.dot(xkv.astype(BF16), wkv_ref[...], preferred_element_type=F32)
    krope = p[:, P_CQ + LANES:P_CQ + 2 * LANES]
    lane = lax.broadcasted_iota(jnp.int32, (tm, LANES), 1)
    caq, cbq, cak, cbk = caq_ref[...], cbq_ref[...], cak_ref[...], cbk_ref[...]
    for hd in range(MLA_HEADS):
        cols = slice(hd * LANES, (hd + 1) * LANES)
        qb = qraw[:, cols]
        rq = lax.rsqrt(jnp.sum(jnp.where(lane < QK_HEAD, qb * qb, 0.0), axis=-1, keepdims=True) / QK_HEAD + EPS)
        q_ref[:, cols] = ((qb * caq + pltpu.roll(qb, LANES - QK_ROPE, 1) * cbq) * rq).astype(BF16)
        kb = jnp.where(lane < QK_NOPE, kvraw[:, cols], krope)
        rk = lax.rsqrt(jnp.sum(jnp.where(lane < QK_HEAD, kb * kb, 0.0), axis=-1, keepdims=True) / QK_HEAD + EPS)
        k_ref[:, cols] = ((kb * cak + pltpu.roll(kb, LANES - QK_ROPE, 1) * cbk) * rk).astype(BF16)
    v_ref[...] = kvraw[:, MLA_HEADS * LANES:].astype(BF16)


def _inproj_call(x, mod, shared_mod, wts, tabs, tm):
    bs, ls, d = x.shape
    full = lambda a: pl.BlockSpec(a.shape, lambda b, j: (0,) * a.ndim)
    mod_map = (lambda b, j: (0, 0, 0)) if shared_mod else (lambda b, j: (b, 0, 0))
    tab_spec = pl.BlockSpec((tm, LANES), lambda b, j: (j, 0))
    tok = lambda w: pl.BlockSpec((None, tm, w), lambda b, j: (b, j, 0))
    widths = (GROUP_W, GROUP_W, GROUP_W, MLA_HEADS * LANES, MLA_HEADS * LANES, MLA_HEADS * V_HEAD)
    return pl.pallas_call(
        _inproj_kernel,
        grid=(bs, ls // tm),
        in_specs=[tok(d), pl.BlockSpec((None, SUBLANES, d), mod_map)] + [full(w) for w in wts] + [tab_spec] * 4,
        out_specs=[tok(w) for w in widths],
        out_shape=[jax.ShapeDtypeStruct((bs, ls, w), BF16) for w in widths],
        compiler_params=_cparams("parallel", "parallel"),
        name="inproj",
    )(x, mod, *wts, *tabs)


def _attn_kernel(*refs, n_kv):
    q_ref = refs[0]
    k_refs = refs[1:1 + n_kv]
    v_refs = refs[1 + n_kv:1 + 2 * n_kv]
    o_ref = refs[-1]
    tq = q_ref.shape[0]
    head_of_lane = lax.broadcasted_iota(jnp.int32, (tq, MLA_HEADS * V_HEAD), 1) // V_HEAD
    out = jnp.zeros((tq, MLA_HEADS * V_HEAD), F32)
    for hd in range(MLA_HEADS):
        cols = slice(hd * LANES, (hd + 1) * LANES)
        qh = q_ref[:, cols]
        ss = [lax.dot_general(qh, kr[:, cols], (((1,), (1,)), ((), ())), preferred_element_type=F32)
              for kr in k_refs]
        m = functools.reduce(jnp.maximum, [jnp.max(s, axis=-1, keepdims=True) for s in ss])
        ps = [jnp.exp(s - m) for s in ss]
        den = functools.reduce(jnp.add, [jnp.sum(pp, axis=-1, keepdims=True) for pp in ps])
        o = functools.reduce(jnp.add, [jnp.dot(pp.astype(BF16), vr[...], preferred_element_type=F32)
                                       for pp, vr in zip(ps, v_refs)])
        out = jnp.where(head_of_lane == hd, o / den, out)
    o_ref[...] = out.astype(BF16)


def _attn_call(q, ks, vs, tq):
    bs, lq, _ = q.shape
    n_kv = len(ks)
    kv_spec = lambda a: pl.BlockSpec((None,) + a.shape[1:], lambda b, j: (b, 0, 0))
    return pl.pallas_call(
        functools.partial(_attn_kernel, n_kv=n_kv),
        grid=(bs, lq // tq),
        in_specs=[pl.BlockSpec((None, tq, q.shape[2]), lambda b, j: (b, j, 0))]
        + [kv_spec(a) for a in ks] + [kv_spec(a) for a in vs],
        out_specs=pl.BlockSpec((None, tq, MLA_HEADS * V_HEAD), lambda b, j: (b, j, 0)),
        out_shape=jax.ShapeDtypeStruct((bs, lq, MLA_HEADS * V_HEAD), BF16),
        compiler_params=_cparams("parallel", "arbitrary"),
        name="attention",
    )(q, *ks, *vs)


def _fourier_kernel(cs_ref, z_ref, cc_ref, scn_ref, wf_ref, o_ref):
    tr = o_ref.shape[0]
    t = jnp.dot(cs_ref[...], z_ref[...], preferred_element_type=F32)
    y = (jnp.dot(t[:tr].astype(BF16), cc_ref[...], preferred_element_type=F32)
         + jnp.dot(t[tr:].astype(BF16), scn_ref[...], preferred_element_type=F32))
    o_ref[...] = jnp.dot(y.astype(BF16), wf_ref[...], preferred_element_type=F32).astype(BF16)


def _fourier_call(zd, cs, cc, scn, wf, tr):
    bs, ls, w = zd.shape
    full = lambda a: pl.BlockSpec(a.shape, lambda r, b: (0,) * a.ndim)
    return pl.pallas_call(
        _fourier_kernel,
        grid=(ls // tr, bs),
        in_specs=[pl.BlockSpec((None, 2 * tr, ls), lambda r, b: (r, 0, 0)),
                  pl.BlockSpec((None, ls, w), lambda r, b: (b, 0, 0)), full(cc), full(scn), full(wf)],
        out_specs=pl.BlockSpec((None, tr, w), lambda r, b: (b, r, 0)),
        out_shape=jax.ShapeDtypeStruct((bs, ls, w), BF16),
        compiler_params=_cparams("parallel", "arbitrary"),
        name="fourier",
    )(cs, zd, cc, scn, wf)


def _merge_kernel(*refs, seq_len, tm, tok_layout, router):
    (x_ref, ya_ref, zb_ref, zbp_ref, zbn_ref, at_ref, yd_ref, mod_ref, pw_ref, ps_ref, wo_ref, n2g_ref) = refs[:12]
    rest = refs[12:]
    if router:
        rw_ref, rest = rest[0], rest[1:]
    xn_ref, h_ref = rest[0], rest[1]
    j = pl.program_id(1)
    nj = pl.num_programs(1)

    zm = zb_ref[...].astype(F32)
    zp = jnp.where(j > 0, zbp_ref[...].astype(F32), 0.0)
    zn = jnp.where(j < nj - 1, zbn_ref[...].astype(F32), 0.0)
    ext = jnp.concatenate([zp, zm, zn], axis=0)
    n = tm + 2 * POOL_HALO

    def ahead(a, k):
        return pltpu.roll(a, n - k, 0)

    d2 = ext + ahead(ext, 1)
    d4 = d2 + ahead(d2, 2)
    d8 = d4 + ahead(d4, 4)
    d16 = d8 + ahead(d8, 8)
    sums = [ahead(d, POOL_HALO - w // 2)[:tm] for d, w in zip((d2, d4, d8, d16), POOL_WINDOWS)]
    tg = j * tm + lax.broadcasted_iota(jnp.int32, (tm, 1), 0)
    grp = lax.broadcasted_iota(jnp.int32, (tm, GROUP_W), 1) // SUB_W
    pooled = jnp.zeros((tm, GROUP_W), F32)
    for g, (s, w) in enumerate(zip(sums, POOL_WINDOWS)):
        cnt = jnp.minimum(tg - w // 2 + w, seq_len) - jnp.maximum(tg - w // 2, 0)
        pooled = jnp.where(grp == g, s / cnt.astype(F32), pooled)
    diff = pooled - zm
    yb = jnp.dot(diff.astype(BF16), pw_ref[...], preferred_element_type=F32) * ps_ref[...]

    acc = jnp.dot(ya_ref[...], wo_ref[0], preferred_element_type=F32)
    acc += jnp.dot(yb.astype(BF16), wo_ref[1], preferred_element_type=F32)
    acc += jnp.dot(at_ref[...], wo_ref[2], preferred_element_type=F32)
    acc += jnp.dot(yd_ref[...], wo_ref[3], preferred_element_type=F32)
    g1 = mod_ref[2:3, :]
    sh2 = mod_ref[3:4, :]
    sc2 = mod_ref[4:5, :]
    xn = x_ref[...] + g1 * acc
    xn_ref[...] = xn
    h2 = xn * _rms(xn) * n2g_ref[...] * (1.0 + sc2) + sh2
    if tok_layout:
        for jj in range(TOK_ROWS):
            h_ref[pl.ds(jj, tm, stride=TOK_ROWS), :] = h2[:, jj * LANES:(jj + 1) * LANES]
    else:
        h_ref[...] = h2.astype(BF16)
    if router:
        lg_ref = rest[2]
        h_hi = h2.astype(BF16)
        h_lo = (h2 - h_hi.astype(F32)).astype(BF16)
        lg_ref[...] = (jnp.dot(h_hi, rw_ref[0], preferred_element_type=F32)
                       + (jnp.dot(h_lo, rw_ref[0], preferred_element_type=F32)
                          + jnp.dot(h_hi, rw_ref[1], preferred_element_type=F32)))


def _merge_call(x, ya, zb, attn, yd, mod, shared_mod, wts, tm, tok_layout, router_w=None):
    bs, ls, d = x.shape
    nj = ls // tm
    hb = tm // POOL_HALO
    full = lambda a: pl.BlockSpec(a.shape, lambda b, j: (0,) * a.ndim)
    mod_map = (lambda b, j: (0, 0, 0)) if shared_mod else (lambda b, j: (b, 0, 0))
    tok = lambda w: pl.BlockSpec((None, tm, w), lambda b, j: (b, j, 0))
    in_specs = [
        tok(d), tok(GROUP_W), tok(GROUP_W),
        pl.BlockSpec((None, POOL_HALO, GROUP_W), lambda b, j: (b, jnp.maximum(j * hb - 1, 0), 0)),
        pl.BlockSpec((None, POOL_HALO, GROUP_W), lambda b, j: (b, jnp.minimum((j + 1) * hb, nj * hb - 1), 0)),
        tok(GROUP_W), tok(GROUP_W),
        pl.BlockSpec((None, SUBLANES, d), mod_map),
    ] + [full(w) for w in wts]
    args = [x, ya, zb, zb, zb, attn, yd, mod, *wts]
    out_specs = [tok(d)]
    out_shape = [jax.ShapeDtypeStruct((bs, ls, d), F32)]
    if tok_layout:
        out_specs.append(pl.BlockSpec((tm * TOK_ROWS, LANES), lambda b, j: (b * nj + j, 0)))
        out_shape.append(jax.ShapeDtypeStruct((bs * ls * TOK_ROWS, LANES), F32))
    else:
        out_specs.append(tok(d))
        out_shape.append(jax.ShapeDtypeStruct((bs, ls, d), BF16))
    if router_w is not None:
        in_specs.append(full(router_w))
        args.append(router_w)
        out_specs.append(tok(LANES))
        out_shape.append(jax.ShapeDtypeStruct((bs, ls, LANES), F32))
    return pl.pallas_call(
        functools.partial(_merge_kernel, seq_len=ls, tm=tm, tok_layout=tok_layout, router=router_w is not None),
        grid=(bs, nj),
        in_specs=in_specs,
        out_specs=out_specs,
        out_shape=out_shape,
        compiler_params=_cparams("parallel", "parallel"),
        name="merge",
    )(*args)


ACC_COLS = 256


def _ffn_kernel(te_ref, tv_ref, *refs, tm, tok_in, tok_out, residual):
    x_ref, wg_ref, wu_ref, wd_ref = refs[:4]
    rest = refs[4:]
    if residual:
        res_ref, mod_ref, rest = rest[0], rest[1], rest[2:]
    o_ref, acc_ref = rest[0], rest[1]
    i = pl.program_id(0)
    f = pl.program_id(1)
    nf = pl.num_programs(1)
    valid = tv_ref[i] > 0

    def store(y):
        if tok_out:
            for jj in range(TOK_ROWS):
                o_ref[pl.ds(jj, tm, stride=TOK_ROWS), :] = y[:, jj * LANES:(jj + 1) * LANES]
        else:
            o_ref[...] = y

    @pl.when(valid)
    def _():
        @pl.when(f == 0)
        def _():
            acc_ref[...] = jnp.zeros_like(acc_ref)
            if tok_in:
                for jj in range(TOK_ROWS):
                    rest[2][:, jj * LANES:(jj + 1) * LANES] = x_ref[pl.ds(jj, tm, stride=TOK_ROWS), :].astype(BF16)

        xb = rest[2][...] if tok_in else x_ref[...]
        gate = jnp.dot(xb, wg_ref[...], preferred_element_type=F32)
        up = jnp.dot(xb, wu_ref[...], preferred_element_type=F32)
        act = (jax.nn.silu(gate) * up).astype(BF16)
        for cb in range(D_MODEL // ACC_COLS):
            cols = slice(cb * ACC_COLS, (cb + 1) * ACC_COLS)
            acc_ref[:, cols] += jnp.dot(act, wd_ref[:, cols], preferred_element_type=F32)

        @pl.when(f == nf - 1)
        def _():
            y = acc_ref[...]
            if residual:
                y = res_ref[...] + mod_ref[5:6, :] * y
            store(y)

    @pl.when(jnp.logical_and(jnp.logical_not(valid), f == nf - 1))
    def _():
        store(jnp.zeros((tm, D_MODEL), F32))


def _ffn_call(x, tile_e, tile_v, wg, wu, wd, tm, tf, tok_in, tok_out, res=None, mod=None, tiles_per_mod=None):
    n_tiles = tile_e.shape[0]
    d, dff = wg.shape[1], wg.shape[2]
    nf = dff // tf
    last = nf - 1
    fsel = lambda i, f, te, tv: jnp.where(tv[i] > 0, f, last)
    x_spec = (pl.BlockSpec((tm * TOK_ROWS, LANES), lambda i, f, te, tv: (i, 0)) if tok_in
              else pl.BlockSpec((tm, d), lambda i, f, te, tv: (i, 0)))
    in_specs = [
        x_spec,
        pl.BlockSpec((None, d, tf), lambda i, f, te, tv: (te[i], 0, fsel(i, f, te, tv))),
        pl.BlockSpec((None, d, tf), lambda i, f, te, tv: (te[i], 0, fsel(i, f, te, tv))),
        pl.BlockSpec((None, tf, d), lambda i, f, te, tv: (te[i], fsel(i, f, te, tv), 0)),
    ]
    args = [x, wg, wu, wd]
    residual = res is not None
    if residual:
        in_specs.append(pl.BlockSpec((tm, d), lambda i, f, te, tv: (i, 0)))
        in_specs.append(pl.BlockSpec((None, SUBLANES, d), lambda i, f, te, tv: (i // tiles_per_mod, 0, 0)))
        args += [res, mod]
    if tok_out:
        out_spec = pl.BlockSpec((tm * TOK_ROWS, LANES), lambda i, f, te, tv: (i, 0))
        out_shape = jax.ShapeDtypeStruct((n_tiles * tm * TOK_ROWS, LANES), F32)
    else:
        out_spec = pl.BlockSpec((tm, d), lambda i, f, te, tv: (i, 0))
        out_shape = jax.ShapeDtypeStruct((n_tiles * tm, d), F32)
    scratch = [pltpu.VMEM((tm, d), F32)]
    if tok_in:
        scratch.append(pltpu.VMEM((tm, d), BF16))
    return pl.pallas_call(
        functools.partial(_ffn_kernel, tm=tm, tok_in=tok_in, tok_out=tok_out, residual=residual),
        grid_spec=pltpu.PrefetchScalarGridSpec(
            num_scalar_prefetch=2, grid=(n_tiles, nf), in_specs=in_specs, out_specs=out_spec,
            scratch_shapes=scratch),
        out_shape=out_shape,
        compiler_params=_cparams("parallel", "arbitrary"),
        name="ffn",
    )(tile_e, tile_v, *args)


def _route_kernel(lg_ref, o_ref, cnt_ref, carry_ref):
    tm = lg_ref.shape[0]
    i = pl.program_id(0)

    @pl.when(i == 0)
    def _():
        carry_ref[...] = jnp.zeros_like(carry_ref)

    lane_i = lax.broadcasted_iota(jnp.int32, (tm, LANES), 1)
    lane = lane_i.astype(F32)
    neg = jnp.float32(-jnp.inf)
    lg = jnp.where(lane_i < N_EXPERTS, lg_ref[...], neg)
    m1 = jnp.max(lg, axis=-1, keepdims=True)
    i1 = jnp.min(jnp.where(lg == m1, lane, float(LANES)), axis=-1, keepdims=True)
    lg2 = jnp.where(lane == i1, neg, lg)
    m2 = jnp.max(lg2, axis=-1, keepdims=True)
    i2 = jnp.min(jnp.where(lg2 == m2, lane, float(LANES)), axis=-1, keepdims=True)
    e2 = jnp.exp(m2 - m1)
    w1 = 1.0 / (1.0 + e2)
    w2 = e2 / (1.0 + e2)
    hit = jnp.logical_or(lane == i1, lane == i2)
    onehot = jnp.where(hit, 1.0, 0.0).astype(BF16)
    r = lax.broadcasted_iota(jnp.int32, (tm, tm), 0)
    c = lax.broadcasted_iota(jnp.int32, (tm, tm), 1)
    before = jnp.where(c < r, 1.0, 0.0).astype(BF16)
    carry = carry_ref[0:1, :]
    cum = jnp.dot(before, onehot, preferred_element_type=F32) + carry
    rank1 = jnp.sum(jnp.where(lane == i1, cum, 0.0), axis=-1, keepdims=True)
    rank2 = jnp.sum(jnp.where(lane == i2, cum, 0.0), axis=-1, keepdims=True)
    total = carry + jnp.sum(onehot.astype(F32), axis=0, keepdims=True)
    carry_ref[...] = jnp.broadcast_to(total, carry_ref.shape)
    cnt_ref[...] = jnp.broadcast_to(total, cnt_ref.shape)
    out = jnp.zeros((tm, LANES), F32)
    for col, val in enumerate((i1, i2, w1, w2, rank1, rank2)):
        out = jnp.where(lane_i == col, val, out)
    o_ref[...] = out


def _route_call(logits, tm):
    n = logits.shape[0]
    return pl.pallas_call(
        _route_kernel,
        grid=(n // tm,),
        in_specs=[pl.BlockSpec((tm, LANES), lambda i: (i, 0))],
        out_specs=[pl.BlockSpec((tm, LANES), lambda i: (i, 0)), pl.BlockSpec((SUBLANES, LANES), lambda i: (0, 0))],
        out_shape=[jax.ShapeDtypeStruct((n, LANES), F32), jax.ShapeDtypeStruct((SUBLANES, LANES), F32)],
        scratch_shapes=[pltpu.VMEM((SUBLANES, LANES), F32)],
        compiler_params=_cparams("arbitrary"),
        name="route",
    )(logits)


ISSUE_UNROLL = 8


def _token_rows(tok):
    start = tok * TOK_ROWS
    return pl.ds(start if isinstance(start, int) else pl.multiple_of(start, TOK_ROWS), TOK_ROWS)


def _token_copy(src, src_tok, dst, dst_tok, sem):
    return pltpu.make_async_copy(src.at[_token_rows(src_tok), :], dst.at[_token_rows(dst_tok), :], sem)


def _for_each_token(n_tok, body):
    def group(g, carry):
        for u in range(ISSUE_UNROLL):
            body(g * ISSUE_UNROLL + u)
        return carry

    lax.fori_loop(0, n_tok // ISSUE_UNROLL, group, 0)


def _dispatch_kernel(pos_ref, h_ref, zero_hbm, xs_hbm, sem, *, td):
    del zero_hbm

    def issue(t):
        for s in range(TOP_K):
            _token_copy(h_ref, t, xs_hbm, pos_ref[0, TOP_K * t + s], sem).start()

    _for_each_token(td, issue)
    for _ in range(TOP_K):
        pltpu.make_async_copy(h_ref, xs_hbm.at[pl.ds(0, td * TOK_ROWS), :], sem).wait()


def _dispatch_call(pos, h_tok, n_slots, td):
    n = pos.shape[0]
    zeros = jnp.zeros((n_slots * TOK_ROWS, LANES), F32)
    return pl.pallas_call(
        functools.partial(_dispatch_kernel, td=td),
        grid=(n // td,),
        in_specs=[pl.BlockSpec((None, 1, TOP_K * td), lambda i: (i, 0, 0), memory_space=pltpu.SMEM),
                  pl.BlockSpec((td * TOK_ROWS, LANES), lambda i: (i, 0)), pl.BlockSpec(memory_space=pl.ANY)],
        out_specs=pl.BlockSpec(memory_space=pl.ANY),
        out_shape=jax.ShapeDtypeStruct(zeros.shape, F32),
        scratch_shapes=[pltpu.SemaphoreType.DMA],
        input_output_aliases={2: 0},
        compiler_params=_cparams("arbitrary"),
        name="dispatch",
    )(pos.reshape(n // td, 1, TOP_K * td), h_tok, zeros)


def _combine_kernel(pos_ref, y_hbm, xn_ref, route_ref, mod_ref, o_ref, buf, sem, *, tc):
    def issue(t):
        for s in range(TOP_K):
            _token_copy(y_hbm, pos_ref[0, TOP_K * t + s], buf, TOP_K * t + s, sem).start()

    _for_each_token(tc, issue)
    pltpu.make_async_copy(y_hbm.at[pl.ds(0, TOP_K * tc * TOK_ROWS), :], buf, sem).wait()

    g2 = mod_ref[5:6, :]
    w1 = route_ref[:, 2:3]
    w2 = route_ref[:, 3:4]
    for jj in range(TOK_ROWS):
        cols = slice(jj * LANES, (jj + 1) * LANES)
        ya = buf[pl.ds(jj, tc, stride=TOP_K * TOK_ROWS), :]
        yb = buf[pl.ds(TOK_ROWS + jj, tc, stride=TOP_K * TOK_ROWS), :]
        o_ref[:, cols] = xn_ref[:, cols] + g2[:, cols] * (w1 * ya + w2 * yb)


def _combine_call(pos, y_tok, xn, route, mod, seq_len, tc):
    n, d = xn.shape
    return pl.pallas_call(
        functools.partial(_combine_kernel, tc=tc),
        grid=(n // tc,),
        in_specs=[pl.BlockSpec((None, 1, TOP_K * tc), lambda i: (i, 0, 0), memory_space=pltpu.SMEM),
                  pl.BlockSpec(memory_space=pl.ANY),
                  pl.BlockSpec((tc, d), lambda i: (i, 0)),
                  pl.BlockSpec((tc, LANES), lambda i: (i, 0)),
                  pl.BlockSpec((None, SUBLANES, d), lambda i: (i * tc // seq_len, 0, 0))],
        out_specs=pl.BlockSpec((tc, d), lambda i: (i, 0)),
        out_shape=jax.ShapeDtypeStruct((n, d), F32),
        scratch_shapes=[pltpu.VMEM((TOP_K * tc * TOK_ROWS, LANES), F32), pltpu.SemaphoreType.DMA],
        compiler_params=_cparams("arbitrary"),
        name="combine",
    )(pos.reshape(n // tc, 1, TOP_K * tc), y_tok, xn, route, mod)


def _rot(w):
    return jnp.concatenate([-w[..., 8:16], w[..., 0:8], -w[..., 24:32], w[..., 16:24]], axis=-1)


def _swap(g):
    return jnp.concatenate([g[..., 8:16], g[..., 0:8], g[..., 24:32], g[..., 16:24]], axis=-1)


def _rope_angles(length):
    rows = length // GRID_W
    row = jnp.repeat(jnp.arange(rows, dtype=F32), GRID_W)
    col = jnp.tile(jnp.arange(GRID_W, dtype=F32), rows)
    n_freq = QK_ROPE // 4
    inv_freq = ROPE_BASE ** (-jnp.arange(n_freq, dtype=F32) / n_freq)
    ar = row[:, None] * inv_freq
    ac = col[:, None] * inv_freq
    cos = jnp.concatenate([jnp.cos(ar), jnp.cos(ar), jnp.cos(ac), jnp.cos(ac)], axis=1)
    sin = jnp.concatenate([jnp.sin(ar), jnp.sin(ar), jnp.sin(ac), jnp.sin(ac)], axis=1)
    return cos, sin


def _rope_tables(gain, length, cos, sin, scale):
    g_nope = jnp.broadcast_to(gain[:QK_NOPE], (length, QK_NOPE))
    g_rope = gain[QK_NOPE:]
    pad = jnp.zeros((length, LANES - QK_HEAD), F32)
    ca = jnp.concatenate([g_nope, g_rope * cos, pad], axis=1) * scale
    cb = jnp.concatenate([jnp.zeros((length, QK_NOPE), F32), _swap(g_rope) * sin, pad], axis=1) * scale
    return ca, cb


def _dft_tables(length, tr):
    k = jnp.arange(length, dtype=jnp.int32)
    ang = ((k[:, None] * k[None, :]) % length).astype(F32) * (2.0 * jnp.pi / length)
    norm = length ** -0.5
    c = (jnp.cos(ang) * norm).astype(BF16).reshape(length // tr, tr, length)
    s = (jnp.sin(ang) * norm).astype(BF16).reshape(length // tr, tr, length)
    cs = jnp.concatenate([c, s], axis=1)
    kc = jnp.arange(SUB_W, dtype=jnp.int32)
    angc = ((kc[:, None] * kc[None, :]) % SUB_W).astype(F32) * (2.0 * jnp.pi / SUB_W)
    eye = jnp.eye(N_GROUPS, dtype=F32)
    cc = jnp.kron(eye, jnp.cos(angc) * SUB_W ** -0.5).astype(BF16)
    scn = jnp.kron(eye, -jnp.sin(angc) * SUB_W ** -0.5).astype(BF16)
    return cs, cc, scn


def _layer_weights(i, w_in, gmlp_norm_g, spatial_w, spatial_b, pool_w, pool_scale, q_norm_g, w_uq, kv_norm_g,
                   w_ukv, fourier_w, w_out, norm1_g, norm2_g):
    wi = w_in[i]
    ckr = wi[:, OFF_CKR:OFF_D]
    win = jnp.concatenate([wi[:, OFF_A:OFF_B], wi[:, OFF_B:OFF_CQ], wi[:, OFF_D:IN_W], wi[:, OFF_CKV:OFF_CKR],
                           wi[:, OFF_CQ:OFF_CKV], ckr, _rot(ckr)], axis=1).astype(BF16)
    wcat = spatial_w[i].transpose(1, 0, 2).reshape(CHUNK, N_GROUPS * CHUNK).astype(BF16)
    bfull = jnp.repeat(spatial_b[i].T, SUB_W, axis=1)
    qng = jnp.concatenate([q_norm_g[i], jnp.zeros((2 * LANES - Q_LORA,), F32)])[None]
    wq4 = w_uq[i].reshape(Q_LORA, MLA_HEADS, QK_HEAD)
    wq = jnp.concatenate([wq4, _rot(wq4[..., QK_NOPE:])], axis=-1).reshape(Q_LORA, MLA_HEADS * LANES)
    wq = jnp.concatenate([wq, jnp.zeros((2 * LANES - Q_LORA, MLA_HEADS * LANES), F32)], axis=0).astype(BF16)
    wkv4 = w_ukv[i].reshape(KV_LORA, MLA_HEADS, QK_NOPE + V_HEAD)
    wk = jnp.concatenate([wkv4[..., :QK_NOPE], jnp.zeros((KV_LORA, MLA_HEADS, LANES - QK_NOPE), F32)], axis=-1)
    wkv = jnp.concatenate([wk.reshape(KV_LORA, MLA_HEADS * LANES),
                           wkv4[..., QK_NOPE:].reshape(KV_LORA, MLA_HEADS * V_HEAD)], axis=1).astype(BF16)
    in_w = (norm1_g[i][None], win, gmlp_norm_g[i][None], wcat, bfull, qng, wq, kv_norm_g[i][None], wkv)
    pw = jnp.zeros((GROUP_W, GROUP_W), F32)
    for g in range(N_GROUPS):
        pw = pw.at[g * SUB_W:(g + 1) * SUB_W, g * SUB_W:(g + 1) * SUB_W].set(pool_w[i, g])
    merge_w = (pw.astype(BF16), pool_scale[i][None], w_out[i].reshape(N_GROUPS, GROUP_W, D_MODEL).astype(BF16),
               norm2_g[i][None])
    return in_w, merge_w, fourier_w[i].astype(BF16)


def _pad_mod(m):
    m = m.reshape(m.shape[:-1] + (N_ADA, D_MODEL))
    return jnp.concatenate([m, jnp.zeros(m.shape[:-2] + (SUBLANES - N_ADA, D_MODEL), F32)], axis=-2)


def _tiles(seq_len):
    tm = min(seq_len, 512)
    return dict(tm=tm, tq=min(seq_len, 512), tr=min(seq_len, 512))


def kernel(x, c, ctx, c_ctx, ada_w, ada_b, norm1_g, w_in, gmlp_norm_g, spatial_w, spatial_b, pool_w, pool_scale,
           q_norm_g, w_uq, kv_norm_g, w_ukv, qk_q_g, qk_k_g, fourier_w, w_out, norm2_g, ffn_w_gate, ffn_w_up,
           ffn_w_down, router_w, moe_w_gate, moe_w_up, moe_w_down):
    bsz, seq, d = x.shape
    lc = ctx.shape[1]
    depth = ada_w.shape[0]
    n_tok = bsz * seq
    tl, tcx = _tiles(seq), _tiles(lc)
    ffn_tm, ffn_tmc, ffn_tf = min(seq, 1024), min(bsz * lc, 1024), 512
    moe_tm = 1024

    c_all = jnp.concatenate([c, c_ctx[None], jnp.zeros((SUBLANES - 1, d), F32)], axis=0)
    mods = _ada_call(c_all, ada_w, ada_b)
    cos, sin = _rope_angles(seq)
    ones, zeros = jnp.ones((lc, QK_ROPE), F32), jnp.zeros((lc, QK_ROPE), F32)
    dft_l = _dft_tables(seq, tl["tr"])
    dft_c = _dft_tables(lc, tcx["tr"])

    xc = ctx
    for i in range(depth):
        last = i == depth - 1
        moe = i % 2 == 1
        mod_l = _pad_mod(mods[i, :bsz])
        mod_c = _pad_mod(mods[i, bsz:bsz + 1])
        in_w, merge_w, wf = _layer_weights(i, w_in, gmlp_norm_g, spatial_w, spatial_b, pool_w, pool_scale,
                                           q_norm_g, w_uq, kv_norm_g, w_ukv, fourier_w, w_out, norm1_g, norm2_g)
        scale = QK_HEAD ** -0.5
        tabs_l = _rope_tables(qk_q_g[i], seq, cos, sin, scale) + _rope_tables(qk_k_g[i], seq, cos, sin, 1.0)
        tabs_c = _rope_tables(qk_q_g[i], lc, ones, zeros, scale) + _rope_tables(qk_k_g[i], lc, ones, zeros, 1.0)

        ya, zb, zd, q, k, v = _inproj_call(x, mod_l, False, in_w, tabs_l, tl["tm"])
        yac, zbc, zdc, qc, kc, vc = _inproj_call(xc, mod_c, True, in_w, tabs_c, tcx["tm"])
        attn = _attn_call(q, [kc, k], [vc, v], tl["tq"])
        yd = _fourier_call(zd, *dft_l, wf, tl["tr"])

        if not moe:
            j = i // 2
            wg, wu, wd = (ffn_w_gate[j:j + 1].astype(BF16), ffn_w_up[j:j + 1].astype(BF16),
                          ffn_w_down[j:j + 1].astype(BF16))
            xn, h2 = _merge_call(x, ya, zb, attn, yd, mod_l, False, merge_w, tl["tm"], False)
            n_tiles = n_tok // ffn_tm
            x = _ffn_call(h2.reshape(n_tok, d), jnp.zeros((n_tiles,), jnp.int32), jnp.ones((n_tiles,), jnp.int32),
                          wg, wu, wd, ffn_tm, ffn_tf, False, False, res=xn.reshape(n_tok, d), mod=mod_l,
                          tiles_per_mod=seq // ffn_tm).reshape(bsz, seq, d)
            if not last:
                attn_c = _attn_call(qc, [kc], [vc], tcx["tq"])
                ydc = _fourier_call(zdc, *dft_c, wf, tcx["tr"])
                xnc, h2c = _merge_call(xc, yac, zbc, attn_c, ydc, mod_c, True, merge_w, tcx["tm"], False)
                nc_tok = bsz * lc
                nct = nc_tok // ffn_tmc
                xc = _ffn_call(h2c.reshape(nc_tok, d), jnp.zeros((nct,), jnp.int32), jnp.ones((nct,), jnp.int32),
                               wg, wu, wd, ffn_tmc, ffn_tf, False, False, res=xnc.reshape(nc_tok, d), mod=mod_c,
                               tiles_per_mod=nct).reshape(bsz, lc, d)
        else:
            j = i // 2
            wg, wu, wd = moe_w_gate[j].astype(BF16), moe_w_up[j].astype(BF16), moe_w_down[j].astype(BF16)
            rw32 = jnp.concatenate([router_w[j], jnp.zeros((d, LANES - N_EXPERTS), F32)], axis=1)
            rw_hi = rw32.astype(BF16)
            rw = jnp.stack([rw_hi, (rw32 - rw_hi.astype(F32)).astype(BF16)])

            def moe_ffn(xs, ya_, zb_, attn_, yd_, mod_, shared, tms):
                b_, l_, _ = xs.shape
                n_ = b_ * l_
                xn, h_tok, logits = _merge_call(xs, ya_, zb_, attn_, yd_, mod_, shared, merge_w, tms, True, rw)
                route, counts = _route_call(logits.reshape(n_, LANES), min(n_, 512))
                eidx = route[:, 0:TOP_K].astype(jnp.int32)
                rank = route[:, 4:4 + TOP_K].astype(jnp.int32)
                cnt = counts[0, :N_EXPERTS].astype(jnp.int32)
                padded = (cnt + moe_tm - 1) // moe_tm * moe_tm
                ends = jnp.cumsum(padded)
                starts = ends - padded
                onehot = eidx[..., None] == jnp.arange(N_EXPERTS, dtype=jnp.int32)
                pos = jnp.sum(jnp.where(onehot, starts, 0), axis=-1) + rank
                n_tiles = TOP_K * n_ // moe_tm + N_EXPERTS
                tile_row = jnp.arange(n_tiles, dtype=jnp.int32) * moe_tm
                tile_e = jnp.minimum(jnp.sum(tile_row[:, None] >= ends[None, :], axis=1), N_EXPERTS - 1)
                tile_v = (tile_row < ends[-1]).astype(jnp.int32)
                xs_tok = _dispatch_call(pos, h_tok, n_tiles * moe_tm, min(n_, 512))
                y_tok = _ffn_call(xs_tok, tile_e.astype(jnp.int32), tile_v, wg, wu, wd, moe_tm, ffn_tf, True, True)
                mod_full = jnp.broadcast_to(mod_, (b_, SUBLANES, d)) if shared else mod_
                out = _combine_call(pos, y_tok, xn.reshape(n_, d), route, mod_full, l_, min(l_, 256))
                return out.reshape(b_, l_, d)

            x_new = moe_ffn(x, ya, zb, attn, yd, mod_l, False, tl["tm"])
            if not last:
                attn_c = _attn_call(qc, [kc], [vc], tcx["tq"])
                ydc = _fourier_call(zdc, *dft_c, wf, tcx["tr"])
                xc = moe_ffn(xc, yac, zbc, attn_c, ydc, mod_c, True, tcx["tm"])
            x = x_new
    return x
```

```python
import functools

import jax
import jax.numpy as jnp
from jax import lax
from jax.experimental import pallas as pl
from jax.experimental.pallas import tpu as pltpu

F32 = jnp.float32
BF16 = jnp.bfloat16

D_MODEL = 1024
GROUP_W = 256
SUB_W = 64
N_GROUPS = 4
CHUNK = 128
POOL_WINDOWS = (2, 4, 8, 16)
POOL_HALO = 16
MLA_HEADS = 4
QK_NOPE = 64
QK_ROPE = 32
V_HEAD = 64
QK_HEAD = QK_NOPE + QK_ROPE
Q_LORA = 192
KV_LORA = 128
GRID_W = 64
ROPE_BASE = 10000.0
D_FF = 3584
N_EXPERTS = 8
TOP_K = 2
N_ADA = 6
EPS = 1e-6
LOG2_E = 1.4426950408889634

OFF_A, OFF_B, OFF_CQ, OFF_CKV, OFF_CKR, OFF_D, IN_W = 0, 512, 768, 960, 1088, 1120, 1376
P_A, P_B, P_D, P_CKV, P_CQ, P_W = 0, 512, 768, 1024, 1152, 1408

LANES = 128
SUBLANES = 8
TOK_ROWS = D_MODEL // LANES
VMEM_LIMIT = 56 * 1024 * 1024


def _cparams(*sem):
    return pltpu.CompilerParams(dimension_semantics=sem, vmem_limit_bytes=VMEM_LIMIT)


def _rms(x, width=None):
    if width is None:
        return lax.rsqrt(jnp.mean(x * x, axis=-1, keepdims=True) + EPS)
    return lax.rsqrt(jnp.sum(x * x, axis=-1, keepdims=True) / width + EPS)


def _ada_kernel(c_ref, w_ref, b_ref, o_ref):
    ca = jax.nn.silu(c_ref[...])
    o_ref[...] = jnp.dot(ca, w_ref[...], preferred_element_type=F32, precision=lax.Precision.HIGHEST) + b_ref[...]


def _ada_call(c_all, ada_w, ada_b):
    depth, d, n = ada_w.shape
    rows = c_all.shape[0]
    tn = 1024
    return pl.pallas_call(
        _ada_kernel,
        grid=(depth, n // tn),
        in_specs=[
            pl.BlockSpec((rows, d), lambda i, j: (0, 0)),
            pl.BlockSpec((None, d, tn), lambda i, j: (i, 0, j)),
            pl.BlockSpec((None, 1, tn), lambda i, j: (i, 0, j)),
        ],
        out_specs=pl.BlockSpec((None, rows, tn), lambda i, j: (i, 0, j)),
        out_shape=jax.ShapeDtypeStruct((depth, rows, n), F32),
        compiler_params=_cparams("arbitrary", "arbitrary"),
        name="ada_mod",
    )(c_all, ada_w, ada_b.reshape(depth, 1, n))


INPROJ_SUB = 512


def _inproj_kernel(x_ref, mod_ref, n1g_ref, win_ref, gg_ref, wcat_ref, bfull_ref, qng_ref, wq_ref, kvng_ref,
                   wkv_ref, caq_ref, cbq_ref, cak_ref, cbk_ref,
                   ya_ref, zb_ref, zd_ref, q_ref, k_ref, v_ref, *, sub):
    tm = x_ref.shape[0]
    gain1 = n1g_ref[...] * (1.0 + mod_ref[1:2, :])
    sh1 = mod_ref[0:1, :]

    def project(rows):
        xt = x_ref[rows, :]
        h = xt * _rms(xt) * gain1 + sh1
        return jnp.dot(h.astype(BF16), win_ref[...], preferred_element_type=F32)

    grp = lax.broadcasted_iota(jnp.int32, (CHUNK, GROUP_W), 1) // SUB_W
    lane2 = lax.broadcasted_iota(jnp.int32, (sub, 2 * LANES), 1)
    lane = lax.broadcasted_iota(jnp.int32, (sub, LANES), 1)

    def mixers(p, r0):
        rows = slice(r0, r0 + sub)
        uv = jax.nn.gelu(p[:, P_A:P_A + 2 * GROUP_W])
        u = uv[:, :GROUP_W]
        vv = uv[:, GROUP_W:]
        vn = vv * _rms(vv) * gg_ref[...]
        for c in range(sub // CHUNK):
            crow = slice(c * CHUNK, (c + 1) * CHUNK)
            vc = vn[crow]
            vstack = jnp.concatenate([jnp.where(grp == g, vc, 0.0).astype(BF16) for g in range(N_GROUPS)], axis=0)
            mixed = jnp.dot(wcat_ref[...], vstack, preferred_element_type=F32) + bfull_ref[...]
            ya_ref[r0 + c * CHUNK:r0 + (c + 1) * CHUNK, :] = (u[crow] * mixed).astype(BF16)

        zb_ref[rows, :] = p[:, P_B:P_B + GROUP_W].astype(BF16)
        zd_ref[rows, :] = p[:, P_D:P_D + GROUP_W].astype(BF16)

        cq = p[:, P_CQ:P_CQ + 2 * LANES]
        cq_ss = jnp.sum(jnp.where(lane2 < Q_LORA, cq * cq, 0.0), axis=-1, keepdims=True)
        xq = cq * lax.rsqrt(cq_ss / Q_LORA + EPS) * qng_ref[...]
        qraw = jnp.dot(xq.astype(BF16), wq_ref[...], preferred_element_type=F32)
        ckv = p[:, P_CKV:P_CKV + KV_LORA]
        xkv = ckv * _rms(ckv) * kvng_ref[...]
        kvraw = jnp.dot(xkv.astype(BF16), wkv_ref[...], preferred_element_type=F32)
        krope = p[:, P_CQ + LANES:P_CQ + 2 * LANES]
        caq, cbq, cak, cbk = caq_ref[rows, :], cbq_ref[rows, :], cak_ref[rows, :], cbk_ref[rows, :]
        for hd in range(MLA_HEADS):
            cols = slice(hd * LANES, (hd + 1) * LANES)
            qb = qraw[:, cols]
            rq = lax.rsqrt(jnp.sum(jnp.where(lane < QK_HEAD, qb * qb, 0.0), axis=-1, keepdims=True) / QK_HEAD + EPS)
            q_ref[rows, cols] = ((qb * caq + pltpu.roll(qb, LANES - QK_ROPE, 1) * cbq) * rq).astype(BF16)
            kb = jnp.where(lane < QK_NOPE, kvraw[:, cols], krope)
            rk = lax.rsqrt(jnp.sum(jnp.where(lane < QK_HEAD, kb * kb, 0.0), axis=-1, keepdims=True) / QK_HEAD + EPS)
            k_ref[rows, cols] = ((kb * cak + pltpu.roll(kb, LANES - QK_ROPE, 1) * cbk) * rk).astype(BF16)
        v_ref[rows, :] = kvraw[:, MLA_HEADS * LANES:].astype(BF16)

    starts = range(0, tm, sub)
    ps = [project(slice(r0, r0 + sub)) for r0 in starts]
    for p, r0 in zip(ps, starts):
        mixers(p, r0)


def _inproj_call(x, mod, shared_mod, wts, tabs, tm):
    bs, ls, d = x.shape
    full = lambda a: pl.BlockSpec(a.shape, lambda b, j: (0,) * a.ndim)
    mod_map = (lambda b, j: (0, 0, 0)) if shared_mod else (lambda b, j: (b, 0, 0))
    tab_spec = pl.BlockSpec((tm, LANES), lambda b, j: (j, 0))
    tok = lambda w: pl.BlockSpec((None, tm, w), lambda b, j: (b, j, 0))
    widths = (GROUP_W, GROUP_W, GROUP_W, MLA_HEADS * LANES, MLA_HEADS * LANES, MLA_HEADS * V_HEAD)
    return pl.pallas_call(
        functools.partial(_inproj_kernel, sub=min(tm, INPROJ_SUB)),
        grid=(bs, ls // tm),
        in_specs=[tok(d), pl.BlockSpec((None, SUBLANES, d), mod_map)] + [full(w) for w in wts] + [tab_spec] * 4,
        out_specs=[tok(w) for w in widths],
        out_shape=[jax.ShapeDtypeStruct((bs, ls, w), BF16) for w in widths],
        compiler_params=_cparams("parallel", "parallel"),
        name="inproj",
    )(x, mod, *wts, *tabs)


def _attn_kernel(*refs, n_kv):
    q_ref = refs[0]
    k_refs = refs[1:1 + n_kv]
    v_refs = refs[1 + n_kv:1 + 2 * n_kv]
    o_ref = refs[-1]
    tq = q_ref.shape[0]
    head_of_lane = lax.broadcasted_iota(jnp.int32, (tq, MLA_HEADS * V_HEAD), 1) // V_HEAD
    out = jnp.zeros((tq, MLA_HEADS * V_HEAD), F32)
    for hd in range(MLA_HEADS):
        cols = slice(hd * LANES, (hd + 1) * LANES)
        qh = q_ref[:, cols]
        ss = [lax.dot_general(qh, kr[:, cols], (((1,), (1,)), ((), ())), preferred_element_type=F32)
              for kr in k_refs]
        m = functools.reduce(jnp.maximum, [jnp.max(s, axis=-1, keepdims=True) for s in ss])
        ps = [jnp.exp2(s - m) for s in ss]
        den = functools.reduce(jnp.add, [jnp.sum(pp, axis=-1, keepdims=True) for pp in ps])
        o = functools.reduce(jnp.add, [jnp.dot(pp.astype(BF16), vr[...], preferred_element_type=F32)
                                       for pp, vr in zip(ps, v_refs)])
        out = jnp.where(head_of_lane == hd, o / den, out)
    o_ref[...] = out.astype(BF16)


def _attn_call(q, ks, vs, tq):
    bs, lq, _ = q.shape
    n_kv = len(ks)
    kv_spec = lambda a: pl.BlockSpec((None,) + a.shape[1:], lambda b, j: (b, 0, 0))
    return pl.pallas_call(
        functools.partial(_attn_kernel, n_kv=n_kv),
        grid=(bs, lq // tq),
        in_specs=[pl.BlockSpec((None, tq, q.shape[2]), lambda b, j: (b, j, 0))]
        + [kv_spec(a) for a in ks] + [kv_spec(a) for a in vs],
        out_specs=pl.BlockSpec((None, tq, MLA_HEADS * V_HEAD), lambda b, j: (b, j, 0)),
        out_shape=jax.ShapeDtypeStruct((bs, lq, MLA_HEADS * V_HEAD), BF16),
        compiler_params=_cparams("parallel", "arbitrary"),
        name="attention",
    )(q, *ks, *vs)


def _fourier_kernel(cs_ref, z_ref, cc_ref, scn_ref, wf_ref, o_ref):
    tr = o_ref.shape[0]
    t = jnp.dot(cs_ref[...], z_ref[...], preferred_element_type=F32)
    y = (jnp.dot(t[:tr].astype(BF16), cc_ref[...], preferred_element_type=F32)
         + jnp.dot(t[tr:].astype(BF16), scn_ref[...], preferred_element_type=F32))
    o_ref[...] = jnp.dot(y.astype(BF16), wf_ref[...], preferred_element_type=F32).astype(BF16)


def _fourier_call(zd, cs, cc, scn, wf, tr):
    bs, ls, w = zd.shape
    full = lambda a: pl.BlockSpec(a.shape, lambda r, b: (0,) * a.ndim)
    return pl.pallas_call(
        _fourier_kernel,
        grid=(ls // tr, bs),
        in_specs=[pl.BlockSpec((None, 2 * tr, ls), lambda r, b: (r, 0, 0)),
                  pl.BlockSpec((None, ls, w), lambda r, b: (b, 0, 0)), full(cc), full(scn), full(wf)],
        out_specs=pl.BlockSpec((None, tr, w), lambda r, b: (b, r, 0)),
        out_shape=jax.ShapeDtypeStruct((bs, ls, w), BF16),
        compiler_params=_cparams("parallel", "arbitrary"),
        name="fourier",
    )(cs, zd, cc, scn, wf)


def _merge_kernel(*refs, seq_len, tm, tok_layout, router):
    (x_ref, ya_ref, zb_ref, zbp_ref, zbn_ref, at_ref, yd_ref, mod_ref, pw_ref, ps_ref, wo_ref, n2g_ref) = refs[:12]
    rest = refs[12:]
    if router:
        rw_ref, rest = rest[0], rest[1:]
    xn_ref, h_ref = rest[0], rest[1]
    j = pl.program_id(1)
    nj = pl.num_programs(1)

    zm = zb_ref[...].astype(F32)
    zp = jnp.where(j > 0, zbp_ref[...].astype(F32), 0.0)
    zn = jnp.where(j < nj - 1, zbn_ref[...].astype(F32), 0.0)
    ext = jnp.concatenate([zp, zm, zn], axis=0)
    n = tm + 2 * POOL_HALO

    def ahead(a, k):
        return pltpu.roll(a, n - k, 0)

    d2 = ext + ahead(ext, 1)
    d4 = d2 + ahead(d2, 2)
    d8 = d4 + ahead(d4, 4)
    d16 = d8 + ahead(d8, 8)
    sums = [ahead(d, POOL_HALO - w // 2)[:tm] for d, w in zip((d2, d4, d8, d16), POOL_WINDOWS)]
    tg = j * tm + lax.broadcasted_iota(jnp.int32, (tm, 1), 0)
    grp = lax.broadcasted_iota(jnp.int32, (tm, GROUP_W), 1) // SUB_W
    pooled = jnp.zeros((tm, GROUP_W), F32)
    for g, (s, w) in enumerate(zip(sums, POOL_WINDOWS)):
        cnt = jnp.minimum(tg - w // 2 + w, seq_len) - jnp.maximum(tg - w // 2, 0)
        pooled = jnp.where(grp == g, s / cnt.astype(F32), pooled)
    diff = pooled - zm
    yb = jnp.dot(diff.astype(BF16), pw_ref[...], preferred_element_type=F32) * ps_ref[...]

    acc = jnp.dot(ya_ref[...], wo_ref[0], preferred_element_type=F32)
    acc += jnp.dot(yb.astype(BF16), wo_ref[1], preferred_element_type=F32)
    acc += jnp.dot(at_ref[...], wo_ref[2], preferred_element_type=F32)
    acc += jnp.dot(yd_ref[...], wo_ref[3], preferred_element_type=F32)
    g1 = mod_ref[2:3, :]
    sh2 = mod_ref[3:4, :]
    sc2 = mod_ref[4:5, :]
    xn = x_ref[...] + g1 * acc
    xn_ref[...] = xn
    h2 = xn * _rms(xn) * n2g_ref[...] * (1.0 + sc2) + sh2
    if tok_layout:
        for jj in range(TOK_ROWS):
            h_ref[pl.ds(jj, tm, stride=TOK_ROWS), :] = h2[:, jj * LANES:(jj + 1) * LANES]
    else:
        h_ref[...] = h2.astype(BF16)
    if router:
        lg_ref = rest[2]
        h_hi = h2.astype(BF16)
        h_lo = (h2 - h_hi.astype(F32)).astype(BF16)
        lg_ref[...] = (jnp.dot(h_hi, rw_ref[0], preferred_element_type=F32)
                       + (jnp.dot(h_lo, rw_ref[0], preferred_element_type=F32)
                          + jnp.dot(h_hi, rw_ref[1], preferred_element_type=F32)))


def _merge_call(x, ya, zb, attn, yd, mod, shared_mod, wts, tm, tok_layout, router_w=None):
    bs, ls, d = x.shape
    nj = ls // tm
    hb = tm // POOL_HALO
    full = lambda a: pl.BlockSpec(a.shape, lambda b, j: (0,) * a.ndim)
    mod_map = (lambda b, j: (0, 0, 0)) if shared_mod else (lambda b, j: (b, 0, 0))
    tok = lambda w: pl.BlockSpec((None, tm, w), lambda b, j: (b, j, 0))
    in_specs = [
        tok(d), tok(GROUP_W), tok(GROUP_W),
        pl.BlockSpec((None, POOL_HALO, GROUP_W), lambda b, j: (b, jnp.maximum(j * hb - 1, 0), 0)),
        pl.BlockSpec((None, POOL_HALO, GROUP_W), lambda b, j: (b, jnp.minimum((j + 1) * hb, nj * hb - 1), 0)),
        tok(GROUP_W), tok(GROUP_W),
        pl.BlockSpec((None, SUBLANES, d), mod_map),
    ] + [full(w) for w in wts]
    args = [x, ya, zb, zb, zb, attn, yd, mod, *wts]
    out_specs = [tok(d)]
    out_shape = [jax.ShapeDtypeStruct((bs, ls, d), F32)]
    if tok_layout:
        out_specs.append(pl.BlockSpec((tm * TOK_ROWS, LANES), lambda b, j: (b * nj + j, 0)))
        out_shape.append(jax.ShapeDtypeStruct((bs * ls * TOK_ROWS, LANES), F32))
    else:
        out_specs.append(tok(d))
        out_shape.append(jax.ShapeDtypeStruct((bs, ls, d), BF16))
    if router_w is not None:
        in_specs.append(full(router_w))
        args.append(router_w)
        out_specs.append(tok(LANES))
        out_shape.append(jax.ShapeDtypeStruct((bs, ls, LANES), F32))
    return pl.pallas_call(
        functools.partial(_merge_kernel, seq_len=ls, tm=tm, tok_layout=tok_layout, router=router_w is not None),
        grid=(bs, nj),
        in_specs=in_specs,
        out_specs=out_specs,
        out_shape=out_shape,
        compiler_params=_cparams("parallel", "parallel"),
        name="merge",
    )(*args)


ACC_COLS = 256
FFN_SUB = 512


def _ffn_kernel(te_ref, tv_ref, *refs, tm, tok_in, tok_out, residual):
    x_ref, wg_ref, wu_ref, wd_ref = refs[:4]
    rest = refs[4:]
    if residual:
        res_ref, mod_ref, rest = rest[0], rest[1], rest[2:]
    o_ref, acc_ref = rest[0], rest[1]
    i = pl.program_id(0)
    f = pl.program_id(1)
    nf = pl.num_programs(1)
    valid = tv_ref[i] > 0

    def store(y):
        if tok_out:
            for jj in range(TOK_ROWS):
                o_ref[pl.ds(jj, tm, stride=TOK_ROWS), :] = y[:, jj * LANES:(jj + 1) * LANES]
        else:
            o_ref[...] = y

    @pl.when(valid)
    def _():
        @pl.when(f == 0)
        def _():
            acc_ref[...] = jnp.zeros_like(acc_ref)
            if tok_in:
                for jj in range(TOK_ROWS):
                    rest[2][:, jj * LANES:(jj + 1) * LANES] = x_ref[pl.ds(jj, tm, stride=TOK_ROWS), :].astype(BF16)

        xb = rest[2][...] if tok_in else x_ref[...]
        tf = wg_ref.shape[1]
        for lo in range(0, tf, FFN_SUB):
            sub = slice(lo, min(lo + FFN_SUB, tf))
            gate = jnp.dot(xb, wg_ref[:, sub], preferred_element_type=F32)
            up = jnp.dot(xb, wu_ref[:, sub], preferred_element_type=F32)
            act = (jax.nn.silu(gate) * up).astype(BF16)
            for cb in range(D_MODEL // ACC_COLS):
                cols = slice(cb * ACC_COLS, (cb + 1) * ACC_COLS)
                acc_ref[:, cols] += jnp.dot(act, wd_ref[sub, cols], preferred_element_type=F32)

        @pl.when(f == nf - 1)
        def _():
            y = acc_ref[...]
            if residual:
                y = res_ref[...] + mod_ref[5:6, :] * y
            store(y)

    @pl.when(jnp.logical_and(jnp.logical_not(valid), f == nf - 1))
    def _():
        store(jnp.zeros((tm, D_MODEL), F32))


def _ffn_call(x, tile_e, tile_v, wg, wu, wd, tm, tf, tok_in, tok_out, res=None, mod=None, tiles_per_mod=None):
    n_tiles = tile_e.shape[0]
    d, dff = wg.shape[1], wg.shape[2]
    nf = dff // tf
    last = nf - 1
    fsel = lambda i, f, te, tv: jnp.where(tv[i] > 0, f, last)
    xsel = lambda i, f, te, tv: (jnp.where(tv[i] > 0, i, 0), 0)
    x_spec = pl.BlockSpec((tm * TOK_ROWS, LANES), xsel) if tok_in else pl.BlockSpec((tm, d), xsel)
    in_specs = [
        x_spec,
        pl.BlockSpec((None, d, tf), lambda i, f, te, tv: (te[i], 0, fsel(i, f, te, tv))),
        pl.BlockSpec((None, d, tf), lambda i, f, te, tv: (te[i], 0, fsel(i, f, te, tv))),
        pl.BlockSpec((None, tf, d), lambda i, f, te, tv: (te[i], fsel(i, f, te, tv), 0)),
    ]
    args = [x, wg, wu, wd]
    residual = res is not None
    if residual:
        in_specs.append(pl.BlockSpec((tm, d), lambda i, f, te, tv: (i, 0)))
        in_specs.append(pl.BlockSpec((None, SUBLANES, d), lambda i, f, te, tv: (i // tiles_per_mod, 0, 0)))
        args += [res, mod]
    if tok_out:
        out_spec = pl.BlockSpec((tm * TOK_ROWS, LANES), lambda i, f, te, tv: (i, 0))
        out_shape = jax.ShapeDtypeStruct((n_tiles * tm * TOK_ROWS, LANES), F32)
    else:
        out_spec = pl.BlockSpec((tm, d), lambda i, f, te, tv: (i, 0))
        out_shape = jax.ShapeDtypeStruct((n_tiles * tm, d), F32)
    scratch = [pltpu.VMEM((tm, d), F32)]
    if tok_in:
        scratch.append(pltpu.VMEM((tm, d), BF16))
    return pl.pallas_call(
        functools.partial(_ffn_kernel, tm=tm, tok_in=tok_in, tok_out=tok_out, residual=residual),
        grid_spec=pltpu.PrefetchScalarGridSpec(
            num_scalar_prefetch=2, grid=(n_tiles, nf), in_specs=in_specs, out_specs=out_spec,
            scratch_shapes=scratch),
        out_shape=out_shape,
        compiler_params=_cparams("parallel", "arbitrary"),
        name="ffn",
    )(tile_e, tile_v, *args)


def _route_kernel(lg_ref, o_ref, cnt_ref, carry_ref):
    tm = lg_ref.shape[0]
    i = pl.program_id(0)

    @pl.when(i == 0)
    def _():
        carry_ref[...] = jnp.zeros_like(carry_ref)

    lane_i = lax.broadcasted_iota(jnp.int32, (tm, LANES), 1)
    lane = lane_i.astype(F32)
    neg = jnp.float32(-jnp.inf)
    lg = jnp.where(lane_i < N_EXPERTS, lg_ref[...], neg)
    m1 = jnp.max(lg, axis=-1, keepdims=True)
    i1 = jnp.min(jnp.where(lg == m1, lane, float(LANES)), axis=-1, keepdims=True)
    lg2 = jnp.where(lane == i1, neg, lg)
    m2 = jnp.max(lg2, axis=-1, keepdims=True)
    i2 = jnp.min(jnp.where(lg2 == m2, lane, float(LANES)), axis=-1, keepdims=True)
    e2 = jnp.exp(m2 - m1)
    w1 = 1.0 / (1.0 + e2)
    w2 = e2 / (1.0 + e2)
    hit = jnp.logical_or(lane == i1, lane == i2)
    onehot = jnp.where(hit, 1.0, 0.0).astype(BF16)
    r = lax.broadcasted_iota(jnp.int32, (tm, tm), 0)
    c = lax.broadcasted_iota(jnp.int32, (tm, tm), 1)
    before = jnp.where(c < r, 1.0, 0.0).astype(BF16)
    carry = carry_ref[0:1, :]
    cum = jnp.dot(before, onehot, preferred_element_type=F32) + carry
    rank1 = jnp.sum(jnp.where(lane == i1, cum, 0.0), axis=-1, keepdims=True)
    rank2 = jnp.sum(jnp.where(lane == i2, cum, 0.0), axis=-1, keepdims=True)
    total = carry + jnp.sum(onehot.astype(F32), axis=0, keepdims=True)
    carry_ref[...] = jnp.broadcast_to(total, carry_ref.shape)
    cnt_ref[...] = jnp.broadcast_to(total, cnt_ref.shape)
    out = jnp.zeros((tm, LANES), F32)
    for col, val in enumerate((i1, i2, w1, w2, rank1, rank2)):
        out = jnp.where(lane_i == col, val, out)
    o_ref[...] = out


def _route_call(logits, tm):
    n = logits.shape[0]
    return pl.pallas_call(
        _route_kernel,
        grid=(n // tm,),
        in_specs=[pl.BlockSpec((tm, LANES), lambda i: (i, 0))],
        out_specs=[pl.BlockSpec((tm, LANES), lambda i: (i, 0)), pl.BlockSpec((SUBLANES, LANES), lambda i: (0, 0))],
        out_shape=[jax.ShapeDtypeStruct((n, LANES), F32), jax.ShapeDtypeStruct((SUBLANES, LANES), F32)],
        scratch_shapes=[pltpu.VMEM((SUBLANES, LANES), F32)],
        compiler_params=_cparams("arbitrary"),
        name="route",
    )(logits)


ISSUE_UNROLL = 8


def _token_rows(tok):
    start = tok * TOK_ROWS
    return pl.ds(start if isinstance(start, int) else pl.multiple_of(start, TOK_ROWS), TOK_ROWS)


def _token_copy(src, src_tok, dst, dst_tok, sem):
    return pltpu.make_async_copy(src.at[_token_rows(src_tok), :], dst.at[_token_rows(dst_tok), :], sem)


def _for_each_token(n_tok, body):
    def group(g, carry):
        for u in range(ISSUE_UNROLL):
            body(g * ISSUE_UNROLL + u)
        return carry

    lax.fori_loop(0, n_tok // ISSUE_UNROLL, group, 0)


def _dispatch_kernel(pad_ref, pos_ref, h_ref, xs_hbm, zbuf, sem, zsem, *, td, pad_bits, max_tail):
    i = pl.program_id(0)

    def pad_copies():
        for e in range(N_EXPERTS):
            first, length = pad_ref[e], pad_ref[N_EXPERTS + e]
            for b in range(pad_bits):
                size = 1 << b
                tok = first + jnp.bitwise_and(length, size - 1)
                copy = pltpu.make_async_copy(
                    zbuf.at[pl.ds(0, size * TOK_ROWS), :],
                    xs_hbm.at[pl.ds(pl.multiple_of(tok * TOK_ROWS, TOK_ROWS), size * TOK_ROWS), :], zsem)
                yield jnp.bitwise_and(length, size) != 0, copy
        first, pieces = pad_ref[2 * N_EXPERTS], pad_ref[2 * N_EXPERTS + 1]
        piece = zbuf.shape[0]
        for k in range(max_tail):
            copy = pltpu.make_async_copy(
                zbuf, xs_hbm.at[pl.ds(pl.multiple_of(first * TOK_ROWS + k * piece, TOK_ROWS), piece), :], zsem)
            yield k < pieces, copy

    @pl.when(i == 0)
    def _():
        zbuf[...] = jnp.zeros_like(zbuf)
        for needed, copy in pad_copies():
            pl.when(needed)(copy.start)

    def issue(t):
        for s in range(TOP_K):
            _token_copy(h_ref, t, xs_hbm, pos_ref[0, TOP_K * t + s], sem).start()

    _for_each_token(td, issue)
    for _ in range(TOP_K):
        pltpu.make_async_copy(h_ref, xs_hbm.at[pl.ds(0, td * TOK_ROWS), :], sem).wait()

    @pl.when(i == 0)
    def _():
        for needed, copy in pad_copies():
            pl.when(needed)(copy.wait)


def _dispatch_call(pos, pad_info, h_tok, n_slots, td, tile):
    n = pos.shape[0]
    pad_bits = tile.bit_length() - 1
    return pl.pallas_call(
        functools.partial(_dispatch_kernel, td=td, pad_bits=pad_bits, max_tail=2 * N_EXPERTS),
        grid_spec=pltpu.PrefetchScalarGridSpec(
            num_scalar_prefetch=1, grid=(n // td,),
            in_specs=[pl.BlockSpec((None, 1, TOP_K * td), lambda i, pad: (i, 0, 0), memory_space=pltpu.SMEM),
                      pl.BlockSpec((td * TOK_ROWS, LANES), lambda i, pad: (i, 0))],
            out_specs=pl.BlockSpec(memory_space=pl.ANY),
            scratch_shapes=[pltpu.VMEM((tile // 2 * TOK_ROWS, LANES), F32), pltpu.SemaphoreType.DMA,
                            pltpu.SemaphoreType.DMA]),
        out_shape=jax.ShapeDtypeStruct((n_slots * TOK_ROWS, LANES), F32),
        compiler_params=_cparams("arbitrary"),
        name="dispatch",
    )(pad_info, pos.reshape(n // td, 1, TOP_K * td), h_tok)


def _combine_kernel(pos_ref, posn_ref, y_hbm, xn_ref, route_ref, mod_ref, o_ref, buf0, buf1, sems, *, tc):
    i = pl.program_id(0)
    n = pl.num_programs(0)
    bufs = (buf0, buf1)

    def gather(p_ref, slot):
        def issue(t):
            for s in range(TOP_K):
                _token_copy(y_hbm, p_ref[0, TOP_K * t + s], bufs[slot], TOP_K * t + s, sems.at[slot]).start()

        _for_each_token(tc, issue)

    def finish(slot):
        buf = bufs[slot]
        pltpu.make_async_copy(y_hbm.at[pl.ds(0, TOP_K * tc * TOK_ROWS), :], buf, sems.at[slot]).wait()
        g2 = mod_ref[5:6, :]
        w1 = route_ref[:, 2:3]
        w2 = route_ref[:, 3:4]
        for jj in range(TOK_ROWS):
            cols = slice(jj * LANES, (jj + 1) * LANES)
            ya = buf[pl.ds(jj, tc, stride=TOP_K * TOK_ROWS), :]
            yb = buf[pl.ds(TOK_ROWS + jj, tc, stride=TOP_K * TOK_ROWS), :]
            o_ref[:, cols] = xn_ref[:, cols] + g2[:, cols] * (w1 * ya + w2 * yb)

    @pl.when(i == 0)
    def _():
        gather(pos_ref, 0)

    for slot in range(2):
        @pl.when(i % 2 == slot)
        def _(slot=slot):
            @pl.when(i + 1 < n)
            def _():
                gather(posn_ref, 1 - slot)

            finish(slot)


def _combine_call(pos, y_tok, xn, route, mod, seq_len, tc):
    n, d = xn.shape
    steps = n // tc
    pos3 = pos.reshape(steps, 1, TOP_K * tc)
    buf = pltpu.VMEM((TOP_K * tc * TOK_ROWS, LANES), F32)
    return pl.pallas_call(
        functools.partial(_combine_kernel, tc=tc),
        grid=(steps,),
        in_specs=[pl.BlockSpec((None, 1, TOP_K * tc), lambda i: (i, 0, 0), memory_space=pltpu.SMEM),
                  pl.BlockSpec((None, 1, TOP_K * tc), lambda i: (jnp.minimum(i + 1, steps - 1), 0, 0),
                               memory_space=pltpu.SMEM),
                  pl.BlockSpec(memory_space=pl.ANY),
                  pl.BlockSpec((tc, d), lambda i: (i, 0)),
                  pl.BlockSpec((tc, LANES), lambda i: (i, 0)),
                  pl.BlockSpec((None, SUBLANES, d), lambda i: (i * tc // seq_len, 0, 0))],
        out_specs=pl.BlockSpec((tc, d), lambda i: (i, 0)),
        out_shape=jax.ShapeDtypeStruct((n, d), F32),
        scratch_shapes=[buf, buf, pltpu.SemaphoreType.DMA((2,))],
        compiler_params=_cparams("arbitrary"),
        name="combine",
    )(pos3, pos3, y_tok, xn, route, mod)


def _rot(w):
    return jnp.concatenate([-w[..., 8:16], w[..., 0:8], -w[..., 24:32], w[..., 16:24]], axis=-1)


def _swap(g):
    return jnp.concatenate([g[..., 8:16], g[..., 0:8], g[..., 24:32], g[..., 16:24]], axis=-1)


def _rope_angles(length):
    rows = length // GRID_W
    row = jnp.repeat(jnp.arange(rows, dtype=F32), GRID_W)
    col = jnp.tile(jnp.arange(GRID_W, dtype=F32), rows)
    n_freq = QK_ROPE // 4
    inv_freq = ROPE_BASE ** (-jnp.arange(n_freq, dtype=F32) / n_freq)
    ar = row[:, None] * inv_freq
    ac = col[:, None] * inv_freq
    cos = jnp.concatenate([jnp.cos(ar), jnp.cos(ar), jnp.cos(ac), jnp.cos(ac)], axis=1)
    sin = jnp.concatenate([jnp.sin(ar), jnp.sin(ar), jnp.sin(ac), jnp.sin(ac)], axis=1)
    return cos, sin


def _rope_tables(gain, length, cos, sin, scale):
    g_nope = jnp.broadcast_to(gain[:QK_NOPE], (length, QK_NOPE))
    g_rope = gain[QK_NOPE:]
    pad = jnp.zeros((length, LANES - QK_HEAD), F32)
    ca = jnp.concatenate([g_nope, g_rope * cos, pad], axis=1) * scale
    cb = jnp.concatenate([jnp.zeros((length, QK_NOPE), F32), _swap(g_rope) * sin, pad], axis=1) * scale
    return ca, cb


def _dft_tables(length, tr):
    k = jnp.arange(length, dtype=jnp.int32)[:, None]
    n1 = jnp.arange(length // GRID_W, dtype=jnp.int32)[None, :]
    n2 = jnp.arange(GRID_W, dtype=jnp.int32)[None, :]
    coarse = ((k * n1 * GRID_W) % length).astype(F32) * (2.0 * jnp.pi / length)
    fine = ((k * n2) % length).astype(F32) * (2.0 * jnp.pi / length)
    norm = length ** -0.5
    ca, sa = jnp.cos(coarse)[:, :, None] * norm, jnp.sin(coarse)[:, :, None] * norm
    cb, sb = jnp.cos(fine)[:, None, :], jnp.sin(fine)[:, None, :]
    c = (ca * cb - sa * sb).astype(BF16).reshape(length // tr, tr, length)
    s = (sa * cb + ca * sb).astype(BF16).reshape(length // tr, tr, length)
    cs = jnp.concatenate([c, s], axis=1)
    kc = jnp.arange(SUB_W, dtype=jnp.int32)
    angc = ((kc[:, None] * kc[None, :]) % SUB_W).astype(F32) * (2.0 * jnp.pi / SUB_W)
    eye = jnp.eye(N_GROUPS, dtype=F32)
    cc = jnp.kron(eye, jnp.cos(angc) * SUB_W ** -0.5).astype(BF16)
    scn = jnp.kron(eye, -jnp.sin(angc) * SUB_W ** -0.5).astype(BF16)
    return cs, cc, scn


def _layer_weights(i, w_in, gmlp_norm_g, spatial_w, spatial_b, pool_w, pool_scale, q_norm_g, w_uq, kv_norm_g,
                   w_ukv, fourier_w, w_out, norm1_g, norm2_g):
    wi = w_in[i]
    ckr = wi[:, OFF_CKR:OFF_D]
    win = jnp.concatenate([wi[:, OFF_A:OFF_B], wi[:, OFF_B:OFF_CQ], wi[:, OFF_D:IN_W], wi[:, OFF_CKV:OFF_CKR],
                           wi[:, OFF_CQ:OFF_CKV], ckr, _rot(ckr)], axis=1).astype(BF16)
    wcat = spatial_w[i].transpose(1, 0, 2).reshape(CHUNK, N_GROUPS * CHUNK).astype(BF16)
    bfull = jnp.repeat(spatial_b[i].T, SUB_W, axis=1)
    qng = jnp.concatenate([q_norm_g[i], jnp.zeros((2 * LANES - Q_LORA,), F32)])[None]
    wq4 = w_uq[i].reshape(Q_LORA, MLA_HEADS, QK_HEAD)
    wq = jnp.concatenate([wq4, _rot(wq4[..., QK_NOPE:])], axis=-1).reshape(Q_LORA, MLA_HEADS * LANES)
    wq = jnp.concatenate([wq, jnp.zeros((2 * LANES - Q_LORA, MLA_HEADS * LANES), F32)], axis=0).astype(BF16)
    wkv4 = w_ukv[i].reshape(KV_LORA, MLA_HEADS, QK_NOPE + V_HEAD)
    wk = jnp.concatenate([wkv4[..., :QK_NOPE], jnp.zeros((KV_LORA, MLA_HEADS, LANES - QK_NOPE), F32)], axis=-1)
    wkv = jnp.concatenate([wk.reshape(KV_LORA, MLA_HEADS * LANES),
                           wkv4[..., QK_NOPE:].reshape(KV_LORA, MLA_HEADS * V_HEAD)], axis=1).astype(BF16)
    in_w = (norm1_g[i][None], win, gmlp_norm_g[i][None], wcat, bfull, qng, wq, kv_norm_g[i][None], wkv)
    pw = jnp.zeros((GROUP_W, GROUP_W), F32)
    for g in range(N_GROUPS):
        pw = pw.at[g * SUB_W:(g + 1) * SUB_W, g * SUB_W:(g + 1) * SUB_W].set(pool_w[i, g])
    merge_w = (pw.astype(BF16), pool_scale[i][None], w_out[i].reshape(N_GROUPS, GROUP_W, D_MODEL).astype(BF16),
               norm2_g[i][None])
    return in_w, merge_w, fourier_w[i].astype(BF16)


def _pad_mod(m):
    m = m.reshape(m.shape[:-1] + (N_ADA, D_MODEL))
    return jnp.concatenate([m, jnp.zeros(m.shape[:-2] + (SUBLANES - N_ADA, D_MODEL), F32)], axis=-2)


def _tiles(seq_len):
    tm = min(seq_len, 512)
    return dict(tm=tm, tin=min(seq_len, 512), tq=min(seq_len, 512), tr=min(seq_len, 512))


def kernel(x, c, ctx, c_ctx, ada_w, ada_b, norm1_g, w_in, gmlp_norm_g, spatial_w, spatial_b, pool_w, pool_scale,
           q_norm_g, w_uq, kv_norm_g, w_ukv, qk_q_g, qk_k_g, fourier_w, w_out, norm2_g, ffn_w_gate, ffn_w_up,
           ffn_w_down, router_w, moe_w_gate, moe_w_up, moe_w_down):
    bsz, seq, d = x.shape
    lc = ctx.shape[1]
    depth = ada_w.shape[0]
    n_tok = bsz * seq
    tl, tcx = _tiles(seq), _tiles(lc)
    ffn_tm, ffn_tmc, ffn_tf = min(seq, 1024), min(bsz * lc, 1024), 1792
    moe_tm = 1024

    c_all = jnp.concatenate([c, c_ctx[None], jnp.zeros((SUBLANES - 1, d), F32)], axis=0)
    mods = _ada_call(c_all, ada_w, ada_b)
    cos, sin = _rope_angles(seq)
    ones, zeros = jnp.ones((lc, QK_ROPE), F32), jnp.zeros((lc, QK_ROPE), F32)
    dft_l = _dft_tables(seq, tl["tr"])
    dft_c = _dft_tables(lc, tcx["tr"])

    xc = ctx
    for i in range(depth):
        last = i == depth - 1
        moe = i % 2 == 1
        mod_l = _pad_mod(mods[i, :bsz])
        mod_c = _pad_mod(mods[i, bsz:bsz + 1])
        in_w, merge_w, wf = _layer_weights(i, w_in, gmlp_norm_g, spatial_w, spatial_b, pool_w, pool_scale,
                                           q_norm_g, w_uq, kv_norm_g, w_ukv, fourier_w, w_out, norm1_g, norm2_g)
        scale = QK_HEAD ** -0.5 * LOG2_E
        tabs_l = _rope_tables(qk_q_g[i], seq, cos, sin, scale) + _rope_tables(qk_k_g[i], seq, cos, sin, 1.0)
        tabs_c = _rope_tables(qk_q_g[i], lc, ones, zeros, scale) + _rope_tables(qk_k_g[i], lc, ones, zeros, 1.0)

        ya, zb, zd, q, k, v = _inproj_call(x, mod_l, False, in_w, tabs_l, tl["tin"])
        yac, zbc, zdc, qc, kc, vc = _inproj_call(xc, mod_c, True, in_w, tabs_c, tcx["tin"])
        attn = _attn_call(q, [kc, k], [vc, v], tl["tq"])
        yd = _fourier_call(zd, *dft_l, wf, tl["tr"])

        if not moe:
            j = i // 2
            wg, wu, wd = (ffn_w_gate[j:j + 1].astype(BF16), ffn_w_up[j:j + 1].astype(BF16),
                          ffn_w_down[j:j + 1].astype(BF16))
            xn, h2 = _merge_call(x, ya, zb, attn, yd, mod_l, False, merge_w, tl["tm"], False)
            n_tiles = n_tok // ffn_tm
            x = _ffn_call(h2.reshape(n_tok, d), jnp.zeros((n_tiles,), jnp.int32), jnp.ones((n_tiles,), jnp.int32),
                          wg, wu, wd, ffn_tm, ffn_tf, False, False, res=xn.reshape(n_tok, d), mod=mod_l,
                          tiles_per_mod=seq // ffn_tm).reshape(bsz, seq, d)
            if not last:
                attn_c = _attn_call(qc, [kc], [vc], tcx["tq"])
                ydc = _fourier_call(zdc, *dft_c, wf, tcx["tr"])
                xnc, h2c = _merge_call(xc, yac, zbc, attn_c, ydc, mod_c, True, merge_w, tcx["tm"], False)
                nc_tok = bsz * lc
                nct = nc_tok // ffn_tmc
                xc = _ffn_call(h2c.reshape(nc_tok, d), jnp.zeros((nct,), jnp.int32), jnp.ones((nct,), jnp.int32),
                               wg, wu, wd, ffn_tmc, ffn_tf, False, False, res=xnc.reshape(nc_tok, d), mod=mod_c,
                               tiles_per_mod=nct).reshape(bsz, lc, d)
        else:
            j = i // 2
            wg, wu, wd = moe_w_gate[j].astype(BF16), moe_w_up[j].astype(BF16), moe_w_down[j].astype(BF16)
            rw32 = jnp.concatenate([router_w[j], jnp.zeros((d, LANES - N_EXPERTS), F32)], axis=1)
            rw_hi = rw32.astype(BF16)
            rw = jnp.stack([rw_hi, (rw32 - rw_hi.astype(F32)).astype(BF16)])

            def moe_ffn(xs, ya_, zb_, attn_, yd_, mod_, shared, tms):
                b_, l_, _ = xs.shape
                n_ = b_ * l_
                xn, h_tok, logits = _merge_call(xs, ya_, zb_, attn_, yd_, mod_, shared, merge_w, tms, True, rw)
                route, counts = _route_call(logits.reshape(n_, LANES), min(n_, 512))
                eidx = route[:, 0:TOP_K].astype(jnp.int32)
                rank = route[:, 4:4 + TOP_K].astype(jnp.int32)
                cnt = counts[0, :N_EXPERTS].astype(jnp.int32)
                padded = (cnt + moe_tm - 1) // moe_tm * moe_tm
                ends = jnp.cumsum(padded)
                starts = ends - padded
                onehot = eidx[..., None] == jnp.arange(N_EXPERTS, dtype=jnp.int32)
                pos = jnp.sum(jnp.where(onehot, starts, 0), axis=-1) + rank
                n_tiles = TOP_K * n_ // moe_tm + N_EXPERTS
                tile_row = jnp.arange(n_tiles, dtype=jnp.int32) * moe_tm
                tile_e = jnp.minimum(jnp.sum(tile_row[:, None] >= ends[None, :], axis=1), N_EXPERTS - 1)
                tile_v = (tile_row < ends[-1]).astype(jnp.int32)
                tail = jnp.stack([ends[-1], (n_tiles * moe_tm - ends[-1]) // (moe_tm // 2)])
                pad_info = jnp.concatenate([starts + cnt, padded - cnt, tail]).astype(jnp.int32)
                xs_tok = _dispatch_call(pos, pad_info, h_tok, n_tiles * moe_tm, min(n_, 512), moe_tm)
                y_tok = _ffn_call(xs_tok, tile_e.astype(jnp.int32), tile_v, wg, wu, wd, moe_tm, ffn_tf, True, True)
                mod_full = jnp.broadcast_to(mod_, (b_, SUBLANES, d)) if shared else mod_
                out = _combine_call(pos, y_tok, xn.reshape(n_, d), route, mod_full, l_, min(l_, 256))
                return out.reshape(b_, l_, d)

            x_new = moe_ffn(x, ya, zb, attn, yd, mod_l, False, tl["tm"])
            if not last:
                attn_c = _attn_call(qc, [kc], [vc], tcx["tq"])
                ydc = _fourier_call(zdc, *dft_c, wf, tcx["tr"])
                xc = moe_ffn(xc, yac, zbc, attn_c, ydc, mod_c, True, tcx["tm"])
            x = x_new
    return x
```

```python
import functools

import jax
import jax.numpy as jnp
from jax import lax
from jax.experimental import pallas as pl
from jax.experimental.pallas import tpu as pltpu

F32 = jnp.float32
BF16 = jnp.bfloat16

D_MODEL = 1024
GROUP_W = 256
SUB_W = 64
N_GROUPS = 4
CHUNK = 128
POOL_WINDOWS = (2, 4, 8, 16)
POOL_HALO = 16
MLA_HEADS = 4
QK_NOPE = 64
QK_ROPE = 32
V_HEAD = 64
QK_HEAD = QK_NOPE + QK_ROPE
Q_LORA = 192
KV_LORA = 128
GRID_W = 64
ROPE_BASE = 10000.0
D_FF = 3584
N_EXPERTS = 8
TOP_K = 2
N_ADA = 6
EPS = 1e-6
LOG2_E = 1.4426950408889634

OFF_A, OFF_B, OFF_CQ, OFF_CKV, OFF_CKR, OFF_D, IN_W = 0, 512, 768, 960, 1088, 1120, 1376
P_A, P_B, P_D, P_CKV, P_CQ, P_W = 0, 512, 768, 1024, 1152, 1408

LANES = 128
SUBLANES = 8
TOK_ROWS = D_MODEL // LANES
VMEM_LIMIT = 56 * 1024 * 1024


def _cparams(*sem):
    return pltpu.CompilerParams(dimension_semantics=sem, vmem_limit_bytes=VMEM_LIMIT)


def _rms(x):
    return lax.rsqrt(jnp.mean(x * x, axis=-1, keepdims=True) + EPS)


def _ada_kernel(c_ref, w_ref, b_ref, o_ref):
    ca = jax.nn.silu(c_ref[...])
    o_ref[...] = jnp.dot(ca, w_ref[...], preferred_element_type=F32, precision=lax.Precision.HIGHEST) + b_ref[...]


def _ada_call(c_all, ada_w, ada_b):
    depth, d, n = ada_w.shape
    rows = c_all.shape[0]
    tn = 1024
    return pl.pallas_call(
        _ada_kernel,
        grid=(depth, n // tn),
        in_specs=[
            pl.BlockSpec((rows, d), lambda i, j: (0, 0)),
            pl.BlockSpec((None, d, tn), lambda i, j: (i, 0, j)),
            pl.BlockSpec((None, 1, tn), lambda i, j: (i, 0, j)),
        ],
        out_specs=pl.BlockSpec((None, rows, tn), lambda i, j: (i, 0, j)),
        out_shape=jax.ShapeDtypeStruct((depth, rows, n), F32),
        compiler_params=_cparams("arbitrary", "arbitrary"),
        name="ada_mod",
    )(c_all, ada_w, ada_b.reshape(depth, 1, n))


INPROJ_SUB = 512


def _inproj_kernel(x_ref, mod_ref, n1g_ref, win_ref, gg_ref, wcat_ref, bfull_ref, qng_ref, wq_ref, kvng_ref,
                   wkv_ref, caq_ref, cbq_ref, cak_ref, cbk_ref,
                   ya_ref, zb_ref, zd_ref, q_ref, k_ref, v_ref, *, sub):
    tm = x_ref.shape[0]
    gain1 = n1g_ref[...] * (1.0 + mod_ref[1:2, :])
    sh1 = mod_ref[0:1, :]

    def project(rows):
        xt = x_ref[rows, :]
        h = xt * _rms(xt) * gain1 + sh1
        return jnp.dot(h.astype(BF16), win_ref[...], preferred_element_type=F32)

    grp = lax.broadcasted_iota(jnp.int32, (CHUNK, GROUP_W), 1) // SUB_W
    lane2 = lax.broadcasted_iota(jnp.int32, (sub, 2 * LANES), 1)
    lane = lax.broadcasted_iota(jnp.int32, (sub, LANES), 1)
    head_rows = lax.broadcasted_iota(jnp.int32, (LANES, LANES), 0) < QK_HEAD
    head_ones = jnp.where(head_rows, 1.0, 0.0).astype(BF16)

    def head_rms(blk):
        ssq = jnp.dot((blk * blk).astype(BF16), head_ones, preferred_element_type=F32)
        return lax.rsqrt(ssq / QK_HEAD + EPS)

    def mixers(p, r0):
        rows = slice(r0, r0 + sub)
        uv = jax.nn.gelu(p[:, P_A:P_A + 2 * GROUP_W])
        u = uv[:, :GROUP_W]
        vv = uv[:, GROUP_W:]
        vn = vv * _rms(vv) * gg_ref[...]
        for c in range(sub // CHUNK):
            crow = slice(c * CHUNK, (c + 1) * CHUNK)
            vc = vn[crow]
            vstack = jnp.concatenate([jnp.where(grp == g, vc, 0.0).astype(BF16) for g in range(N_GROUPS)], axis=0)
            mixed = jnp.dot(wcat_ref[...], vstack, preferred_element_type=F32) + bfull_ref[...]
            ya_ref[r0 + c * CHUNK:r0 + (c + 1) * CHUNK, :] = (u[crow] * mixed).astype(BF16)

        zb_ref[rows, :] = p[:, P_B:P_B + GROUP_W].astype(BF16)
        zd_ref[rows, :] = p[:, P_D:P_D + GROUP_W].astype(BF16)

        cq = p[:, P_CQ:P_CQ + 2 * LANES]
        cq_ss = jnp.sum(jnp.where(lane2 < Q_LORA, cq * cq, 0.0), axis=-1, keepdims=True)
        xq = cq * lax.rsqrt(cq_ss / Q_LORA + EPS) * qng_ref[...]
        qraw = jnp.dot(xq.astype(BF16), wq_ref[...], preferred_element_type=F32)
        ckv = p[:, P_CKV:P_CKV + KV_LORA]
        xkv = ckv * _rms(ckv) * kvng_ref[...]
        kvraw = jnp.dot(xkv.astype(BF16), wkv_ref[...], preferred_element_type=F32)
        krope = p[:, P_CQ + LANES:P_CQ + 2 * LANES]
        caq, cbq, cak, cbk = caq_ref[rows, :], cbq_ref[rows, :], cak_ref[rows, :], cbk_ref[rows, :]
        for hd in range(MLA_HEADS):
            cols = slice(hd * LANES, (hd + 1) * LANES)
            qb = qraw[:, cols]
            rq = head_rms(qb)
            q_ref[rows, cols] = ((qb * caq + pltpu.roll(qb, LANES - QK_ROPE, 1) * cbq) * rq).astype(BF16)
            kb = jnp.where(lane < QK_NOPE, kvraw[:, cols], krope)
            rk = head_rms(kb)
            k_ref[rows, cols] = ((kb * cak + pltpu.roll(kb, LANES - QK_ROPE, 1) * cbk) * rk).astype(BF16)
        v_ref[rows, :] = kvraw[:, MLA_HEADS * LANES:].astype(BF16)

    starts = range(0, tm, sub)
    ps = [project(slice(r0, r0 + sub)) for r0 in starts]
    for p, r0 in zip(ps, starts):
        mixers(p, r0)


def _inproj_call(x, mod, shared_mod, wts, tabs, tm):
    bs, ls, d = x.shape
    full = lambda a: pl.BlockSpec(a.shape, lambda b, j: (0,) * a.ndim)
    mod_map = (lambda b, j: (0, 0, 0)) if shared_mod else (lambda b, j: (b, 0, 0))
    tab_spec = pl.BlockSpec((tm, LANES), lambda b, j: (j, 0))
    tok = lambda w: pl.BlockSpec((None, tm, w), lambda b, j: (b, j, 0))
    widths = (GROUP_W, GROUP_W, GROUP_W, MLA_HEADS * LANES, MLA_HEADS * LANES, MLA_HEADS * V_HEAD)
    return pl.pallas_call(
        functools.partial(_inproj_kernel, sub=min(tm, INPROJ_SUB)),
        grid=(bs, ls // tm),
        in_specs=[tok(d), pl.BlockSpec((None, SUBLANES, d), mod_map)] + [full(w) for w in wts] + [tab_spec] * 4,
        out_specs=[tok(w) for w in widths],
        out_shape=[jax.ShapeDtypeStruct((bs, ls, w), BF16) for w in widths],
        compiler_params=_cparams("parallel", "parallel"),
        name="inproj",
    )(x, mod, *wts, *tabs)


SCORES_AHEAD = 1


def _attn_kernel(*refs, n_kv):
    q_ref = refs[0]
    k_refs = refs[1:1 + n_kv]
    v_refs = refs[1 + n_kv:1 + 2 * n_kv]
    o_ref = refs[-1]
    tq = q_ref.shape[0]
    head_of_lane = lax.broadcasted_iota(jnp.int32, (tq, MLA_HEADS * V_HEAD), 1) // V_HEAD
    out = jnp.zeros((tq, MLA_HEADS * V_HEAD), F32)

    def scores(hd):
        cols = slice(hd * LANES, (hd + 1) * LANES)
        qh = q_ref[:, cols]
        return [lax.dot_general(qh, kr[:, cols], (((1,), (1,)), ((), ())), preferred_element_type=F32)
                for kr in k_refs]

    ahead = [scores(hd) for hd in range(min(SCORES_AHEAD, MLA_HEADS))]
    for hd in range(MLA_HEADS):
        ss = ahead.pop(0)
        if hd + SCORES_AHEAD < MLA_HEADS:
            ahead.append(scores(hd + SCORES_AHEAD))
        m = functools.reduce(jnp.maximum, [jnp.max(s, axis=-1, keepdims=True) for s in ss])
        ps = [jnp.exp2(s - m) for s in ss]
        den = functools.reduce(jnp.add, [jnp.sum(pp, axis=-1, keepdims=True) for pp in ps])
        o = functools.reduce(jnp.add, [jnp.dot(pp.astype(BF16), vr[...], preferred_element_type=F32)
                                       for pp, vr in zip(ps, v_refs)])
        out = jnp.where(head_of_lane == hd, o / den, out)
    o_ref[...] = out.astype(BF16)


def _attn_call(q, ks, vs, tq):
    bs, lq, _ = q.shape
    n_kv = len(ks)
    kv_spec = lambda a: pl.BlockSpec((None,) + a.shape[1:], lambda b, j: (b, 0, 0))
    return pl.pallas_call(
        functools.partial(_attn_kernel, n_kv=n_kv),
        grid=(bs, lq // tq),
        in_specs=[pl.BlockSpec((None, tq, q.shape[2]), lambda b, j: (b, j, 0))]
        + [kv_spec(a) for a in ks] + [kv_spec(a) for a in vs],
        out_specs=pl.BlockSpec((None, tq, MLA_HEADS * V_HEAD), lambda b, j: (b, j, 0)),
        out_shape=jax.ShapeDtypeStruct((bs, lq, MLA_HEADS * V_HEAD), BF16),
        compiler_params=_cparams("parallel", "arbitrary"),
        name="attention",
    )(q, *ks, *vs)


def _fourier_kernel(cs_ref, z_ref, cc_ref, scn_ref, wf_ref, o_ref):
    tr = o_ref.shape[0]
    t = jnp.dot(cs_ref[...], z_ref[...], preferred_element_type=F32)
    y = (jnp.dot(t[:tr].astype(BF16), cc_ref[...], preferred_element_type=F32)
         + jnp.dot(t[tr:].astype(BF16), scn_ref[...], preferred_element_type=F32))
    o_ref[...] = jnp.dot(y.astype(BF16), wf_ref[...], preferred_element_type=F32).astype(BF16)


def _fourier_call(zd, cs, cc, scn, wf, tr):
    bs, ls, w = zd.shape
    full = lambda a: pl.BlockSpec(a.shape, lambda r, b: (0,) * a.ndim)
    return pl.pallas_call(
        _fourier_kernel,
        grid=(ls // tr, bs),
        in_specs=[pl.BlockSpec((None, 2 * tr, ls), lambda r, b: (r, 0, 0)),
                  pl.BlockSpec((None, ls, w), lambda r, b: (b, 0, 0)), full(cc), full(scn), full(wf)],
        out_specs=pl.BlockSpec((None, tr, w), lambda r, b: (b, r, 0)),
        out_shape=jax.ShapeDtypeStruct((bs, ls, w), BF16),
        compiler_params=_cparams("parallel", "arbitrary"),
        name="fourier",
    )(cs, zd, cc, scn, wf)


MERGE_SUB = 512


def _merge_kernel(*refs, seq_len, tm, sub, tok_layout, router):
    (x_ref, ya_ref, zb_ref, zbp_ref, zbn_ref, at_ref, yd_ref, mod_ref, pw_ref, ps_ref, wo_ref, n2g_ref) = refs[:12]
    rest = refs[12:]
    if router:
        rw_ref, rest = rest[0], rest[1:]
    xn_ref, h_ref = rest[0], rest[1]
    j = pl.program_id(1)
    nj = pl.num_programs(1)

    zm = zb_ref[...].astype(F32)
    zp = jnp.where(j > 0, zbp_ref[...].astype(F32), 0.0)
    zn = jnp.where(j < nj - 1, zbn_ref[...].astype(F32), 0.0)
    ext = jnp.concatenate([zp, zm, zn], axis=0)
    n = tm + 2 * POOL_HALO

    def ahead(a, k):
        return pltpu.roll(a, n - k, 0)

    tg = j * tm + lax.broadcasted_iota(jnp.int32, (tm, 1), 0)
    low_group = lax.broadcasted_iota(jnp.int32, (tm, LANES), 1) < SUB_W

    def window_mean(d, w):
        cnt = jnp.minimum(tg - w // 2 + w, seq_len) - jnp.maximum(tg - w // 2, 0)
        return ahead(d, POOL_HALO - w // 2)[:tm] * (1.0 / cnt.astype(F32))

    halves = []
    for half in range(GROUP_W // LANES):
        w_lo, w_hi = POOL_WINDOWS[2 * half], POOL_WINDOWS[2 * half + 1]
        d, width, means = ext[:, half * LANES:(half + 1) * LANES], 1, {}
        while width < w_hi:
            d = d + ahead(d, width)
            width *= 2
            if width in (w_lo, w_hi):
                means[width] = window_mean(d, width)
        halves.append(jnp.where(low_group, means[w_lo], means[w_hi]))
    diff = jnp.concatenate(halves, axis=1) - zm
    yb = jnp.dot(diff.astype(BF16), pw_ref[...], preferred_element_type=F32) * ps_ref[...]

    yb = yb.astype(BF16)
    g1 = mod_ref[2:3, :]
    sh2 = mod_ref[3:4, :]
    gain2 = n2g_ref[...] * (1.0 + mod_ref[4:5, :])

    def project(rows):
        acc = jnp.dot(ya_ref[rows, :], wo_ref[0], preferred_element_type=F32)
        acc += jnp.dot(yb[rows], wo_ref[1], preferred_element_type=F32)
        acc += jnp.dot(at_ref[rows, :], wo_ref[2], preferred_element_type=F32)
        acc += jnp.dot(yd_ref[rows, :], wo_ref[3], preferred_element_type=F32)
        return x_ref[rows, :] + g1 * acc

    def modulate(xn, r0):
        rows = slice(r0, r0 + sub)
        xn_ref[rows, :] = xn
        h2 = xn * _rms(xn) * gain2 + sh2
        if tok_layout:
            for jj in range(TOK_ROWS):
                h_ref[pl.ds(r0 * TOK_ROWS + jj, sub, stride=TOK_ROWS), :] = h2[:, jj * LANES:(jj + 1) * LANES]
        else:
            h_ref[rows, :] = h2.astype(BF16)
        if router:
            lg_ref = rest[2]
            h_hi = h2.astype(BF16)
            h_lo = (h2 - h_hi.astype(F32)).astype(BF16)
            lg_ref[rows, :] = (jnp.dot(h_hi, rw_ref[0], preferred_element_type=F32)
                               + (jnp.dot(h_lo, rw_ref[0], preferred_element_type=F32)
                                  + jnp.dot(h_hi, rw_ref[1], preferred_element_type=F32)))

    starts = range(0, tm, sub)
    xns = [project(slice(r0, r0 + sub)) for r0 in starts]
    for xn, r0 in zip(xns, starts):
        modulate(xn, r0)


def _merge_call(x, ya, zb, attn, yd, mod, shared_mod, wts, tm, tok_layout, router_w=None):
    bs, ls, d = x.shape
    nj = ls // tm
    hb = tm // POOL_HALO
    full = lambda a: pl.BlockSpec(a.shape, lambda b, j: (0,) * a.ndim)
    mod_map = (lambda b, j: (0, 0, 0)) if shared_mod else (lambda b, j: (b, 0, 0))
    tok = lambda w: pl.BlockSpec((None, tm, w), lambda b, j: (b, j, 0))
    in_specs = [
        tok(d), tok(GROUP_W), tok(GROUP_W),
        pl.BlockSpec((None, POOL_HALO, GROUP_W), lambda b, j: (b, jnp.maximum(j * hb - 1, 0), 0)),
        pl.BlockSpec((None, POOL_HALO, GROUP_W), lambda b, j: (b, jnp.minimum((j + 1) * hb, nj * hb - 1), 0)),
        tok(GROUP_W), tok(GROUP_W),
        pl.BlockSpec((None, SUBLANES, d), mod_map),
    ] + [full(w) for w in wts]
    args = [x, ya, zb, zb, zb, attn, yd, mod, *wts]
    out_specs = [tok(d)]
    out_shape = [jax.ShapeDtypeStruct((bs, ls, d), F32)]
    if tok_layout:
        out_specs.append(pl.BlockSpec((tm * TOK_ROWS, LANES), lambda b, j: (b * nj + j, 0)))
        out_shape.append(jax.ShapeDtypeStruct((bs * ls * TOK_ROWS, LANES), F32))
    else:
        out_specs.append(tok(d))
        out_shape.append(jax.ShapeDtypeStruct((bs, ls, d), BF16))
    if router_w is not None:
        in_specs.append(full(router_w))
        args.append(router_w)
        out_specs.append(tok(LANES))
        out_shape.append(jax.ShapeDtypeStruct((bs, ls, LANES), F32))
    return pl.pallas_call(
        functools.partial(_merge_kernel, seq_len=ls, tm=tm, sub=min(tm, MERGE_SUB), tok_layout=tok_layout,
                          router=router_w is not None),
        grid=(bs, nj),
        in_specs=in_specs,
        out_specs=out_specs,
        out_shape=out_shape,
        compiler_params=_cparams("parallel", "parallel"),
        name="merge",
    )(*args)


ACC_COLS = 256
FFN_SUB = 512


def _ffn_kernel(te_ref, tv_ref, *refs, tm, tok_in, tok_out, residual):
    x_ref, wg_ref, wu_ref, wd_ref = refs[:4]
    rest = refs[4:]
    if residual:
        res_ref, mod_ref, rest = rest[0], rest[1], rest[2:]
    o_ref, acc_ref = rest[0], rest[1]
    i = pl.program_id(0)
    f = pl.program_id(1)
    nf = pl.num_programs(1)
    valid = tv_ref[i] > 0

    def store(y):
        if tok_out:
            for jj in range(TOK_ROWS):
                o_ref[pl.ds(jj, tm, stride=TOK_ROWS), :] = y[:, jj * LANES:(jj + 1) * LANES]
        else:
            o_ref[...] = y

    @pl.when(valid)
    def _():
        @pl.when(f == 0)
        def _():
            acc_ref[...] = jnp.zeros_like(acc_ref)
            if tok_in:
                for jj in range(TOK_ROWS):
                    rest[2][:, jj * LANES:(jj + 1) * LANES] = x_ref[pl.ds(jj, tm, stride=TOK_ROWS), :].astype(BF16)

        xb = rest[2][...] if tok_in else x_ref[...]
        tf = wg_ref.shape[1]
        acts = []
        for lo in range(0, tf, FFN_SUB):
            sub = slice(lo, min(lo + FFN_SUB, tf))
            gate = jnp.dot(xb, wg_ref[:, sub], preferred_element_type=F32)
            up = jnp.dot(xb, wu_ref[:, sub], preferred_element_type=F32)
            acts.append((jax.nn.silu(gate) * up).astype(BF16))
        act = jnp.concatenate(acts, axis=1)
        for cb in range(D_MODEL // ACC_COLS):
            cols = slice(cb * ACC_COLS, (cb + 1) * ACC_COLS)
            acc_ref[:, cols] += jnp.dot(act, wd_ref[:, cols], preferred_element_type=F32)

        @pl.when(f == nf - 1)
        def _():
            y = acc_ref[...]
            if residual:
                y = res_ref[...] + mod_ref[5:6, :] * y
            store(y)

    @pl.when(jnp.logical_and(jnp.logical_not(valid), f == nf - 1))
    def _():
        store(jnp.zeros((tm, D_MODEL), F32))


def _ffn_call(x, tile_e, tile_v, wg, wu, wd, tm, tf, tok_in, tok_out, res=None, mod=None, tiles_per_mod=None):
    n_tiles = tile_e.shape[0]
    d, dff = wg.shape[1], wg.shape[2]
    nf = dff // tf
    last = nf - 1
    fsel = lambda i, f, te, tv: jnp.where(tv[i] > 0, f, last)
    xsel = lambda i, f, te, tv: (jnp.where(tv[i] > 0, i, 0), 0)
    x_spec = pl.BlockSpec((tm * TOK_ROWS, LANES), xsel) if tok_in else pl.BlockSpec((tm, d), xsel)
    in_specs = [
        x_spec,
        pl.BlockSpec((None, d, tf), lambda i, f, te, tv: (te[i], 0, fsel(i, f, te, tv))),
        pl.BlockSpec((None, d, tf), lambda i, f, te, tv: (te[i], 0, fsel(i, f, te, tv))),
        pl.BlockSpec((None, tf, d), lambda i, f, te, tv: (te[i], fsel(i, f, te, tv), 0)),
    ]
    args = [x, wg, wu, wd]
    residual = res is not None
    if residual:
        in_specs.append(pl.BlockSpec((tm, d), lambda i, f, te, tv: (i, 0)))
        in_specs.append(pl.BlockSpec((None, SUBLANES, d), lambda i, f, te, tv: (i // tiles_per_mod, 0, 0)))
        args += [res, mod]
    if tok_out:
        out_spec = pl.BlockSpec((tm * TOK_ROWS, LANES), lambda i, f, te, tv: (i, 0))
        out_shape = jax.ShapeDtypeStruct((n_tiles * tm * TOK_ROWS, LANES), F32)
    else:
        out_spec = pl.BlockSpec((tm, d), lambda i, f, te, tv: (i, 0))
        out_shape = jax.ShapeDtypeStruct((n_tiles * tm, d), F32)
    scratch = [pltpu.VMEM((tm, d), F32)]
    if tok_in:
        scratch.append(pltpu.VMEM((tm, d), BF16))
    return pl.pallas_call(
        functools.partial(_ffn_kernel, tm=tm, tok_in=tok_in, tok_out=tok_out, residual=residual),
        grid_spec=pltpu.PrefetchScalarGridSpec(
            num_scalar_prefetch=2, grid=(n_tiles, nf), in_specs=in_specs, out_specs=out_spec,
            scratch_shapes=scratch),
        out_shape=out_shape,
        compiler_params=_cparams("parallel", "arbitrary"),
        name="ffn",
    )(tile_e, tile_v, *args)


def _route_kernel(lg_ref, o_ref, cnt_ref, carry_ref):
    tm = lg_ref.shape[0]
    i = pl.program_id(0)

    @pl.when(i == 0)
    def _():
        carry_ref[...] = jnp.zeros_like(carry_ref)

    lane_i = lax.broadcasted_iota(jnp.int32, (tm, LANES), 1)
    lane = lane_i.astype(F32)
    neg = jnp.float32(-jnp.inf)
    lg = jnp.where(lane_i < N_EXPERTS, lg_ref[...], neg)
    m1 = jnp.max(lg, axis=-1, keepdims=True)
    i1 = jnp.min(jnp.where(lg == m1, lane, float(LANES)), axis=-1, keepdims=True)
    lg2 = jnp.where(lane == i1, neg, lg)
    m2 = jnp.max(lg2, axis=-1, keepdims=True)
    i2 = jnp.min(jnp.where(lg2 == m2, lane, float(LANES)), axis=-1, keepdims=True)
    e2 = jnp.exp(m2 - m1)
    w1 = 1.0 / (1.0 + e2)
    w2 = e2 / (1.0 + e2)
    hit = jnp.logical_or(lane == i1, lane == i2)
    onehot = jnp.where(hit, 1.0, 0.0).astype(BF16)
    r = lax.broadcasted_iota(jnp.int32, (tm, tm), 0)
    c = lax.broadcasted_iota(jnp.int32, (tm, tm), 1)
    before = jnp.where(c < r, 1.0, 0.0).astype(BF16)
    carry = carry_ref[0:1, :]
    cum = jnp.dot(before, onehot, preferred_element_type=F32) + carry
    rank1 = jnp.sum(jnp.where(lane == i1, cum, 0.0), axis=-1, keepdims=True)
    rank2 = jnp.sum(jnp.where(lane == i2, cum, 0.0), axis=-1, keepdims=True)
    total = carry + jnp.sum(onehot.astype(F32), axis=0, keepdims=True)
    carry_ref[...] = jnp.broadcast_to(total, carry_ref.shape)
    cnt_ref[...] = jnp.broadcast_to(total, cnt_ref.shape)
    out = jnp.zeros((tm, LANES), F32)
    for col, val in enumerate((i1, i2, w1, w2, rank1, rank2)):
        out = jnp.where(lane_i == col, val, out)
    o_ref[...] = out


def _route_call(logits, tm):
    n = logits.shape[0]
    return pl.pallas_call(
        _route_kernel,
        grid=(n // tm,),
        in_specs=[pl.BlockSpec((tm, LANES), lambda i: (i, 0))],
        out_specs=[pl.BlockSpec((tm, LANES), lambda i: (i, 0)), pl.BlockSpec((SUBLANES, LANES), lambda i: (0, 0))],
        out_shape=[jax.ShapeDtypeStruct((n, LANES), F32), jax.ShapeDtypeStruct((SUBLANES, LANES), F32)],
        scratch_shapes=[pltpu.VMEM((SUBLANES, LANES), F32)],
        compiler_params=_cparams("arbitrary"),
        name="route",
    )(logits)


ISSUE_UNROLL = 8


def _token_rows(tok):
    start = tok * TOK_ROWS
    return pl.ds(start if isinstance(start, int) else pl.multiple_of(start, TOK_ROWS), TOK_ROWS)


def _token_copy(src, src_tok, dst, dst_tok, sem):
    return pltpu.make_async_copy(src.at[_token_rows(src_tok), :], dst.at[_token_rows(dst_tok), :], sem)


def _for_each_token(n_tok, body):
    def group(g, carry):
        for u in range(ISSUE_UNROLL):
            body(g * ISSUE_UNROLL + u)
        return carry

    lax.fori_loop(0, n_tok // ISSUE_UNROLL, group, 0)


def _dispatch_kernel(pad_ref, pos_ref, h_ref, xs_hbm, zbuf, sem, zsem, *, td, pad_bits, max_tail):
    i = pl.program_id(0)

    def pad_copies():
        for e in range(N_EXPERTS):
            first, length = pad_ref[e], pad_ref[N_EXPERTS + e]
            for b in range(pad_bits):
                size = 1 << b
                tok = first + jnp.bitwise_and(length, size - 1)
                copy = pltpu.make_async_copy(
                    zbuf.at[pl.ds(0, size * TOK_ROWS), :],
                    xs_hbm.at[pl.ds(pl.multiple_of(tok * TOK_ROWS, TOK_ROWS), size * TOK_ROWS), :], zsem)
                yield jnp.bitwise_and(length, size) != 0, copy
        first, pieces = pad_ref[2 * N_EXPERTS], pad_ref[2 * N_EXPERTS + 1]
        piece = zbuf.shape[0]
        for k in range(max_tail):
            copy = pltpu.make_async_copy(
                zbuf, xs_hbm.at[pl.ds(pl.multiple_of(first * TOK_ROWS + k * piece, TOK_ROWS), piece), :], zsem)
            yield k < pieces, copy

    @pl.when(i == 0)
    def _():
        zbuf[...] = jnp.zeros_like(zbuf)
        for needed, copy in pad_copies():
            pl.when(needed)(copy.start)

    def issue(t):
        for s in range(TOP_K):
            _token_copy(h_ref, t, xs_hbm, pos_ref[0, TOP_K * t + s], sem).start()

    _for_each_token(td, issue)
    for _ in range(TOP_K):
        pltpu.make_async_copy(h_ref, xs_hbm.at[pl.ds(0, td * TOK_ROWS), :], sem).wait()

    @pl.when(i == 0)
    def _():
        for needed, copy in pad_copies():
            pl.when(needed)(copy.wait)


def _dispatch_call(pos, pad_info, h_tok, n_slots, td, tile):
    n = pos.shape[0]
    pad_bits = tile.bit_length() - 1
    return pl.pallas_call(
        functools.partial(_dispatch_kernel, td=td, pad_bits=pad_bits, max_tail=2 * N_EXPERTS),
        grid_spec=pltpu.PrefetchScalarGridSpec(
            num_scalar_prefetch=1, grid=(n // td,),
            in_specs=[pl.BlockSpec((None, 1, TOP_K * td), lambda i, pad: (i, 0, 0), memory_space=pltpu.SMEM),
                      pl.BlockSpec((td * TOK_ROWS, LANES), lambda i, pad: (i, 0))],
            out_specs=pl.BlockSpec(memory_space=pl.ANY),
            scratch_shapes=[pltpu.VMEM((tile // 2 * TOK_ROWS, LANES), F32), pltpu.SemaphoreType.DMA,
                            pltpu.SemaphoreType.DMA]),
        out_shape=jax.ShapeDtypeStruct((n_slots * TOK_ROWS, LANES), F32),
        compiler_params=_cparams("arbitrary"),
        name="dispatch",
    )(pad_info, pos.reshape(n // td, 1, TOP_K * td), h_tok)


def _combine_kernel(pos_ref, posn_ref, y_hbm, xn_ref, route_ref, mod_ref, o_ref, buf0, buf1, sems, *, tc):
    i = pl.program_id(0)
    n = pl.num_programs(0)
    bufs = (buf0, buf1)

    def gather(p_ref, slot):
        def issue(t):
            for s in range(TOP_K):
                _token_copy(y_hbm, p_ref[0, TOP_K * t + s], bufs[slot], TOP_K * t + s, sems.at[slot]).start()

        _for_each_token(tc, issue)

    def finish(slot):
        buf = bufs[slot]
        pltpu.make_async_copy(y_hbm.at[pl.ds(0, TOP_K * tc * TOK_ROWS), :], buf, sems.at[slot]).wait()
        g2 = mod_ref[5:6, :]
        w1 = route_ref[:, 2:3]
        w2 = route_ref[:, 3:4]
        for jj in range(TOK_ROWS):
            cols = slice(jj * LANES, (jj + 1) * LANES)
            ya = buf[pl.ds(jj, tc, stride=TOP_K * TOK_ROWS), :]
            yb = buf[pl.ds(TOK_ROWS + jj, tc, stride=TOP_K * TOK_ROWS), :]
            o_ref[:, cols] = xn_ref[:, cols] + g2[:, cols] * (w1 * ya + w2 * yb)

    @pl.when(i == 0)
    def _():
        gather(pos_ref, 0)

    for slot in range(2):
        @pl.when(i % 2 == slot)
        def _(slot=slot):
            @pl.when(i + 1 < n)
            def _():
                gather(posn_ref, 1 - slot)

            finish(slot)


def _combine_call(pos, y_tok, xn, route, mod, seq_len, tc):
    n, d = xn.shape
    steps = n // tc
    pos3 = pos.reshape(steps, 1, TOP_K * tc)
    buf = pltpu.VMEM((TOP_K * tc * TOK_ROWS, LANES), F32)
    return pl.pallas_call(
        functools.partial(_combine_kernel, tc=tc),
        grid=(steps,),
        in_specs=[pl.BlockSpec((None, 1, TOP_K * tc), lambda i: (i, 0, 0), memory_space=pltpu.SMEM),
                  pl.BlockSpec((None, 1, TOP_K * tc), lambda i: (jnp.minimum(i + 1, steps - 1), 0, 0),
                               memory_space=pltpu.SMEM),
                  pl.BlockSpec(memory_space=pl.ANY),
                  pl.BlockSpec((tc, d), lambda i: (i, 0)),
                  pl.BlockSpec((tc, LANES), lambda i: (i, 0)),
                  pl.BlockSpec((None, SUBLANES, d), lambda i: (i * tc // seq_len, 0, 0))],
        out_specs=pl.BlockSpec((tc, d), lambda i: (i, 0)),
        out_shape=jax.ShapeDtypeStruct((n, d), F32),
        scratch_shapes=[buf, buf, pltpu.SemaphoreType.DMA((2,))],
        compiler_params=_cparams("arbitrary"),
        name="combine",
    )(pos3, pos3, y_tok, xn, route, mod)


def _rot(w):
    return jnp.concatenate([-w[..., 8:16], w[..., 0:8], -w[..., 24:32], w[..., 16:24]], axis=-1)


def _swap(g):
    return jnp.concatenate([g[..., 8:16], g[..., 0:8], g[..., 24:32], g[..., 16:24]], axis=-1)


def _rope_angles(length):
    rows = length // GRID_W
    row = jnp.repeat(jnp.arange(rows, dtype=F32), GRID_W)
    col = jnp.tile(jnp.arange(GRID_W, dtype=F32), rows)
    n_freq = QK_ROPE // 4
    inv_freq = ROPE_BASE ** (-jnp.arange(n_freq, dtype=F32) / n_freq)
    ar = row[:, None] * inv_freq
    ac = col[:, None] * inv_freq
    cos = jnp.concatenate([jnp.cos(ar), jnp.cos(ar), jnp.cos(ac), jnp.cos(ac)], axis=1)
    sin = jnp.concatenate([jnp.sin(ar), jnp.sin(ar), jnp.sin(ac), jnp.sin(ac)], axis=1)
    return cos, sin


def _rope_tables(gain, length, cos, sin, scale):
    g_nope = jnp.broadcast_to(gain[:QK_NOPE], (length, QK_NOPE))
    g_rope = gain[QK_NOPE:]
    pad = jnp.zeros((length, LANES - QK_HEAD), F32)
    ca = jnp.concatenate([g_nope, g_rope * cos, pad], axis=1) * scale
    cb = jnp.concatenate([jnp.zeros((length, QK_NOPE), F32), _swap(g_rope) * sin, pad], axis=1) * scale
    return ca, cb


def _dft_tables(length, tr):
    k = jnp.arange(length, dtype=jnp.int32)[:, None]
    n1 = jnp.arange(length // GRID_W, dtype=jnp.int32)[None, :]
    n2 = jnp.arange(GRID_W, dtype=jnp.int32)[None, :]
    coarse = ((k * n1 * GRID_W) % length).astype(F32) * (2.0 * jnp.pi / length)
    fine = ((k * n2) % length).astype(F32) * (2.0 * jnp.pi / length)
    norm = length ** -0.5
    ca, sa = jnp.cos(coarse)[:, :, None] * norm, jnp.sin(coarse)[:, :, None] * norm
    cb, sb = jnp.cos(fine)[:, None, :], jnp.sin(fine)[:, None, :]
    c = (ca * cb - sa * sb).astype(BF16).reshape(length // tr, tr, length)
    s = (sa * cb + ca * sb).astype(BF16).reshape(length // tr, tr, length)
    cs = jnp.concatenate([c, s], axis=1)
    kc = jnp.arange(SUB_W, dtype=jnp.int32)
    angc = ((kc[:, None] * kc[None, :]) % SUB_W).astype(F32) * (2.0 * jnp.pi / SUB_W)
    eye = jnp.eye(N_GROUPS, dtype=F32)
    cc = jnp.kron(eye, jnp.cos(angc) * SUB_W ** -0.5).astype(BF16)
    scn = jnp.kron(eye, -jnp.sin(angc) * SUB_W ** -0.5).astype(BF16)
    return cs, cc, scn


def _layer_weights(i, w_in, gmlp_norm_g, spatial_w, spatial_b, pool_w, pool_scale, q_norm_g, w_uq, kv_norm_g,
                   w_ukv, fourier_w, w_out, norm1_g, norm2_g):
    wi = w_in[i]
    ckr = wi[:, OFF_CKR:OFF_D]
    win = jnp.concatenate([wi[:, OFF_A:OFF_B], wi[:, OFF_B:OFF_CQ], wi[:, OFF_D:IN_W], wi[:, OFF_CKV:OFF_CKR],
                           wi[:, OFF_CQ:OFF_CKV], ckr, _rot(ckr)], axis=1).astype(BF16)
    wcat = spatial_w[i].transpose(1, 0, 2).reshape(CHUNK, N_GROUPS * CHUNK).astype(BF16)
    bfull = jnp.repeat(spatial_b[i].T, SUB_W, axis=1)
    qng = jnp.concatenate([q_norm_g[i], jnp.zeros((2 * LANES - Q_LORA,), F32)])[None]
    wq4 = w_uq[i].reshape(Q_LORA, MLA_HEADS, QK_HEAD)
    wq = jnp.concatenate([wq4, _rot(wq4[..., QK_NOPE:])], axis=-1).reshape(Q_LORA, MLA_HEADS * LANES)
    wq = jnp.concatenate([wq, jnp.zeros((2 * LANES - Q_LORA, MLA_HEADS * LANES), F32)], axis=0).astype(BF16)
    wkv4 = w_ukv[i].reshape(KV_LORA, MLA_HEADS, QK_NOPE + V_HEAD)
    wk = jnp.concatenate([wkv4[..., :QK_NOPE], jnp.zeros((KV_LORA, MLA_HEADS, LANES - QK_NOPE), F32)], axis=-1)
    wkv = jnp.concatenate([wk.reshape(KV_LORA, MLA_HEADS * LANES),
                           wkv4[..., QK_NOPE:].reshape(KV_LORA, MLA_HEADS * V_HEAD)], axis=1).astype(BF16)
    in_w = (norm1_g[i][None], win, gmlp_norm_g[i][None], wcat, bfull, qng, wq, kv_norm_g[i][None], wkv)
    pw = jnp.zeros((GROUP_W, GROUP_W), F32)
    for g in range(N_GROUPS):
        pw = pw.at[g * SUB_W:(g + 1) * SUB_W, g * SUB_W:(g + 1) * SUB_W].set(pool_w[i, g])
    merge_w = (pw.astype(BF16), pool_scale[i][None], w_out[i].reshape(N_GROUPS, GROUP_W, D_MODEL).astype(BF16),
               norm2_g[i][None])
    return in_w, merge_w, fourier_w[i].astype(BF16)


def _pad_mod(m):
    m = m.reshape(m.shape[:-1] + (N_ADA, D_MODEL))
    return jnp.concatenate([m, jnp.zeros(m.shape[:-2] + (SUBLANES - N_ADA, D_MODEL), F32)], axis=-2)


def _tiles(seq_len):
    return dict(tin=min(seq_len, 512), tmg=min(seq_len, 1024), tq=min(seq_len, 512), tr=min(seq_len, 512))


def kernel(x, c, ctx, c_ctx, ada_w, ada_b, norm1_g, w_in, gmlp_norm_g, spatial_w, spatial_b, pool_w, pool_scale,
           q_norm_g, w_uq, kv_norm_g, w_ukv, qk_q_g, qk_k_g, fourier_w, w_out, norm2_g, ffn_w_gate, ffn_w_up,
           ffn_w_down, router_w, moe_w_gate, moe_w_up, moe_w_down):
    bsz, seq, d = x.shape
    lc = ctx.shape[1]
    depth = ada_w.shape[0]
    n_tok = bsz * seq
    tl, tcx = _tiles(seq), _tiles(lc)
    ffn_tm, ffn_tmc, ffn_tf = min(seq, 1024), min(bsz * lc, 1024), 1792
    moe_tm = 1024

    c_all = jnp.concatenate([c, c_ctx[None], jnp.zeros((SUBLANES - 1, d), F32)], axis=0)
    mods = _ada_call(c_all, ada_w, ada_b)
    cos, sin = _rope_angles(seq)
    ones, zeros = jnp.ones((lc, QK_ROPE), F32), jnp.zeros((lc, QK_ROPE), F32)
    dft_l = _dft_tables(seq, tl["tr"])
    dft_c = _dft_tables(lc, tcx["tr"])

    xc = ctx
    for i in range(depth):
        last = i == depth - 1
        moe = i % 2 == 1
        mod_l = _pad_mod(mods[i, :bsz])
        mod_c = _pad_mod(mods[i, bsz:bsz + 1])
        in_w, merge_w, wf = _layer_weights(i, w_in, gmlp_norm_g, spatial_w, spatial_b, pool_w, pool_scale,
                                           q_norm_g, w_uq, kv_norm_g, w_ukv, fourier_w, w_out, norm1_g, norm2_g)
        scale = QK_HEAD ** -0.5 * LOG2_E
        tabs_l = _rope_tables(qk_q_g[i], seq, cos, sin, scale) + _rope_tables(qk_k_g[i], seq, cos, sin, 1.0)
        tabs_c = _rope_tables(qk_q_g[i], lc, ones, zeros, scale) + _rope_tables(qk_k_g[i], lc, ones, zeros, 1.0)

        ya, zb, zd, q, k, v = _inproj_call(x, mod_l, False, in_w, tabs_l, tl["tin"])
        yac, zbc, zdc, qc, kc, vc = _inproj_call(xc, mod_c, True, in_w, tabs_c, tcx["tin"])
        attn = _attn_call(q, [kc, k], [vc, v], tl["tq"])
        yd = _fourier_call(zd, *dft_l, wf, tl["tr"])

        if not moe:
            j = i // 2
            wg, wu, wd = (ffn_w_gate[j:j + 1].astype(BF16), ffn_w_up[j:j + 1].astype(BF16),
                          ffn_w_down[j:j + 1].astype(BF16))
            xn, h2 = _merge_call(x, ya, zb, attn, yd, mod_l, False, merge_w, tl["tmg"], False)
            n_tiles = n_tok // ffn_tm
            x = _ffn_call(h2.reshape(n_tok, d), jnp.zeros((n_tiles,), jnp.int32), jnp.ones((n_tiles,), jnp.int32),
                          wg, wu, wd, ffn_tm, ffn_tf, False, False, res=xn.reshape(n_tok, d), mod=mod_l,
                          tiles_per_mod=seq // ffn_tm).reshape(bsz, seq, d)
            if not last:
                attn_c = _attn_call(qc, [kc], [vc], tcx["tq"])
                ydc = _fourier_call(zdc, *dft_c, wf, tcx["tr"])
                xnc, h2c = _merge_call(xc, yac, zbc, attn_c, ydc, mod_c, True, merge_w, tcx["tmg"], False)
                nc_tok = bsz * lc
                nct = nc_tok // ffn_tmc
                xc = _ffn_call(h2c.reshape(nc_tok, d), jnp.zeros((nct,), jnp.int32), jnp.ones((nct,), jnp.int32),
                               wg, wu, wd, ffn_tmc, ffn_tf, False, False, res=xnc.reshape(nc_tok, d), mod=mod_c,
                               tiles_per_mod=nct).reshape(bsz, lc, d)
        else:
            j = i // 2
            wg, wu, wd = moe_w_gate[j].astype(BF16), moe_w_up[j].astype(BF16), moe_w_down[j].astype(BF16)
            rw32 = jnp.concatenate([router_w[j], jnp.zeros((d, LANES - N_EXPERTS), F32)], axis=1)
            rw_hi = rw32.astype(BF16)
            rw = jnp.stack([rw_hi, (rw32 - rw_hi.astype(F32)).astype(BF16)])

            def moe_ffn(xs, ya_, zb_, attn_, yd_, mod_, shared, tms):
                b_, l_, _ = xs.shape
                n_ = b_ * l_
                xn, h_tok, logits = _merge_call(xs, ya_, zb_, attn_, yd_, mod_, shared, merge_w, tms, True, rw)
                route, counts = _route_call(logits.reshape(n_, LANES), min(n_, 512))
                eidx = route[:, 0:TOP_K].astype(jnp.int32)
                rank = route[:, 4:4 + TOP_K].astype(jnp.int32)
                cnt = counts[0, :N_EXPERTS].astype(jnp.int32)
                padded = (cnt + moe_tm - 1) // moe_tm * moe_tm
                ends = jnp.cumsum(padded)
                starts = ends - padded
                onehot = eidx[..., None] == jnp.arange(N_EXPERTS, dtype=jnp.int32)
                pos = jnp.sum(jnp.where(onehot, starts, 0), axis=-1) + rank
                n_tiles = TOP_K * n_ // moe_tm + N_EXPERTS
                tile_row = jnp.arange(n_tiles, dtype=jnp.int32) * moe_tm
                tile_e = jnp.minimum(jnp.sum(tile_row[:, None] >= ends[None, :], axis=1), N_EXPERTS - 1)
                tile_v = (tile_row < ends[-1]).astype(jnp.int32)
                tail = jnp.stack([ends[-1], (n_tiles * moe_tm - ends[-1]) // (moe_tm // 2)])
                pad_info = jnp.concatenate([starts + cnt, padded - cnt, tail]).astype(jnp.int32)
                xs_tok = _dispatch_call(pos, pad_info, h_tok, n_tiles * moe_tm, min(n_, 512), moe_tm)
                y_tok = _ffn_call(xs_tok, tile_e.astype(jnp.int32), tile_v, wg, wu, wd, moe_tm, ffn_tf, True, True)
                mod_full = jnp.broadcast_to(mod_, (b_, SUBLANES, d)) if shared else mod_
                out = _combine_call(pos, y_tok, xn.reshape(n_, d), route, mod_full, l_, min(l_, 256))
                return out.reshape(b_, l_, d)

            x_new = moe_ffn(x, ya, zb, attn, yd, mod_l, False, tl["tmg"])
            if not last:
                attn_c = _attn_call(qc, [kc], [vc], tcx["tq"])
                ydc = _fourier_call(zdc, *dft_c, wf, tcx["tr"])
                xc = moe_ffn(xc, yac, zbc, attn_c, ydc, mod_c, True, tcx["tmg"])
            x = x_new
    return x
```

```python
import functools

import jax
import jax.numpy as jnp
from jax import lax
from jax.experimental import pallas as pl
from jax.experimental.pallas import tpu as pltpu

F32 = jnp.float32
BF16 = jnp.bfloat16

D_MODEL = 1024
GROUP_W = 256
SUB_W = 64
N_GROUPS = 4
CHUNK = 128
POOL_WINDOWS = (2, 4, 8, 16)
POOL_HALO = 16
MLA_HEADS = 4
QK_NOPE = 64
QK_ROPE = 32
V_HEAD = 64
QK_HEAD = QK_NOPE + QK_ROPE
Q_LORA = 192
KV_LORA = 128
GRID_W = 64
ROPE_BASE = 10000.0
D_FF = 3584
N_EXPERTS = 8
TOP_K = 2
N_ADA = 6
EPS = 1e-6
LOG2_E = 1.4426950408889634

OFF_A, OFF_B, OFF_CQ, OFF_CKV, OFF_CKR, OFF_D, IN_W = 0, 512, 768, 960, 1088, 1120, 1376
P_A, P_B, P_D, P_CKV, P_CQ, P_W = 0, 512, 768, 1024, 1152, 1408

LANES = 128
SUBLANES = 8
TOK_ROWS = D_MODEL // LANES
VMEM_LIMIT = 56 * 1024 * 1024


def _cparams(*sem):
    return pltpu.CompilerParams(dimension_semantics=sem, vmem_limit_bytes=VMEM_LIMIT)


def _rms(x):
    return lax.rsqrt(jnp.mean(x * x, axis=-1, keepdims=True) + EPS)


def _ada_kernel(c_ref, w_ref, b_ref, o_ref):
    ca = jax.nn.silu(c_ref[...])
    o_ref[...] = jnp.dot(ca, w_ref[...], preferred_element_type=F32, precision=lax.Precision.HIGHEST) + b_ref[...]


def _ada_call(c_all, ada_w, ada_b):
    depth, d, n = ada_w.shape
    rows = c_all.shape[0]
    tn = 1024
    return pl.pallas_call(
        _ada_kernel,
        grid=(depth, n // tn),
        in_specs=[
            pl.BlockSpec((rows, d), lambda i, j: (0, 0)),
            pl.BlockSpec((None, d, tn), lambda i, j: (i, 0, j)),
            pl.BlockSpec((None, 1, tn), lambda i, j: (i, 0, j)),
        ],
        out_specs=pl.BlockSpec((None, rows, tn), lambda i, j: (i, 0, j)),
        out_shape=jax.ShapeDtypeStruct((depth, rows, n), F32),
        compiler_params=_cparams("arbitrary", "arbitrary"),
        name="ada_mod",
    )(c_all, ada_w, ada_b.reshape(depth, 1, n))


INPROJ_SUB = 512


def _inproj_kernel(x_ref, mod_ref, n1g_ref, win_ref, gg_ref, wcat_ref, bfull_ref, qng_ref, wq_ref, kvng_ref,
                   wkv_ref, caq_ref, cbq_ref, cak_ref, cbk_ref,
                   ya_ref, zb_ref, zd_ref, q_ref, k_ref, v_ref, *, sub):
    tm = x_ref.shape[0]
    gain1 = n1g_ref[...] * (1.0 + mod_ref[1:2, :])
    sh1 = mod_ref[0:1, :]

    def project(rows):
        xt = x_ref[rows, :]
        h = xt * _rms(xt) * gain1 + sh1
        return jnp.dot(h.astype(BF16), win_ref[...], preferred_element_type=F32)

    grp = lax.broadcasted_iota(jnp.int32, (CHUNK, GROUP_W), 1) // SUB_W
    lane2 = lax.broadcasted_iota(jnp.int32, (sub, 2 * LANES), 1)
    lane = lax.broadcasted_iota(jnp.int32, (sub, LANES), 1)
    head_rows = lax.broadcasted_iota(jnp.int32, (LANES, LANES), 0) < QK_HEAD
    head_ones = jnp.where(head_rows, 1.0, 0.0).astype(BF16)

    def head_rms(blk):
        ssq = jnp.dot((blk * blk).astype(BF16), head_ones, preferred_element_type=F32)
        return lax.rsqrt(ssq / QK_HEAD + EPS)

    def mixers(p, r0):
        rows = slice(r0, r0 + sub)
        uv = jax.nn.gelu(p[:, P_A:P_A + 2 * GROUP_W])
        u = uv[:, :GROUP_W]
        vv = uv[:, GROUP_W:]
        vn = vv * _rms(vv) * gg_ref[...]
        for c in range(sub // CHUNK):
            crow = slice(c * CHUNK, (c + 1) * CHUNK)
            vc = vn[crow]
            vstack = jnp.concatenate([jnp.where(grp == g, vc, 0.0).astype(BF16) for g in range(N_GROUPS)], axis=0)
            mixed = jnp.dot(wcat_ref[...], vstack, preferred_element_type=F32) + bfull_ref[...]
            ya_ref[r0 + c * CHUNK:r0 + (c + 1) * CHUNK, :] = (u[crow] * mixed).astype(BF16)

        zb_ref[rows, :] = p[:, P_B:P_B + GROUP_W].astype(BF16)
        zd_ref[rows, :] = p[:, P_D:P_D + GROUP_W].astype(BF16)

        cq = p[:, P_CQ:P_CQ + 2 * LANES]
        cq_ss = jnp.sum(jnp.where(lane2 < Q_LORA, cq * cq, 0.0), axis=-1, keepdims=True)
        xq = cq * lax.rsqrt(cq_ss / Q_LORA + EPS) * qng_ref[...]
        qraw = jnp.dot(xq.astype(BF16), wq_ref[...], preferred_element_type=F32)
        ckv = p[:, P_CKV:P_CKV + KV_LORA]
        xkv = ckv * _rms(ckv) * kvng_ref[...]
        kvraw = jnp.dot(xkv.astype(BF16), wkv_ref[...], preferred_element_type=F32)
        krope = p[:, P_CQ + LANES:P_CQ + 2 * LANES]
        caq, cbq, cak, cbk = caq_ref[rows, :], cbq_ref[rows, :], cak_ref[rows, :], cbk_ref[rows, :]
        for hd in range(MLA_HEADS):
            cols = slice(hd * LANES, (hd + 1) * LANES)
            qb = qraw[:, cols]
            rq = head_rms(qb)
            q_ref[rows, cols] = ((qb * caq + pltpu.roll(qb, LANES - QK_ROPE, 1) * cbq) * rq).astype(BF16)
            kb = jnp.where(lane < QK_NOPE, kvraw[:, cols], krope)
            rk = head_rms(kb)
            k_ref[rows, cols] = ((kb * cak + pltpu.roll(kb, LANES - QK_ROPE, 1) * cbk) * rk).astype(BF16)
        v_ref[rows, :] = kvraw[:, MLA_HEADS * LANES:].astype(BF16)

    starts = range(0, tm, sub)
    ps = [project(slice(r0, r0 + sub)) for r0 in starts]
    for p, r0 in zip(ps, starts):
        mixers(p, r0)


def _inproj_call(x, mod, shared_mod, wts, tabs, tm):
    bs, ls, d = x.shape
    full = lambda a: pl.BlockSpec(a.shape, lambda b, j: (0,) * a.ndim)
    mod_map = (lambda b, j: (0, 0, 0)) if shared_mod else (lambda b, j: (b, 0, 0))
    tab_spec = pl.BlockSpec((tm, LANES), lambda b, j: (j, 0))
    tok = lambda w: pl.BlockSpec((None, tm, w), lambda b, j: (b, j, 0))
    widths = (GROUP_W, GROUP_W, GROUP_W, MLA_HEADS * LANES, MLA_HEADS * LANES, MLA_HEADS * V_HEAD)
    return pl.pallas_call(
        functools.partial(_inproj_kernel, sub=min(tm, INPROJ_SUB)),
        grid=(bs, ls // tm),
        in_specs=[tok(d), pl.BlockSpec((None, SUBLANES, d), mod_map)] + [full(w) for w in wts] + [tab_spec] * 4,
        out_specs=[tok(w) for w in widths],
        out_shape=[jax.ShapeDtypeStruct((bs, ls, w), BF16) for w in widths],
        compiler_params=_cparams("parallel", "parallel"),
        name="inproj",
    )(x, mod, *wts, *tabs)


SCORES_AHEAD = 1


def _attn_kernel(*refs, n_kv):
    q_ref = refs[0]
    k_refs = refs[1:1 + n_kv]
    v_refs = refs[1 + n_kv:1 + 2 * n_kv]
    o_ref = refs[-1]
    tq = q_ref.shape[0]
    head_of_lane = lax.broadcasted_iota(jnp.int32, (tq, MLA_HEADS * V_HEAD), 1) // V_HEAD
    out = jnp.zeros((tq, MLA_HEADS * V_HEAD), F32)

    def scores(hd):
        cols = slice(hd * LANES, (hd + 1) * LANES)
        qh = q_ref[:, cols]
        return [lax.dot_general(qh, kr[:, cols], (((1,), (1,)), ((), ())), preferred_element_type=F32)
                for kr in k_refs]

    ahead = [scores(hd) for hd in range(min(SCORES_AHEAD, MLA_HEADS))]
    for hd in range(MLA_HEADS):
        ss = ahead.pop(0)
        if hd + SCORES_AHEAD < MLA_HEADS:
            ahead.append(scores(hd + SCORES_AHEAD))
        m = functools.reduce(jnp.maximum, [jnp.max(s, axis=-1, keepdims=True) for s in ss])
        ps = [jnp.exp2(s - m) for s in ss]
        den = functools.reduce(jnp.add, [jnp.sum(pp, axis=-1, keepdims=True) for pp in ps])
        o = functools.reduce(jnp.add, [jnp.dot(pp.astype(BF16), vr[...], preferred_element_type=F32)
                                       for pp, vr in zip(ps, v_refs)])
        out = jnp.where(head_of_lane == hd, o / den, out)
    o_ref[...] = out.astype(BF16)


def _attn_call(q, ks, vs, tq):
    bs, lq, _ = q.shape
    n_kv = len(ks)
    kv_spec = lambda a: pl.BlockSpec((None,) + a.shape[1:], lambda b, j: (b, 0, 0))
    return pl.pallas_call(
        functools.partial(_attn_kernel, n_kv=n_kv),
        grid=(bs, lq // tq),
        in_specs=[pl.BlockSpec((None, tq, q.shape[2]), lambda b, j: (b, j, 0))]
        + [kv_spec(a) for a in ks] + [kv_spec(a) for a in vs],
        out_specs=pl.BlockSpec((None, tq, MLA_HEADS * V_HEAD), lambda b, j: (b, j, 0)),
        out_shape=jax.ShapeDtypeStruct((bs, lq, MLA_HEADS * V_HEAD), BF16),
        compiler_params=_cparams("parallel", "arbitrary"),
        name="attention",
    )(q, *ks, *vs)


FOURIER_SUB = 512


def _fourier_kernel(cs_ref, z_ref, cc_ref, scn_ref, wf_ref, o_ref):
    tr = o_ref.shape[0]
    sub = min(tr, FOURIER_SUB)
    starts = range(0, tr, sub)
    tcs = [(jnp.dot(cs_ref[r0:r0 + sub, :], z_ref[...], preferred_element_type=F32),
            jnp.dot(cs_ref[tr + r0:tr + r0 + sub, :], z_ref[...], preferred_element_type=F32))
           for r0 in starts]
    for (tc, ts), r0 in zip(tcs, starts):
        y = (jnp.dot(tc.astype(BF16), cc_ref[...], preferred_element_type=F32)
             + jnp.dot(ts.astype(BF16), scn_ref[...], preferred_element_type=F32))
        o_ref[r0:r0 + sub, :] = jnp.dot(y.astype(BF16), wf_ref[...], preferred_element_type=F32).astype(BF16)


def _fourier_call(zd, cs, cc, scn, wf, tr):
    bs, ls, w = zd.shape
    full = lambda a: pl.BlockSpec(a.shape, lambda r, b: (0,) * a.ndim)
    return pl.pallas_call(
        _fourier_kernel,
        grid=(ls // tr, bs),
        in_specs=[pl.BlockSpec((None, 2 * tr, ls), lambda r, b: (r, 0, 0)),
                  pl.BlockSpec((None, ls, w), lambda r, b: (b, 0, 0)), full(cc), full(scn), full(wf)],
        out_specs=pl.BlockSpec((None, tr, w), lambda r, b: (b, r, 0)),
        out_shape=jax.ShapeDtypeStruct((bs, ls, w), BF16),
        compiler_params=_cparams("parallel", "arbitrary"),
        name="fourier",
    )(cs, zd, cc, scn, wf)


MERGE_SUB = 512


def _merge_kernel(*refs, seq_len, tm, sub, tok_layout, router):
    (x_ref, ya_ref, zb_ref, zbp_ref, zbn_ref, at_ref, yd_ref, mod_ref, pw_ref, ps_ref, wo_ref, n2g_ref) = refs[:12]
    rest = refs[12:]
    if router:
        rw_ref, rest = rest[0], rest[1:]
    xn_ref, h_ref = rest[0], rest[1]
    j = pl.program_id(1)
    nj = pl.num_programs(1)

    starts = range(0, tm, sub)
    parts = []
    for r0 in starts:
        rows = slice(r0, r0 + sub)
        part = jnp.dot(ya_ref[rows, :], wo_ref[0], preferred_element_type=F32)
        part += jnp.dot(at_ref[rows, :], wo_ref[2], preferred_element_type=F32)
        part += jnp.dot(yd_ref[rows, :], wo_ref[3], preferred_element_type=F32)
        parts.append(part)

    zm = zb_ref[...].astype(F32)
    zp = jnp.where(j > 0, zbp_ref[...].astype(F32), 0.0)
    zn = jnp.where(j < nj - 1, zbn_ref[...].astype(F32), 0.0)
    ext = jnp.concatenate([zp, zm, zn], axis=0)
    n = tm + 2 * POOL_HALO

    def ahead(a, k):
        return pltpu.roll(a, n - k, 0)

    tg = j * tm + lax.broadcasted_iota(jnp.int32, (tm, 1), 0)
    low_group = lax.broadcasted_iota(jnp.int32, (tm, LANES), 1) < SUB_W

    def window_mean(d, w):
        cnt = jnp.minimum(tg - w // 2 + w, seq_len) - jnp.maximum(tg - w // 2, 0)
        return ahead(d, POOL_HALO - w // 2)[:tm] * (1.0 / cnt.astype(F32))

    halves = []
    for half in range(GROUP_W // LANES):
        w_lo, w_hi = POOL_WINDOWS[2 * half], POOL_WINDOWS[2 * half + 1]
        d, width, means = ext[:, half * LANES:(half + 1) * LANES], 1, {}
        while width < w_hi:
            d = d + ahead(d, width)
            width *= 2
            if width in (w_lo, w_hi):
                means[width] = window_mean(d, width)
        halves.append(jnp.where(low_group, means[w_lo], means[w_hi]))
    diff = jnp.concatenate(halves, axis=1) - zm
    yb = jnp.dot(diff.astype(BF16), pw_ref[...], preferred_element_type=F32) * ps_ref[...]

    yb = yb.astype(BF16)
    g1 = mod_ref[2:3, :]
    sh2 = mod_ref[3:4, :]
    gain2 = n2g_ref[...] * (1.0 + mod_ref[4:5, :])

    def project(part, rows):
        acc = part + jnp.dot(yb[rows], wo_ref[1], preferred_element_type=F32)
        return x_ref[rows, :] + g1 * acc

    def modulate(xn, r0):
        rows = slice(r0, r0 + sub)
        xn_ref[rows, :] = xn
        h2 = xn * _rms(xn) * gain2 + sh2
        if tok_layout:
            for jj in range(TOK_ROWS):
                h_ref[pl.ds(r0 * TOK_ROWS + jj, sub, stride=TOK_ROWS), :] = h2[:, jj * LANES:(jj + 1) * LANES]
        else:
            h_ref[rows, :] = h2.astype(BF16)
        if router:
            lg_ref = rest[2]
            h_hi = h2.astype(BF16)
            h_lo = (h2 - h_hi.astype(F32)).astype(BF16)
            lg_ref[rows, :] = (jnp.dot(h_hi, rw_ref[0], preferred_element_type=F32)
                               + (jnp.dot(h_lo, rw_ref[0], preferred_element_type=F32)
                                  + jnp.dot(h_hi, rw_ref[1], preferred_element_type=F32)))

    xns = [project(part, slice(r0, r0 + sub)) for part, r0 in zip(parts, starts)]
    for xn, r0 in zip(xns, starts):
        modulate(xn, r0)


def _merge_call(x, ya, zb, attn, yd, mod, shared_mod, wts, tm, tok_layout, router_w=None):
    bs, ls, d = x.shape
    nj = ls // tm
    hb = tm // POOL_HALO
    full = lambda a: pl.BlockSpec(a.shape, lambda b, j: (0,) * a.ndim)
    mod_map = (lambda b, j: (0, 0, 0)) if shared_mod else (lambda b, j: (b, 0, 0))
    tok = lambda w: pl.BlockSpec((None, tm, w), lambda b, j: (b, j, 0))
    in_specs = [
        tok(d), tok(GROUP_W), tok(GROUP_W),
        pl.BlockSpec((None, POOL_HALO, GROUP_W), lambda b, j: (b, jnp.maximum(j * hb - 1, 0), 0)),
        pl.BlockSpec((None, POOL_HALO, GROUP_W), lambda b, j: (b, jnp.minimum((j + 1) * hb, nj * hb - 1), 0)),
        tok(GROUP_W), tok(GROUP_W),
        pl.BlockSpec((None, SUBLANES, d), mod_map),
    ] + [full(w) for w in wts]
    args = [x, ya, zb, zb, zb, attn, yd, mod, *wts]
    out_specs = [tok(d)]
    out_shape = [jax.ShapeDtypeStruct((bs, ls, d), F32)]
    if tok_layout:
        out_specs.append(pl.BlockSpec((tm * TOK_ROWS, LANES), lambda b, j: (b * nj + j, 0)))
        out_shape.append(jax.ShapeDtypeStruct((bs * ls * TOK_ROWS, LANES), F32))
    else:
        out_specs.append(tok(d))
        out_shape.append(jax.ShapeDtypeStruct((bs, ls, d), BF16))
    if router_w is not None:
        in_specs.append(full(router_w))
        args.append(router_w)
        out_specs.append(tok(LANES))
        out_shape.append(jax.ShapeDtypeStruct((bs, ls, LANES), F32))
    return pl.pallas_call(
        functools.partial(_merge_kernel, seq_len=ls, tm=tm, sub=min(tm, MERGE_SUB), tok_layout=tok_layout,
                          router=router_w is not None),
        grid=(bs, nj),
        in_specs=in_specs,
        out_specs=out_specs,
        out_shape=out_shape,
        compiler_params=_cparams("parallel", "parallel"),
        name="merge",
    )(*args)


ACC_COLS = 256
FFN_SUB = 512


def _ffn_kernel(te_ref, tv_ref, *refs, tm, tok_in, tok_out, residual):
    x_ref, wg_ref, wu_ref, wd_ref = refs[:4]
    rest = refs[4:]
    if residual:
        res_ref, mod_ref, rest = rest[0], rest[1], rest[2:]
    o_ref, acc_ref = rest[0], rest[1]
    i = pl.program_id(0)
    f = pl.program_id(1)
    nf = pl.num_programs(1)
    valid = tv_ref[i] > 0

    def store(y):
        if tok_out:
            for jj in range(TOK_ROWS):
                o_ref[pl.ds(jj, tm, stride=TOK_ROWS), :] = y[:, jj * LANES:(jj + 1) * LANES]
        else:
            o_ref[...] = y

    @pl.when(valid)
    def _():
        @pl.when(f == 0)
        def _():
            acc_ref[...] = jnp.zeros_like(acc_ref)
            if tok_in:
                for jj in range(TOK_ROWS):
                    rest[2][:, jj * LANES:(jj + 1) * LANES] = x_ref[pl.ds(jj, tm, stride=TOK_ROWS), :].astype(BF16)

        xb = rest[2][...] if tok_in else x_ref[...]
        tf = wg_ref.shape[1]
        acts = []
        for lo in range(0, tf, FFN_SUB):
            sub = slice(lo, min(lo + FFN_SUB, tf))
            gate = jnp.dot(xb, wg_ref[:, sub], preferred_element_type=F32)
            up = jnp.dot(xb, wu_ref[:, sub], preferred_element_type=F32)
            acts.append((jax.nn.silu(gate) * up).astype(BF16))
        act = jnp.concatenate(acts, axis=1)
        for cb in range(D_MODEL // ACC_COLS):
            cols = slice(cb * ACC_COLS, (cb + 1) * ACC_COLS)
            acc_ref[:, cols] += jnp.dot(act, wd_ref[:, cols], preferred_element_type=F32)

        @pl.when(f == nf - 1)
        def _():
            y = acc_ref[...]
            if residual:
                y = res_ref[...] + mod_ref[5:6, :] * y
            store(y)

    @pl.when(jnp.logical_and(jnp.logical_not(valid), f == nf - 1))
    def _():
        store(jnp.zeros((tm, D_MODEL), F32))


def _ffn_call(x, tile_e, tile_v, wg, wu, wd, tm, tf, tok_in, tok_out, res=None, mod=None, tiles_per_mod=None):
    n_tiles = tile_e.shape[0]
    d, dff = wg.shape[1], wg.shape[2]
    nf = dff // tf
    last = nf - 1
    fsel = lambda i, f, te, tv: jnp.where(tv[i] > 0, f, last)
    xsel = lambda i, f, te, tv: (jnp.where(tv[i] > 0, i, 0), 0)
    x_spec = pl.BlockSpec((tm * TOK_ROWS, LANES), xsel) if tok_in else pl.BlockSpec((tm, d), xsel)
    in_specs = [
        x_spec,
        pl.BlockSpec((None, d, tf), lambda i, f, te, tv: (te[i], 0, fsel(i, f, te, tv))),
        pl.BlockSpec((None, d, tf), lambda i, f, te, tv: (te[i], 0, fsel(i, f, te, tv))),
        pl.BlockSpec((None, tf, d), lambda i, f, te, tv: (te[i], fsel(i, f, te, tv), 0)),
    ]
    args = [x, wg, wu, wd]
    residual = res is not None
    if residual:
        in_specs.append(pl.BlockSpec((tm, d), lambda i, f, te, tv: (i, 0)))
        in_specs.append(pl.BlockSpec((None, SUBLANES, d), lambda i, f, te, tv: (i // tiles_per_mod, 0, 0)))
        args += [res, mod]
    if tok_out:
        out_spec = pl.BlockSpec((tm * TOK_ROWS, LANES), lambda i, f, te, tv: (i, 0))
        out_shape = jax.ShapeDtypeStruct((n_tiles * tm * TOK_ROWS, LANES), F32)
    else:
        out_spec = pl.BlockSpec((tm, d), lambda i, f, te, tv: (i, 0))
        out_shape = jax.ShapeDtypeStruct((n_tiles * tm, d), F32)
    scratch = [pltpu.VMEM((tm, d), F32)]
    if tok_in:
        scratch.append(pltpu.VMEM((tm, d), BF16))
    return pl.pallas_call(
        functools.partial(_ffn_kernel, tm=tm, tok_in=tok_in, tok_out=tok_out, residual=residual),
        grid_spec=pltpu.PrefetchScalarGridSpec(
            num_scalar_prefetch=2, grid=(n_tiles, nf), in_specs=in_specs, out_specs=out_spec,
            scratch_shapes=scratch),
        out_shape=out_shape,
        compiler_params=_cparams("parallel", "arbitrary"),
        name="ffn",
    )(tile_e, tile_v, *args)


def _route_kernel(lg_ref, o_ref, cnt_ref, carry_ref):
    tm = lg_ref.shape[0]
    i = pl.program_id(0)

    @pl.when(i == 0)
    def _():
        carry_ref[...] = jnp.zeros_like(carry_ref)

    lane_i = lax.broadcasted_iota(jnp.int32, (tm, LANES), 1)
    lane = lane_i.astype(F32)
    neg = jnp.float32(-jnp.inf)
    lg = jnp.where(lane_i < N_EXPERTS, lg_ref[...], neg)
    m1 = jnp.max(lg, axis=-1, keepdims=True)
    i1 = jnp.min(jnp.where(lg == m1, lane, float(LANES)), axis=-1, keepdims=True)
    lg2 = jnp.where(lane == i1, neg, lg)
    m2 = jnp.max(lg2, axis=-1, keepdims=True)
    i2 = jnp.min(jnp.where(lg2 == m2, lane, float(LANES)), axis=-1, keepdims=True)
    e2 = jnp.exp(m2 - m1)
    w1 = 1.0 / (1.0 + e2)
    w2 = e2 / (1.0 + e2)
    hit = jnp.logical_or(lane == i1, lane == i2)
    onehot = jnp.where(hit, 1.0, 0.0).astype(BF16)
    r = lax.broadcasted_iota(jnp.int32, (tm, tm), 0)
    c = lax.broadcasted_iota(jnp.int32, (tm, tm), 1)
    before = jnp.where(c < r, 1.0, 0.0).astype(BF16)
    carry = carry_ref[0:1, :]
    cum = jnp.dot(before, onehot, preferred_element_type=F32) + carry
    rank1 = jnp.sum(jnp.where(lane == i1, cum, 0.0), axis=-1, keepdims=True)
    rank2 = jnp.sum(jnp.where(lane == i2, cum, 0.0), axis=-1, keepdims=True)
    total = carry + jnp.sum(onehot.astype(F32), axis=0, keepdims=True)
    carry_ref[...] = jnp.broadcast_to(total, carry_ref.shape)
    cnt_ref[...] = jnp.broadcast_to(total, cnt_ref.shape)
    out = jnp.zeros((tm, LANES), F32)
    for col, val in enumerate((i1, i2, w1, w2, rank1, rank2)):
        out = jnp.where(lane_i == col, val, out)
    o_ref[...] = out


def _route_call(logits, tm):
    n = logits.shape[0]
    return pl.pallas_call(
        _route_kernel,
        grid=(n // tm,),
        in_specs=[pl.BlockSpec((tm, LANES), lambda i: (i, 0))],
        out_specs=[pl.BlockSpec((tm, LANES), lambda i: (i, 0)), pl.BlockSpec((SUBLANES, LANES), lambda i: (0, 0))],
        out_shape=[jax.ShapeDtypeStruct((n, LANES), F32), jax.ShapeDtypeStruct((SUBLANES, LANES), F32)],
        scratch_shapes=[pltpu.VMEM((SUBLANES, LANES), F32)],
        compiler_params=_cparams("arbitrary"),
        name="route",
    )(logits)


ISSUE_UNROLL = 8


def _token_rows(tok):
    start = tok * TOK_ROWS
    return pl.ds(start if isinstance(start, int) else pl.multiple_of(start, TOK_ROWS), TOK_ROWS)


def _token_copy(src, src_tok, dst, dst_tok, sem):
    return pltpu.make_async_copy(src.at[_token_rows(src_tok), :], dst.at[_token_rows(dst_tok), :], sem)


def _for_each_token(n_tok, body):
    def group(g, carry):
        for u in range(ISSUE_UNROLL):
            body(g * ISSUE_UNROLL + u)
        return carry

    lax.fori_loop(0, n_tok // ISSUE_UNROLL, group, 0)


def _dispatch_kernel(pad_ref, pos_ref, h_ref, xs_hbm, zbuf, sem, zsem, *, td, pad_bits, max_tail):
    i = pl.program_id(0)

    def pad_copies():
        for e in range(N_EXPERTS):
            first, length = pad_ref[e], pad_ref[N_EXPERTS + e]
            for b in range(pad_bits):
                size = 1 << b
                tok = first + jnp.bitwise_and(length, size - 1)
                copy = pltpu.make_async_copy(
                    zbuf.at[pl.ds(0, size * TOK_ROWS), :],
                    xs_hbm.at[pl.ds(pl.multiple_of(tok * TOK_ROWS, TOK_ROWS), size * TOK_ROWS), :], zsem)
                yield jnp.bitwise_and(length, size) != 0, copy
        first, pieces = pad_ref[2 * N_EXPERTS], pad_ref[2 * N_EXPERTS + 1]
        piece = zbuf.shape[0]
        for k in range(max_tail):
            copy = pltpu.make_async_copy(
                zbuf, xs_hbm.at[pl.ds(pl.multiple_of(first * TOK_ROWS + k * piece, TOK_ROWS), piece), :], zsem)
            yield k < pieces, copy

    @pl.when(i == 0)
    def _():
        zbuf[...] = jnp.zeros_like(zbuf)
        for needed, copy in pad_copies():
            pl.when(needed)(copy.start)

    def issue(t):
        for s in range(TOP_K):
            _token_copy(h_ref, t, xs_hbm, pos_ref[0, TOP_K * t + s], sem).start(priority=s % 2)

    _for_each_token(td, issue)
    for _ in range(TOP_K):
        pltpu.make_async_copy(h_ref, xs_hbm.at[pl.ds(0, td * TOK_ROWS), :], sem).wait()

    @pl.when(i == 0)
    def _():
        for needed, copy in pad_copies():
            pl.when(needed)(copy.wait)


def _dispatch_call(pos, pad_info, h_tok, n_slots, td, tile):
    n = pos.shape[0]
    pad_bits = tile.bit_length() - 1
    return pl.pallas_call(
        functools.partial(_dispatch_kernel, td=td, pad_bits=pad_bits, max_tail=2 * N_EXPERTS),
        grid_spec=pltpu.PrefetchScalarGridSpec(
            num_scalar_prefetch=1, grid=(n // td,),
            in_specs=[pl.BlockSpec((None, 1, TOP_K * td), lambda i, pad: (i, 0, 0), memory_space=pltpu.SMEM),
                      pl.BlockSpec((td * TOK_ROWS, LANES), lambda i, pad: (i, 0))],
            out_specs=pl.BlockSpec(memory_space=pl.ANY),
            scratch_shapes=[pltpu.VMEM((tile // 2 * TOK_ROWS, LANES), F32), pltpu.SemaphoreType.DMA,
                            pltpu.SemaphoreType.DMA]),
        out_shape=jax.ShapeDtypeStruct((n_slots * TOK_ROWS, LANES), F32),
        compiler_params=_cparams("arbitrary"),
        name="dispatch",
    )(pad_info, pos.reshape(n // td, 1, TOP_K * td), h_tok)


def _combine_kernel(pos_ref, posn_ref, y_hbm, xn_ref, route_ref, mod_ref, o_ref, buf0, buf1, sems, *, tc):
    i = pl.program_id(0)
    n = pl.num_programs(0)
    bufs = (buf0, buf1)

    def gather(p_ref, slot):
        def issue(t):
            for s in range(TOP_K):
                _token_copy(y_hbm, p_ref[0, TOP_K * t + s], bufs[slot], TOP_K * t + s,
                            sems.at[slot]).start(priority=s % 2)

        _for_each_token(tc, issue)

    def finish(slot):
        buf = bufs[slot]
        pltpu.make_async_copy(y_hbm.at[pl.ds(0, TOP_K * tc * TOK_ROWS), :], buf, sems.at[slot]).wait()
        g2 = mod_ref[5:6, :]
        w1 = route_ref[:, 2:3]
        w2 = route_ref[:, 3:4]
        for jj in range(TOK_ROWS):
            cols = slice(jj * LANES, (jj + 1) * LANES)
            ya = buf[pl.ds(jj, tc, stride=TOP_K * TOK_ROWS), :]
            yb = buf[pl.ds(TOK_ROWS + jj, tc, stride=TOP_K * TOK_ROWS), :]
            o_ref[:, cols] = xn_ref[:, cols] + g2[:, cols] * (w1 * ya + w2 * yb)

    @pl.when(i == 0)
    def _():
        gather(pos_ref, 0)

    for slot in range(2):
        @pl.when(i % 2 == slot)
        def _(slot=slot):
            @pl.when(i + 1 < n)
            def _():
                gather(posn_ref, 1 - slot)

            finish(slot)


def _combine_call(pos, y_tok, xn, route, mod, seq_len, tc):
    n, d = xn.shape
    steps = n // tc
    pos3 = pos.reshape(steps, 1, TOP_K * tc)
    buf = pltpu.VMEM((TOP_K * tc * TOK_ROWS, LANES), F32)
    return pl.pallas_call(
        functools.partial(_combine_kernel, tc=tc),
        grid=(steps,),
        in_specs=[pl.BlockSpec((None, 1, TOP_K * tc), lambda i: (i, 0, 0), memory_space=pltpu.SMEM),
                  pl.BlockSpec((None, 1, TOP_K * tc), lambda i: (jnp.minimum(i + 1, steps - 1), 0, 0),
                               memory_space=pltpu.SMEM),
                  pl.BlockSpec(memory_space=pl.ANY),
                  pl.BlockSpec((tc, d), lambda i: (i, 0)),
                  pl.BlockSpec((tc, LANES), lambda i: (i, 0)),
                  pl.BlockSpec((None, SUBLANES, d), lambda i: (i * tc // seq_len, 0, 0))],
        out_specs=pl.BlockSpec((tc, d), lambda i: (i, 0)),
        out_shape=jax.ShapeDtypeStruct((n, d), F32),
        scratch_shapes=[buf, buf, pltpu.SemaphoreType.DMA((2,))],
        compiler_params=_cparams("arbitrary"),
        name="combine",
    )(pos3, pos3, y_tok, xn, route, mod)


def _rot(w):
    return jnp.concatenate([-w[..., 8:16], w[..., 0:8], -w[..., 24:32], w[..., 16:24]], axis=-1)


def _swap(g):
    return jnp.concatenate([g[..., 8:16], g[..., 0:8], g[..., 24:32], g[..., 16:24]], axis=-1)


def _rope_angles(length):
    rows = length // GRID_W
    row = jnp.repeat(jnp.arange(rows, dtype=F32), GRID_W)
    col = jnp.tile(jnp.arange(GRID_W, dtype=F32), rows)
    n_freq = QK_ROPE // 4
    inv_freq = ROPE_BASE ** (-jnp.arange(n_freq, dtype=F32) / n_freq)
    ar = row[:, None] * inv_freq
    ac = col[:, None] * inv_freq
    cos = jnp.concatenate([jnp.cos(ar), jnp.cos(ar), jnp.cos(ac), jnp.cos(ac)], axis=1)
    sin = jnp.concatenate([jnp.sin(ar), jnp.sin(ar), jnp.sin(ac), jnp.sin(ac)], axis=1)
    return cos, sin


def _rope_tables(gain, length, cos, sin, scale):
    g_nope = jnp.broadcast_to(gain[:QK_NOPE], (length, QK_NOPE))
    g_rope = gain[QK_NOPE:]
    pad = jnp.zeros((length, LANES - QK_HEAD), F32)
    ca = jnp.concatenate([g_nope, g_rope * cos, pad], axis=1) * scale
    cb = jnp.concatenate([jnp.zeros((length, QK_NOPE), F32), _swap(g_rope) * sin, pad], axis=1) * scale
    return ca, cb


def _dft_table_kernel(ca_ref, sa_ref, cb_ref, sb_ref, o_ref):
    tr = ca_ref.shape[0]
    first = lax.broadcasted_iota(jnp.int32, (tr, LANES), 1) < GRID_W
    cb, sb = cb_ref[...], sb_ref[...]
    per_tile = LANES // GRID_W
    for t in range(ca_ref.shape[1] // per_tile):
        ca = jnp.where(first, ca_ref[:, per_tile * t:per_tile * t + 1], ca_ref[:, per_tile * t + 1:per_tile * t + 2])
        sa = jnp.where(first, sa_ref[:, per_tile * t:per_tile * t + 1], sa_ref[:, per_tile * t + 1:per_tile * t + 2])
        cols = slice(t * LANES, (t + 1) * LANES)
        o_ref[:tr, cols] = (ca * cb - sa * sb).astype(BF16)
        o_ref[tr:, cols] = (sa * cb + ca * sb).astype(BF16)


def _dft_tables(length, tr):
    k = jnp.arange(length, dtype=jnp.int32)[:, None]
    n1 = jnp.arange(length // GRID_W, dtype=jnp.int32)[None, :]
    n2 = jnp.arange(GRID_W, dtype=jnp.int32)[None, :]
    coarse = ((k * n1 * GRID_W) % length).astype(F32) * (2.0 * jnp.pi / length)
    fine = ((k * n2) % length).astype(F32) * (2.0 * jnp.pi / length)
    norm = length ** -0.5
    fine2 = jnp.concatenate([fine, fine], axis=1)
    cs = pl.pallas_call(
        _dft_table_kernel,
        grid=(length // tr,),
        in_specs=[pl.BlockSpec((tr, length // GRID_W), lambda r: (r, 0))] * 2
        + [pl.BlockSpec((tr, LANES), lambda r: (r, 0))] * 2,
        out_specs=pl.BlockSpec((None, 2 * tr, length), lambda r: (r, 0, 0)),
        out_shape=jax.ShapeDtypeStruct((length // tr, 2 * tr, length), BF16),
        compiler_params=_cparams("parallel"),
        name="dft_table",
    )(jnp.cos(coarse) * norm, jnp.sin(coarse) * norm, jnp.cos(fine2), jnp.sin(fine2))
    kc = jnp.arange(SUB_W, dtype=jnp.int32)
    angc = ((kc[:, None] * kc[None, :]) % SUB_W).astype(F32) * (2.0 * jnp.pi / SUB_W)
    eye = jnp.eye(N_GROUPS, dtype=F32)
    cc = jnp.kron(eye, jnp.cos(angc) * SUB_W ** -0.5).astype(BF16)
    scn = jnp.kron(eye, -jnp.sin(angc) * SUB_W ** -0.5).astype(BF16)
    return cs, cc, scn


def _layer_weights(i, w_in, gmlp_norm_g, spatial_w, spatial_b, pool_w, pool_scale, q_norm_g, w_uq, kv_norm_g,
                   w_ukv, fourier_w, w_out, norm1_g, norm2_g):
    wi = w_in[i]
    ckr = wi[:, OFF_CKR:OFF_D]
    win = jnp.concatenate([wi[:, OFF_A:OFF_B], wi[:, OFF_B:OFF_CQ], wi[:, OFF_D:IN_W], wi[:, OFF_CKV:OFF_CKR],
                           wi[:, OFF_CQ:OFF_CKV], ckr, _rot(ckr)], axis=1).astype(BF16)
    wcat = spatial_w[i].transpose(1, 0, 2).reshape(CHUNK, N_GROUPS * CHUNK).astype(BF16)
    bfull = jnp.repeat(spatial_b[i].T, SUB_W, axis=1)
    qng = jnp.concatenate([q_norm_g[i], jnp.zeros((2 * LANES - Q_LORA,), F32)])[None]
    wq4 = w_uq[i].reshape(Q_LORA, MLA_HEADS, QK_HEAD)
    wq = jnp.concatenate([wq4, _rot(wq4[..., QK_NOPE:])], axis=-1).reshape(Q_LORA, MLA_HEADS * LANES)
    wq = jnp.concatenate([wq, jnp.zeros((2 * LANES - Q_LORA, MLA_HEADS * LANES), F32)], axis=0).astype(BF16)
    wkv4 = w_ukv[i].reshape(KV_LORA, MLA_HEADS, QK_NOPE + V_HEAD)
    wk = jnp.concatenate([wkv4[..., :QK_NOPE], jnp.zeros((KV_LORA, MLA_HEADS, LANES - QK_NOPE), F32)], axis=-1)
    wkv = jnp.concatenate([wk.reshape(KV_LORA, MLA_HEADS * LANES),
                           wkv4[..., QK_NOPE:].reshape(KV_LORA, MLA_HEADS * V_HEAD)], axis=1).astype(BF16)
    in_w = (norm1_g[i][None], win, gmlp_norm_g[i][None], wcat, bfull, qng, wq, kv_norm_g[i][None], wkv)
    pw = jnp.zeros((GROUP_W, GROUP_W), F32)
    for g in range(N_GROUPS):
        pw = pw.at[g * SUB_W:(g + 1) * SUB_W, g * SUB_W:(g + 1) * SUB_W].set(pool_w[i, g])
    merge_w = (pw.astype(BF16), pool_scale[i][None], w_out[i].reshape(N_GROUPS, GROUP_W, D_MODEL).astype(BF16),
               norm2_g[i][None])
    return in_w, merge_w, fourier_w[i].astype(BF16)


def _pad_mod(m):
    m = m.reshape(m.shape[:-1] + (N_ADA, D_MODEL))
    return jnp.concatenate([m, jnp.zeros(m.shape[:-2] + (SUBLANES - N_ADA, D_MODEL), F32)], axis=-2)


def _tiles(seq_len):
    return dict(tin=min(seq_len, 512), tmg=min(seq_len, 1024), tq=min(seq_len, 512), tr=min(seq_len, 1024))


def kernel(x, c, ctx, c_ctx, ada_w, ada_b, norm1_g, w_in, gmlp_norm_g, spatial_w, spatial_b, pool_w, pool_scale,
           q_norm_g, w_uq, kv_norm_g, w_ukv, qk_q_g, qk_k_g, fourier_w, w_out, norm2_g, ffn_w_gate, ffn_w_up,
           ffn_w_down, router_w, moe_w_gate, moe_w_up, moe_w_down):
    bsz, seq, d = x.shape
    lc = ctx.shape[1]
    depth = ada_w.shape[0]
    n_tok = bsz * seq
    tl, tcx = _tiles(seq), _tiles(lc)
    ffn_tm, ffn_tmc, ffn_tf = min(seq, 1024), min(bsz * lc, 1024), 1792
    moe_tm = 1024

    c_all = jnp.concatenate([c, c_ctx[None], jnp.zeros((SUBLANES - 1, d), F32)], axis=0)
    mods = _ada_call(c_all, ada_w, ada_b)
    cos, sin = _rope_angles(seq)
    ones, zeros = jnp.ones((lc, QK_ROPE), F32), jnp.zeros((lc, QK_ROPE), F32)
    dft_l = _dft_tables(seq, tl["tr"])
    dft_c = _dft_tables(lc, tcx["tr"])

    xc = ctx
    for i in range(depth):
        last = i == depth - 1
        moe = i % 2 == 1
        mod_l = _pad_mod(mods[i, :bsz])
        mod_c = _pad_mod(mods[i, bsz:bsz + 1])
        in_w, merge_w, wf = _layer_weights(i, w_in, gmlp_norm_g, spatial_w, spatial_b, pool_w, pool_scale,
                                           q_norm_g, w_uq, kv_norm_g, w_ukv, fourier_w, w_out, norm1_g, norm2_g)
        scale = QK_HEAD ** -0.5 * LOG2_E
        tabs_l = _rope_tables(qk_q_g[i], seq, cos, sin, scale) + _rope_tables(qk_k_g[i], seq, cos, sin, 1.0)
        tabs_c = _rope_tables(qk_q_g[i], lc, ones, zeros, scale) + _rope_tables(qk_k_g[i], lc, ones, zeros, 1.0)

        ya, zb, zd, q, k, v = _inproj_call(x, mod_l, False, in_w, tabs_l, tl["tin"])
        yac, zbc, zdc, qc, kc, vc = _inproj_call(xc, mod_c, True, in_w, tabs_c, tcx["tin"])
        attn = _attn_call(q, [kc, k], [vc, v], tl["tq"])
        yd = _fourier_call(zd, *dft_l, wf, tl["tr"])

        if not moe:
            j = i // 2
            wg, wu, wd = (ffn_w_gate[j:j + 1].astype(BF16), ffn_w_up[j:j + 1].astype(BF16),
                          ffn_w_down[j:j + 1].astype(BF16))
            xn, h2 = _merge_call(x, ya, zb, attn, yd, mod_l, False, merge_w, tl["tmg"], False)
            n_tiles = n_tok // ffn_tm
            x = _ffn_call(h2.reshape(n_tok, d), jnp.zeros((n_tiles,), jnp.int32), jnp.ones((n_tiles,), jnp.int32),
                          wg, wu, wd, ffn_tm, ffn_tf, False, False, res=xn.reshape(n_tok, d), mod=mod_l,
                          tiles_per_mod=seq // ffn_tm).reshape(bsz, seq, d)
            if not last:
                attn_c = _attn_call(qc, [kc], [vc], tcx["tq"])
                ydc = _fourier_call(zdc, *dft_c, wf, tcx["tr"])
                xnc, h2c = _merge_call(xc, yac, zbc, attn_c, ydc, mod_c, True, merge_w, tcx["tmg"], False)
                nc_tok = bsz * lc
                nct = nc_tok // ffn_tmc
                xc = _ffn_call(h2c.reshape(nc_tok, d), jnp.zeros((nct,), jnp.int32), jnp.ones((nct,), jnp.int32),
                               wg, wu, wd, ffn_tmc, ffn_tf, False, False, res=xnc.reshape(nc_tok, d), mod=mod_c,
                               tiles_per_mod=nct).reshape(bsz, lc, d)
        else:
            j = i // 2
            wg, wu, wd = moe_w_gate[j].astype(BF16), moe_w_up[j].astype(BF16), moe_w_down[j].astype(BF16)
            rw32 = jnp.concatenate([router_w[j], jnp.zeros((d, LANES - N_EXPERTS), F32)], axis=1)
            rw_hi = rw32.astype(BF16)
            rw = jnp.stack([rw_hi, (rw32 - rw_hi.astype(F32)).astype(BF16)])

            def moe_ffn(xs, ya_, zb_, attn_, yd_, mod_, shared, tms):
                b_, l_, _ = xs.shape
                n_ = b_ * l_
                xn, h_tok, logits = _merge_call(xs, ya_, zb_, attn_, yd_, mod_, shared, merge_w, tms, True, rw)
                route, counts = _route_call(logits.reshape(n_, LANES), min(n_, 512))
                eidx = route[:, 0:TOP_K].astype(jnp.int32)
                rank = route[:, 4:4 + TOP_K].astype(jnp.int32)
                cnt = counts[0, :N_EXPERTS].astype(jnp.int32)
                padded = (cnt + moe_tm - 1) // moe_tm * moe_tm
                ends = jnp.cumsum(padded)
                starts = ends - padded
                onehot = eidx[..., None] == jnp.arange(N_EXPERTS, dtype=jnp.int32)
                pos = jnp.sum(jnp.where(onehot, starts, 0), axis=-1) + rank
                n_tiles = TOP_K * n_ // moe_tm + N_EXPERTS
                tile_row = jnp.arange(n_tiles, dtype=jnp.int32) * moe_tm
                tile_e = jnp.minimum(jnp.sum(tile_row[:, None] >= ends[None, :], axis=1), N_EXPERTS - 1)
                tile_v = (tile_row < ends[-1]).astype(jnp.int32)
                tail = jnp.stack([ends[-1], (n_tiles * moe_tm - ends[-1]) // (moe_tm // 2)])
                pad_info = jnp.concatenate([starts + cnt, padded - cnt, tail]).astype(jnp.int32)
                xs_tok = _dispatch_call(pos, pad_info, h_tok, n_tiles * moe_tm, min(n_, 512), moe_tm)
                y_tok = _ffn_call(xs_tok, tile_e.astype(jnp.int32), tile_v, wg, wu, wd, moe_tm, ffn_tf, True, True)
                mod_full = jnp.broadcast_to(mod_, (b_, SUBLANES, d)) if shared else mod_
                out = _combine_call(pos, y_tok, xn.reshape(n_, d), route, mod_full, l_, min(l_, 256))
                return out.reshape(b_, l_, d)

            x_new = moe_ffn(x, ya, zb, attn, yd, mod_l, False, tl["tmg"])
            if not last:
                attn_c = _attn_call(qc, [kc], [vc], tcx["tq"])
                ydc = _fourier_call(zdc, *dft_c, wf, tcx["tr"])
                xc = moe_ffn(xc, yac, zbc, attn_c, ydc, mod_c, True, tcx["tmg"])
            x = x_new
    return x
```

```python
import functools

import jax
import jax.numpy as jnp
from jax import lax
from jax.experimental import pallas as pl
from jax.experimental.pallas import tpu as pltpu

F32 = jnp.float32
BF16 = jnp.bfloat16

D_MODEL = 1024
GROUP_W = 256
SUB_W = 64
N_GROUPS = 4
CHUNK = 128
POOL_WINDOWS = (2, 4, 8, 16)
POOL_HALO = 16
MLA_HEADS = 4
QK_NOPE = 64
QK_ROPE = 32
V_HEAD = 64
QK_HEAD = QK_NOPE + QK_ROPE
Q_LORA = 192
KV_LORA = 128
GRID_W = 64
ROPE_BASE = 10000.0
D_FF = 3584
N_EXPERTS = 8
TOP_K = 2
N_ADA = 6
EPS = 1e-6
LOG2_E = 1.4426950408889634

OFF_A, OFF_B, OFF_CQ, OFF_CKV, OFF_CKR, OFF_D, IN_W = 0, 512, 768, 960, 1088, 1120, 1376
P_A, P_B, P_D, P_CKV, P_CQ, P_W = 0, 512, 768, 1024, 1152, 1408

LANES = 128
SUBLANES = 8
TOK_ROWS = D_MODEL // LANES
VMEM_LIMIT = 56 * 1024 * 1024


def _cparams(*sem):
    return pltpu.CompilerParams(dimension_semantics=sem, vmem_limit_bytes=VMEM_LIMIT)


def _rms(x):
    return lax.rsqrt(jnp.mean(x * x, axis=-1, keepdims=True) + EPS)


def _ada_kernel(c_ref, w_ref, b_ref, o_ref):
    ca = jax.nn.silu(c_ref[...])
    o_ref[...] = jnp.dot(ca, w_ref[...], preferred_element_type=F32, precision=lax.Precision.HIGHEST) + b_ref[...]


def _ada_call(c_all, ada_w, ada_b):
    depth, d, n = ada_w.shape
    rows = c_all.shape[0]
    tn = 1024
    return pl.pallas_call(
        _ada_kernel,
        grid=(depth, n // tn),
        in_specs=[
            pl.BlockSpec((rows, d), lambda i, j: (0, 0)),
            pl.BlockSpec((None, d, tn), lambda i, j: (i, 0, j)),
            pl.BlockSpec((None, 1, tn), lambda i, j: (i, 0, j)),
        ],
        out_specs=pl.BlockSpec((None, rows, tn), lambda i, j: (i, 0, j)),
        out_shape=jax.ShapeDtypeStruct((depth, rows, n), F32),
        compiler_params=_cparams("arbitrary", "arbitrary"),
        name="ada_mod",
    )(c_all, ada_w, ada_b.reshape(depth, 1, n))


INPROJ_SUB = 512


def _inproj_kernel(x_ref, mod_ref, n1g_ref, win_ref, gg_ref, wcat_ref, bfull_ref, qng_ref, wq_ref, kvng_ref,
                   wkv_ref, caq_ref, cbq_ref, cak_ref, cbk_ref,
                   ya_ref, zb_ref, zd_ref, q_ref, k_ref, v_ref, *, sub):
    tm = x_ref.shape[0]
    gain1 = n1g_ref[...] * (1.0 + mod_ref[1:2, :])
    sh1 = mod_ref[0:1, :]

    def project(rows):
        xt = x_ref[rows, :]
        h = xt * _rms(xt) * gain1 + sh1
        return jnp.dot(h.astype(BF16), win_ref[...], preferred_element_type=F32)

    grp = lax.broadcasted_iota(jnp.int32, (CHUNK, GROUP_W), 1) // SUB_W
    lane2 = lax.broadcasted_iota(jnp.int32, (sub, 2 * LANES), 1)
    lane = lax.broadcasted_iota(jnp.int32, (sub, LANES), 1)
    head_rows = lax.broadcasted_iota(jnp.int32, (LANES, LANES), 0) < QK_HEAD
    head_ones = jnp.where(head_rows, 1.0, 0.0).astype(BF16)

    def head_rms(blk):
        ssq = jnp.dot((blk * blk).astype(BF16), head_ones, preferred_element_type=F32)
        return lax.rsqrt(ssq / QK_HEAD + EPS)

    def mixers(p, r0):
        rows = slice(r0, r0 + sub)
        uv = jax.nn.gelu(p[:, P_A:P_A + 2 * GROUP_W])
        u = uv[:, :GROUP_W]
        vv = uv[:, GROUP_W:]
        vn = vv * _rms(vv) * gg_ref[...]
        for c in range(sub // CHUNK):
            crow = slice(c * CHUNK, (c + 1) * CHUNK)
            vc = vn[crow]
            vstack = jnp.concatenate([jnp.where(grp == g, vc, 0.0).astype(BF16) for g in range(N_GROUPS)], axis=0)
            mixed = jnp.dot(wcat_ref[...], vstack, preferred_element_type=F32) + bfull_ref[...]
            ya_ref[r0 + c * CHUNK:r0 + (c + 1) * CHUNK, :] = (u[crow] * mixed).astype(BF16)

        zb_ref[rows, :] = p[:, P_B:P_B + GROUP_W].astype(BF16)
        zd_ref[rows, :] = p[:, P_D:P_D + GROUP_W].astype(BF16)

        cq = p[:, P_CQ:P_CQ + 2 * LANES]
        cq_ss = jnp.sum(jnp.where(lane2 < Q_LORA, cq * cq, 0.0), axis=-1, keepdims=True)
        xq = cq * lax.rsqrt(cq_ss / Q_LORA + EPS) * qng_ref[...]
        qraw = jnp.dot(xq.astype(BF16), wq_ref[...], preferred_element_type=F32)
        ckv = p[:, P_CKV:P_CKV + KV_LORA]
        xkv = ckv * _rms(ckv) * kvng_ref[...]
        kvraw = jnp.dot(xkv.astype(BF16), wkv_ref[...], preferred_element_type=F32)
        krope = p[:, P_CQ + LANES:P_CQ + 2 * LANES]
        caq, cbq, cak, cbk = caq_ref[rows, :], cbq_ref[rows, :], cak_ref[rows, :], cbk_ref[rows, :]
        for hd in range(MLA_HEADS):
            cols = slice(hd * LANES, (hd + 1) * LANES)
            qb = qraw[:, cols]
            rq = head_rms(qb)
            q_ref[rows, cols] = ((qb * caq + pltpu.roll(qb, LANES - QK_ROPE, 1) * cbq) * rq).astype(BF16)
            kb = jnp.where(lane < QK_NOPE, kvraw[:, cols], krope)
            rk = head_rms(kb)
            k_ref[rows, cols] = ((kb * cak + pltpu.roll(kb, LANES - QK_ROPE, 1) * cbk) * rk).astype(BF16)
        v_ref[rows, :] = kvraw[:, MLA_HEADS * LANES:].astype(BF16)

    starts = range(0, tm, sub)
    ps = [project(slice(r0, r0 + sub)) for r0 in starts]
    for p, r0 in zip(ps, starts):
        mixers(p, r0)


def _inproj_call(x, mod, shared_mod, wts, tabs, tm):
    bs, ls, d = x.shape
    full = lambda a: pl.BlockSpec(a.shape, lambda b, j: (0,) * a.ndim)
    mod_map = (lambda b, j: (0, 0, 0)) if shared_mod else (lambda b, j: (b, 0, 0))
    tab_spec = pl.BlockSpec((tm, LANES), lambda b, j: (j, 0))
    tok = lambda w: pl.BlockSpec((None, tm, w), lambda b, j: (b, j, 0))
    widths = (GROUP_W, GROUP_W, GROUP_W, MLA_HEADS * LANES, MLA_HEADS * LANES, MLA_HEADS * V_HEAD)
    return pl.pallas_call(
        functools.partial(_inproj_kernel, sub=min(tm, INPROJ_SUB)),
        grid=(bs, ls // tm),
        in_specs=[tok(d), pl.BlockSpec((None, SUBLANES, d), mod_map)] + [full(w) for w in wts] + [tab_spec] * 4,
        out_specs=[tok(w) for w in widths],
        out_shape=[jax.ShapeDtypeStruct((bs, ls, w), BF16) for w in widths],
        compiler_params=_cparams("parallel", "parallel"),
        name="inproj",
    )(x, mod, *wts, *tabs)


SCORES_AHEAD = 1


def _attn_kernel(*refs, n_kv, n_cast):
    q_ref = refs[0]
    k_refs = refs[1:1 + n_kv]
    v_refs = refs[1 + n_kv:1 + 2 * n_kv]
    cast_in = refs[1 + 2 * n_kv:1 + 2 * n_kv + n_cast]
    o_ref = refs[1 + 2 * n_kv + n_cast]
    cast_out = refs[2 + 2 * n_kv + n_cast:]
    for src, dst in zip(cast_in, cast_out):
        dst[...] = src[...].astype(BF16)
    tq = q_ref.shape[0]
    head_of_lane = lax.broadcasted_iota(jnp.int32, (tq, MLA_HEADS * V_HEAD), 1) // V_HEAD
    out = jnp.zeros((tq, MLA_HEADS * V_HEAD), F32)

    def scores(hd):
        cols = slice(hd * LANES, (hd + 1) * LANES)
        qh = q_ref[:, cols]
        return [lax.dot_general(qh, kr[:, cols], (((1,), (1,)), ((), ())), preferred_element_type=F32)
                for kr in k_refs]

    ahead = [scores(hd) for hd in range(min(SCORES_AHEAD, MLA_HEADS))]
    for hd in range(MLA_HEADS):
        ss = ahead.pop(0)
        if hd + SCORES_AHEAD < MLA_HEADS:
            ahead.append(scores(hd + SCORES_AHEAD))
        m = functools.reduce(jnp.maximum, [jnp.max(s, axis=-1, keepdims=True) for s in ss])
        ps = [jnp.exp2(s - m) for s in ss]
        den = functools.reduce(jnp.add, [jnp.sum(pp, axis=-1, keepdims=True) for pp in ps])
        o = functools.reduce(jnp.add, [jnp.dot(pp.astype(BF16), vr[...], preferred_element_type=F32)
                                       for pp, vr in zip(ps, v_refs)])
        out = jnp.where(head_of_lane == hd, o / den, out)
    o_ref[...] = out.astype(BF16)


BF16_ROWS = 16


def _cast_blocks(rows, steps):
    n = steps
    while rows % n or (rows // n) % BF16_ROWS:
        n //= 2
    return n


def _attn_call(q, ks, vs, tq, cast=()):
    bs, lq, _ = q.shape
    n_kv = len(ks)
    nj = lq // tq
    steps = bs * nj
    kv_spec = lambda a: pl.BlockSpec((None,) + a.shape[1:], lambda b, j: (b, 0, 0))

    def cast_spec(a):
        n = _cast_blocks(a.shape[0], steps)
        return pl.BlockSpec((a.shape[0] // n, a.shape[1]), lambda b, j: ((b * nj + j) * n // steps, 0))

    out = pl.pallas_call(
        functools.partial(_attn_kernel, n_kv=n_kv, n_cast=len(cast)),
        grid=(bs, nj),
        in_specs=[pl.BlockSpec((None, tq, q.shape[2]), lambda b, j: (b, j, 0))]
        + [kv_spec(a) for a in ks] + [kv_spec(a) for a in vs] + [cast_spec(a) for a in cast],
        out_specs=[pl.BlockSpec((None, tq, MLA_HEADS * V_HEAD), lambda b, j: (b, j, 0))]
        + [cast_spec(a) for a in cast],
        out_shape=[jax.ShapeDtypeStruct((bs, lq, MLA_HEADS * V_HEAD), BF16)]
        + [jax.ShapeDtypeStruct(a.shape, BF16) for a in cast],
        compiler_params=_cparams("arbitrary", "arbitrary"),
        name="attention",
    )(q, *ks, *vs, *cast)
    return out if cast else out[0]


FOURIER_SUB = 512


def _fourier_kernel(cs_ref, z_ref, cc_ref, scn_ref, wf_ref, o_ref):
    tr = o_ref.shape[0]
    sub = min(tr, FOURIER_SUB)
    starts = range(0, tr, sub)
    tcs = [(jnp.dot(cs_ref[r0:r0 + sub, :], z_ref[...], preferred_element_type=F32),
            jnp.dot(cs_ref[tr + r0:tr + r0 + sub, :], z_ref[...], preferred_element_type=F32))
           for r0 in starts]
    for (tc, ts), r0 in zip(tcs, starts):
        y = (jnp.dot(tc.astype(BF16), cc_ref[...], preferred_element_type=F32)
             + jnp.dot(ts.astype(BF16), scn_ref[...], preferred_element_type=F32))
        o_ref[r0:r0 + sub, :] = jnp.dot(y.astype(BF16), wf_ref[...], preferred_element_type=F32).astype(BF16)


def _fourier_call(zd, cs, cc, scn, wf, tr):
    bs, ls, w = zd.shape
    full = lambda a: pl.BlockSpec(a.shape, lambda r, b: (0,) * a.ndim)
    return pl.pallas_call(
        _fourier_kernel,
        grid=(ls // tr, bs),
        in_specs=[pl.BlockSpec((None, 2 * tr, ls), lambda r, b: (r, 0, 0)),
                  pl.BlockSpec((None, ls, w), lambda r, b: (b, 0, 0)), full(cc), full(scn), full(wf)],
        out_specs=pl.BlockSpec((None, tr, w), lambda r, b: (b, r, 0)),
        out_shape=jax.ShapeDtypeStruct((bs, ls, w), BF16),
        compiler_params=_cparams("parallel", "arbitrary"),
        name="fourier",
    )(cs, zd, cc, scn, wf)


MERGE_SUB = 512


def _merge_kernel(*refs, seq_len, tm, sub, tok_layout, router):
    (x_ref, ya_ref, zb_ref, zbp_ref, zbn_ref, at_ref, yd_ref, mod_ref, pw_ref, ps_ref, wo_ref, n2g_ref) = refs[:12]
    rest = refs[12:]
    if router:
        rw_ref, rest = rest[0], rest[1:]
    xn_ref, h_ref = rest[0], rest[1]
    j = pl.program_id(1)
    nj = pl.num_programs(1)

    starts = range(0, tm, sub)
    parts = []
    for r0 in starts:
        rows = slice(r0, r0 + sub)
        part = jnp.dot(ya_ref[rows, :], wo_ref[0], preferred_element_type=F32)
        part += jnp.dot(at_ref[rows, :], wo_ref[2], preferred_element_type=F32)
        part += jnp.dot(yd_ref[rows, :], wo_ref[3], preferred_element_type=F32)
        parts.append(part)

    zm = zb_ref[...].astype(F32)
    zp = jnp.where(j > 0, zbp_ref[...].astype(F32), 0.0)
    zn = jnp.where(j < nj - 1, zbn_ref[...].astype(F32), 0.0)
    ext = jnp.concatenate([zp, zm, zn], axis=0)
    n = tm + 2 * POOL_HALO

    def ahead(a, k):
        return pltpu.roll(a, n - k, 0)

    tg = j * tm + lax.broadcasted_iota(jnp.int32, (tm, 1), 0)
    low_group = lax.broadcasted_iota(jnp.int32, (tm, LANES), 1) < SUB_W

    def window_mean(d, w):
        cnt = jnp.minimum(tg - w // 2 + w, seq_len) - jnp.maximum(tg - w // 2, 0)
        return ahead(d, POOL_HALO - w // 2)[:tm] * (1.0 / cnt.astype(F32))

    halves = []
    for half in range(GROUP_W // LANES):
        w_lo, w_hi = POOL_WINDOWS[2 * half], POOL_WINDOWS[2 * half + 1]
        d, width, means = ext[:, half * LANES:(half + 1) * LANES], 1, {}
        while width < w_hi:
            d = d + ahead(d, width)
            width *= 2
            if width in (w_lo, w_hi):
                means[width] = window_mean(d, width)
        halves.append(jnp.where(low_group, means[w_lo], means[w_hi]))
    diff = jnp.concatenate(halves, axis=1) - zm
    yb = jnp.dot(diff.astype(BF16), pw_ref[...], preferred_element_type=F32) * ps_ref[...]

    yb = yb.astype(BF16)
    g1 = mod_ref[2:3, :]
    sh2 = mod_ref[3:4, :]
    gain2 = n2g_ref[...] * (1.0 + mod_ref[4:5, :])

    def project(part, rows):
        acc = part + jnp.dot(yb[rows], wo_ref[1], preferred_element_type=F32)
        return x_ref[rows, :] + g1 * acc

    def modulate(xn, r0):
        rows = slice(r0, r0 + sub)
        xn_ref[rows, :] = xn
        h2 = xn * _rms(xn) * gain2 + sh2
        if tok_layout:
            for jj in range(TOK_ROWS):
                h_ref[pl.ds(r0 * TOK_ROWS + jj, sub, stride=TOK_ROWS), :] = h2[:, jj * LANES:(jj + 1) * LANES]
        else:
            h_ref[rows, :] = h2.astype(BF16)
        if router:
            lg_ref = rest[2]
            h_hi = h2.astype(BF16)
            h_lo = (h2 - h_hi.astype(F32)).astype(BF16)
            lg_ref[rows, :] = (jnp.dot(h_hi, rw_ref[0], preferred_element_type=F32)
                               + (jnp.dot(h_lo, rw_ref[0], preferred_element_type=F32)
                                  + jnp.dot(h_hi, rw_ref[1], preferred_element_type=F32)))

    xns = [project(part, slice(r0, r0 + sub)) for part, r0 in zip(parts, starts)]
    for xn, r0 in zip(xns, starts):
        modulate(xn, r0)


def _merge_call(x, ya, zb, attn, yd, mod, shared_mod, wts, tm, tok_layout, router_w=None):
    bs, ls, d = x.shape
    nj = ls // tm
    hb = tm // POOL_HALO
    full = lambda a: pl.BlockSpec(a.shape, lambda b, j: (0,) * a.ndim)
    mod_map = (lambda b, j: (0, 0, 0)) if shared_mod else (lambda b, j: (b, 0, 0))
    tok = lambda w: pl.BlockSpec((None, tm, w), lambda b, j: (b, j, 0))
    in_specs = [
        tok(d), tok(GROUP_W), tok(GROUP_W),
        pl.BlockSpec((None, POOL_HALO, GROUP_W), lambda b, j: (b, jnp.maximum(j * hb - 1, 0), 0)),
        pl.BlockSpec((None, POOL_HALO, GROUP_W), lambda b, j: (b, jnp.minimum((j + 1) * hb, nj * hb - 1), 0)),
        tok(GROUP_W), tok(GROUP_W),
        pl.BlockSpec((None, SUBLANES, d), mod_map),
    ] + [full(w) for w in wts]
    args = [x, ya, zb, zb, zb, attn, yd, mod, *wts]
    out_specs = [tok(d)]
    out_shape = [jax.ShapeDtypeStruct((bs, ls, d), F32)]
    if tok_layout:
        out_specs.append(pl.BlockSpec((tm * TOK_ROWS, LANES), lambda b, j: (b * nj + j, 0)))
        out_shape.append(jax.ShapeDtypeStruct((bs * ls * TOK_ROWS, LANES), F32))
    else:
        out_specs.append(tok(d))
        out_shape.append(jax.ShapeDtypeStruct((bs, ls, d), BF16))
    if router_w is not None:
        in_specs.append(full(router_w))
        args.append(router_w)
        out_specs.append(tok(LANES))
        out_shape.append(jax.ShapeDtypeStruct((bs, ls, LANES), F32))
    return pl.pallas_call(
        functools.partial(_merge_kernel, seq_len=ls, tm=tm, sub=min(tm, MERGE_SUB), tok_layout=tok_layout,
                          router=router_w is not None),
        grid=(bs, nj),
        in_specs=in_specs,
        out_specs=out_specs,
        out_shape=out_shape,
        compiler_params=_cparams("parallel", "parallel"),
        name="merge",
    )(*args)


ACC_COLS = 256
FFN_SUB = 512


def _ffn_kernel(te_ref, tv_ref, *refs, tm, tok_in, tok_out, residual):
    x_ref, wg_ref, wu_ref, wd_ref = refs[:4]
    rest = refs[4:]
    if residual:
        res_ref, mod_ref, rest = rest[0], rest[1], rest[2:]
    o_ref, acc_ref = rest[0], rest[1]
    i = pl.program_id(0)
    f = pl.program_id(1)
    nf = pl.num_programs(1)
    valid = tv_ref[i] > 0

    def store(y):
        if tok_out:
            for jj in range(TOK_ROWS):
                o_ref[pl.ds(jj, tm, stride=TOK_ROWS), :] = y[:, jj * LANES:(jj + 1) * LANES]
        else:
            o_ref[...] = y

    @pl.when(valid)
    def _():
        @pl.when(f == 0)
        def _():
            acc_ref[...] = jnp.zeros_like(acc_ref)
            if tok_in:
                for jj in range(TOK_ROWS):
                    rest[2][:, jj * LANES:(jj + 1) * LANES] = x_ref[pl.ds(jj, tm, stride=TOK_ROWS), :].astype(BF16)

        xb = rest[2][...] if tok_in else x_ref[...]
        tf = wg_ref.shape[1]
        acts = []
        for lo in range(0, tf, FFN_SUB):
            sub = slice(lo, min(lo + FFN_SUB, tf))
            gate = jnp.dot(xb, wg_ref[:, sub], preferred_element_type=F32)
            up = jnp.dot(xb, wu_ref[:, sub], preferred_element_type=F32)
            acts.append((jax.nn.silu(gate) * up).astype(BF16))
        act = jnp.concatenate(acts, axis=1)
        for cb in range(D_MODEL // ACC_COLS):
            cols = slice(cb * ACC_COLS, (cb + 1) * ACC_COLS)
            acc_ref[:, cols] += jnp.dot(act, wd_ref[:, cols], preferred_element_type=F32)

        @pl.when(f == nf - 1)
        def _():
            y = acc_ref[...]
            if residual:
                y = res_ref[...] + mod_ref[5:6, :] * y
            store(y)

    @pl.when(jnp.logical_and(jnp.logical_not(valid), f == nf - 1))
    def _():
        store(jnp.zeros((tm, D_MODEL), F32))


def _ffn_call(x, tile_e, tile_v, wg, wu, wd, tm, tf, tok_in, tok_out, res=None, mod=None, tiles_per_mod=None):
    n_tiles = tile_e.shape[0]
    d, dff = wg.shape[1], wg.shape[2]
    nf = dff // tf
    last = nf - 1
    fsel = lambda i, f, te, tv: jnp.where(tv[i] > 0, f, last)
    xsel = lambda i, f, te, tv: (jnp.where(tv[i] > 0, i, 0), 0)
    x_spec = pl.BlockSpec((tm * TOK_ROWS, LANES), xsel) if tok_in else pl.BlockSpec((tm, d), xsel)
    in_specs = [
        x_spec,
        pl.BlockSpec((None, d, tf), lambda i, f, te, tv: (te[i], 0, fsel(i, f, te, tv))),
        pl.BlockSpec((None, d, tf), lambda i, f, te, tv: (te[i], 0, fsel(i, f, te, tv))),
        pl.BlockSpec((None, tf, d), lambda i, f, te, tv: (te[i], fsel(i, f, te, tv), 0)),
    ]
    args = [x, wg, wu, wd]
    residual = res is not None
    if residual:
        in_specs.append(pl.BlockSpec((tm, d), lambda i, f, te, tv: (i, 0)))
        in_specs.append(pl.BlockSpec((None, SUBLANES, d), lambda i, f, te, tv: (i // tiles_per_mod, 0, 0)))
        args += [res, mod]
    if tok_out:
        out_spec = pl.BlockSpec((tm * TOK_ROWS, LANES), lambda i, f, te, tv: (i, 0))
        out_shape = jax.ShapeDtypeStruct((n_tiles * tm * TOK_ROWS, LANES), F32)
    else:
        out_spec = pl.BlockSpec((tm, d), lambda i, f, te, tv: (i, 0))
        out_shape = jax.ShapeDtypeStruct((n_tiles * tm, d), F32)
    scratch = [pltpu.VMEM((tm, d), F32)]
    if tok_in:
        scratch.append(pltpu.VMEM((tm, d), BF16))
    return pl.pallas_call(
        functools.partial(_ffn_kernel, tm=tm, tok_in=tok_in, tok_out=tok_out, residual=residual),
        grid_spec=pltpu.PrefetchScalarGridSpec(
            num_scalar_prefetch=2, grid=(n_tiles, nf), in_specs=in_specs, out_specs=out_spec,
            scratch_shapes=scratch),
        out_shape=out_shape,
        compiler_params=_cparams("parallel", "arbitrary"),
        name="ffn",
    )(tile_e, tile_v, *args)


def _route_kernel(lg_ref, o_ref, cnt_ref, carry_ref):
    tm = lg_ref.shape[0]
    i = pl.program_id(0)

    @pl.when(i == 0)
    def _():
        carry_ref[...] = jnp.zeros_like(carry_ref)

    lane_i = lax.broadcasted_iota(jnp.int32, (tm, LANES), 1)
    lane = lane_i.astype(F32)
    neg = jnp.float32(-jnp.inf)
    lg = jnp.where(lane_i < N_EXPERTS, lg_ref[...], neg)
    m1 = jnp.max(lg, axis=-1, keepdims=True)
    i1 = jnp.min(jnp.where(lg == m1, lane, float(LANES)), axis=-1, keepdims=True)
    lg2 = jnp.where(lane == i1, neg, lg)
    m2 = jnp.max(lg2, axis=-1, keepdims=True)
    i2 = jnp.min(jnp.where(lg2 == m2, lane, float(LANES)), axis=-1, keepdims=True)
    e2 = jnp.exp(m2 - m1)
    w1 = 1.0 / (1.0 + e2)
    w2 = e2 / (1.0 + e2)
    hit = jnp.logical_or(lane == i1, lane == i2)
    onehot = jnp.where(hit, 1.0, 0.0).astype(BF16)
    r = lax.broadcasted_iota(jnp.int32, (tm, tm), 0)
    c = lax.broadcasted_iota(jnp.int32, (tm, tm), 1)
    before = jnp.where(c < r, 1.0, 0.0).astype(BF16)
    carry = carry_ref[0:1, :]
    cum = jnp.dot(before, onehot, preferred_element_type=F32) + carry
    rank1 = jnp.sum(jnp.where(lane == i1, cum, 0.0), axis=-1, keepdims=True)
    rank2 = jnp.sum(jnp.where(lane == i2, cum, 0.0), axis=-1, keepdims=True)
    total = carry + jnp.sum(onehot.astype(F32), axis=0, keepdims=True)
    carry_ref[...] = jnp.broadcast_to(total, carry_ref.shape)
    cnt_ref[...] = jnp.broadcast_to(total, cnt_ref.shape)
    out = jnp.zeros((tm, LANES), F32)
    for col, val in enumerate((i1, i2, w1, w2, rank1, rank2)):
        out = jnp.where(lane_i == col, val, out)
    o_ref[...] = out


def _route_call(logits, tm):
    n = logits.shape[0]
    return pl.pallas_call(
        _route_kernel,
        grid=(n // tm,),
        in_specs=[pl.BlockSpec((tm, LANES), lambda i: (i, 0))],
        out_specs=[pl.BlockSpec((tm, LANES), lambda i: (i, 0)), pl.BlockSpec((SUBLANES, LANES), lambda i: (0, 0))],
        out_shape=[jax.ShapeDtypeStruct((n, LANES), F32), jax.ShapeDtypeStruct((SUBLANES, LANES), F32)],
        scratch_shapes=[pltpu.VMEM((SUBLANES, LANES), F32)],
        compiler_params=_cparams("arbitrary"),
        name="route",
    )(logits)


ISSUE_UNROLL = 8


def _token_rows(tok):
    start = tok * TOK_ROWS
    return pl.ds(start if isinstance(start, int) else pl.multiple_of(start, TOK_ROWS), TOK_ROWS)


def _token_copy(src, src_tok, dst, dst_tok, sem):
    return pltpu.make_async_copy(src.at[_token_rows(src_tok), :], dst.at[_token_rows(dst_tok), :], sem)


def _for_each_token(n_tok, body):
    def group(g, carry):
        for u in range(ISSUE_UNROLL):
            body(g * ISSUE_UNROLL + u)
        return carry

    lax.fori_loop(0, n_tok // ISSUE_UNROLL, group, 0)


def _dispatch_kernel(pad_ref, pos_ref, h_ref, xs_hbm, zbuf, sem, zsem, *, td, pad_bits, max_tail):
    i = pl.program_id(0)

    def pad_copies():
        for e in range(N_EXPERTS):
            first, length = pad_ref[e], pad_ref[N_EXPERTS + e]
            for b in range(pad_bits):
                size = 1 << b
                tok = first + jnp.bitwise_and(length, size - 1)
                copy = pltpu.make_async_copy(
                    zbuf.at[pl.ds(0, size * TOK_ROWS), :],
                    xs_hbm.at[pl.ds(pl.multiple_of(tok * TOK_ROWS, TOK_ROWS), size * TOK_ROWS), :], zsem)
                yield jnp.bitwise_and(length, size) != 0, copy
        first, pieces = pad_ref[2 * N_EXPERTS], pad_ref[2 * N_EXPERTS + 1]
        piece = zbuf.shape[0]
        for k in range(max_tail):
            copy = pltpu.make_async_copy(
                zbuf, xs_hbm.at[pl.ds(pl.multiple_of(first * TOK_ROWS + k * piece, TOK_ROWS), piece), :], zsem)
            yield k < pieces, copy

    @pl.when(i == 0)
    def _():
        zbuf[...] = jnp.zeros_like(zbuf)
        for needed, copy in pad_copies():
            pl.when(needed)(copy.start)

    def issue(t):
        for s in range(TOP_K):
            _token_copy(h_ref, t, xs_hbm, pos_ref[0, TOP_K * t + s], sem).start(priority=s % 2)

    _for_each_token(td, issue)
    for _ in range(TOP_K):
        pltpu.make_async_copy(h_ref, xs_hbm.at[pl.ds(0, td * TOK_ROWS), :], sem).wait()

    @pl.when(i == 0)
    def _():
        for needed, copy in pad_copies():
            pl.when(needed)(copy.wait)


def _dispatch_call(pos, pad_info, h_tok, n_slots, td, tile):
    n = pos.shape[0]
    pad_bits = tile.bit_length() - 1
    return pl.pallas_call(
        functools.partial(_dispatch_kernel, td=td, pad_bits=pad_bits, max_tail=2 * N_EXPERTS),
        grid_spec=pltpu.PrefetchScalarGridSpec(
            num_scalar_prefetch=1, grid=(n // td,),
            in_specs=[pl.BlockSpec((None, 1, TOP_K * td), lambda i, pad: (i, 0, 0), memory_space=pltpu.SMEM),
                      pl.BlockSpec((td * TOK_ROWS, LANES), lambda i, pad: (i, 0))],
            out_specs=pl.BlockSpec(memory_space=pl.ANY),
            scratch_shapes=[pltpu.VMEM((tile // 2 * TOK_ROWS, LANES), F32), pltpu.SemaphoreType.DMA,
                            pltpu.SemaphoreType.DMA]),
        out_shape=jax.ShapeDtypeStruct((n_slots * TOK_ROWS, LANES), F32),
        compiler_params=_cparams("arbitrary"),
        name="dispatch",
    )(pad_info, pos.reshape(n // td, 1, TOP_K * td), h_tok)


COMBINE_AHEAD = 2


def _combine_kernel(*refs, tc):
    pos_refs = refs[:COMBINE_AHEAD + 1]
    y_hbm, xn_ref, route_ref, mod_ref, o_ref = refs[COMBINE_AHEAD + 1:COMBINE_AHEAD + 6]
    bufs = refs[COMBINE_AHEAD + 6:-1]
    sems = refs[-1]
    n_buf = len(bufs)
    i = pl.program_id(0)
    n = pl.num_programs(0)

    def issue(p_ref, slot, t):
        for s in range(TOP_K):
            _token_copy(y_hbm, p_ref[0, TOP_K * t + s], bufs[slot], TOP_K * t + s,
                        sems.at[slot]).start(priority=s % 2)

    def blend(slot):
        buf = bufs[slot]
        g2 = mod_ref[5:6, :]
        w1 = route_ref[:, 2:3]
        w2 = route_ref[:, 3:4]
        for jj in range(TOK_ROWS):
            cols = slice(jj * LANES, (jj + 1) * LANES)
            ya = buf[pl.ds(jj, tc, stride=TOP_K * TOK_ROWS), :]
            yb = buf[pl.ds(TOK_ROWS + jj, tc, stride=TOP_K * TOK_ROWS), :]
            o_ref[:, cols] = xn_ref[:, cols] + g2[:, cols] * (w1 * ya + w2 * yb)

    @pl.when(i == 0)
    def _():
        for ahead in range(COMBINE_AHEAD):
            @pl.when(ahead < n)
            def _(ahead=ahead):
                _for_each_token(tc, functools.partial(issue, pos_refs[ahead], ahead))

    for slot in range(n_buf):
        @pl.when(i % n_buf == slot)
        def _(slot=slot):
            pltpu.make_async_copy(y_hbm.at[pl.ds(0, TOP_K * tc * TOK_ROWS), :], bufs[slot], sems.at[slot]).wait()

            @pl.when(i + COMBINE_AHEAD < n)
            def _():
                for t in range(tc):
                    issue(pos_refs[COMBINE_AHEAD], (slot + COMBINE_AHEAD) % n_buf, t)
                blend(slot)

            @pl.when(i + COMBINE_AHEAD >= n)
            def _():
                blend(slot)


def _combine_call(pos, y_tok, xn, route, mod, seq_len, tc):
    n, d = xn.shape
    steps = n // tc
    pos3 = pos.reshape(steps, 1, TOP_K * tc)
    pos_spec = lambda k: pl.BlockSpec((None, 1, TOP_K * tc), lambda i: (jnp.minimum(i + k, steps - 1), 0, 0),
                                      memory_space=pltpu.SMEM)
    n_buf = COMBINE_AHEAD + 1
    return pl.pallas_call(
        functools.partial(_combine_kernel, tc=tc),
        grid=(steps,),
        in_specs=[pos_spec(k) for k in range(n_buf)]
        + [pl.BlockSpec(memory_space=pl.ANY),
           pl.BlockSpec((tc, d), lambda i: (i, 0)),
           pl.BlockSpec((tc, LANES), lambda i: (i, 0)),
           pl.BlockSpec((None, SUBLANES, d), lambda i: (i * tc // seq_len, 0, 0))],
        out_specs=pl.BlockSpec((tc, d), lambda i: (i, 0)),
        out_shape=jax.ShapeDtypeStruct((n, d), F32),
        scratch_shapes=[pltpu.VMEM((TOP_K * tc * TOK_ROWS, LANES), F32)] * n_buf
        + [pltpu.SemaphoreType.DMA((n_buf,))],
        compiler_params=_cparams("arbitrary"),
        name="combine",
    )(*([pos3] * n_buf), y_tok, xn, route, mod)


def _rot(w):
    return jnp.concatenate([-w[..., 8:16], w[..., 0:8], -w[..., 24:32], w[..., 16:24]], axis=-1)


def _swap(g):
    return jnp.concatenate([g[..., 8:16], g[..., 0:8], g[..., 24:32], g[..., 16:24]], axis=-1)


def _rope_angles(length):
    rows = length // GRID_W
    row = jnp.repeat(jnp.arange(rows, dtype=F32), GRID_W)
    col = jnp.tile(jnp.arange(GRID_W, dtype=F32), rows)
    n_freq = QK_ROPE // 4
    inv_freq = ROPE_BASE ** (-jnp.arange(n_freq, dtype=F32) / n_freq)
    ar = row[:, None] * inv_freq
    ac = col[:, None] * inv_freq
    cos = jnp.concatenate([jnp.cos(ar), jnp.cos(ar), jnp.cos(ac), jnp.cos(ac)], axis=1)
    sin = jnp.concatenate([jnp.sin(ar), jnp.sin(ar), jnp.sin(ac), jnp.sin(ac)], axis=1)
    return cos, sin


def _rope_tables(gain, length, cos, sin, scale):
    g_nope = jnp.broadcast_to(gain[:QK_NOPE], (length, QK_NOPE))
    g_rope = gain[QK_NOPE:]
    pad = jnp.zeros((length, LANES - QK_HEAD), F32)
    ca = jnp.concatenate([g_nope, g_rope * cos, pad], axis=1) * scale
    cb = jnp.concatenate([jnp.zeros((length, QK_NOPE), F32), _swap(g_rope) * sin, pad], axis=1) * scale
    return ca, cb


def _dft_table_kernel(ca_ref, sa_ref, cb_ref, sb_ref, o_ref):
    tr = ca_ref.shape[0]
    first = lax.broadcasted_iota(jnp.int32, (tr, LANES), 1) < GRID_W
    cb, sb = cb_ref[...], sb_ref[...]
    per_tile = LANES // GRID_W
    for t in range(ca_ref.shape[1] // per_tile):
        ca = jnp.where(first, ca_ref[:, per_tile * t:per_tile * t + 1], ca_ref[:, per_tile * t + 1:per_tile * t + 2])
        sa = jnp.where(first, sa_ref[:, per_tile * t:per_tile * t + 1], sa_ref[:, per_tile * t + 1:per_tile * t + 2])
        cols = slice(t * LANES, (t + 1) * LANES)
        o_ref[:tr, cols] = (ca * cb - sa * sb).astype(BF16)
        o_ref[tr:, cols] = (sa * cb + ca * sb).astype(BF16)


def _dft_tables(length, tr):
    k = jnp.arange(length, dtype=jnp.int32)[:, None]
    n1 = jnp.arange(length // GRID_W, dtype=jnp.int32)[None, :]
    n2 = jnp.arange(GRID_W, dtype=jnp.int32)[None, :]
    coarse = ((k * n1 * GRID_W) % length).astype(F32) * (2.0 * jnp.pi / length)
    fine = ((k * n2) % length).astype(F32) * (2.0 * jnp.pi / length)
    norm = length ** -0.5
    fine2 = jnp.concatenate([fine, fine], axis=1)
    cs = pl.pallas_call(
        _dft_table_kernel,
        grid=(length // tr,),
        in_specs=[pl.BlockSpec((tr, length // GRID_W), lambda r: (r, 0))] * 2
        + [pl.BlockSpec((tr, LANES), lambda r: (r, 0))] * 2,
        out_specs=pl.BlockSpec((None, 2 * tr, length), lambda r: (r, 0, 0)),
        out_shape=jax.ShapeDtypeStruct((length // tr, 2 * tr, length), BF16),
        compiler_params=_cparams("parallel"),
        name="dft_table",
    )(jnp.cos(coarse) * norm, jnp.sin(coarse) * norm, jnp.cos(fine2), jnp.sin(fine2))
    kc = jnp.arange(SUB_W, dtype=jnp.int32)
    angc = ((kc[:, None] * kc[None, :]) % SUB_W).astype(F32) * (2.0 * jnp.pi / SUB_W)
    eye = jnp.eye(N_GROUPS, dtype=F32)
    cc = jnp.kron(eye, jnp.cos(angc) * SUB_W ** -0.5).astype(BF16)
    scn = jnp.kron(eye, -jnp.sin(angc) * SUB_W ** -0.5).astype(BF16)
    return cs, cc, scn


def _layer_weights(i, w_in, gmlp_norm_g, spatial_w, spatial_b, pool_w, pool_scale, q_norm_g, w_uq, kv_norm_g,
                   w_ukv, fourier_w, w_out, norm1_g, norm2_g):
    wi = w_in[i]
    ckr = wi[:, OFF_CKR:OFF_D]
    win = jnp.concatenate([wi[:, OFF_A:OFF_B], wi[:, OFF_B:OFF_CQ], wi[:, OFF_D:IN_W], wi[:, OFF_CKV:OFF_CKR],
                           wi[:, OFF_CQ:OFF_CKV], ckr, _rot(ckr)], axis=1).astype(BF16)
    wcat = spatial_w[i].transpose(1, 0, 2).reshape(CHUNK, N_GROUPS * CHUNK).astype(BF16)
    bfull = jnp.repeat(spatial_b[i].T, SUB_W, axis=1)
    qng = jnp.concatenate([q_norm_g[i], jnp.zeros((2 * LANES - Q_LORA,), F32)])[None]
    wq4 = w_uq[i].reshape(Q_LORA, MLA_HEADS, QK_HEAD)
    wq = jnp.concatenate([wq4, _rot(wq4[..., QK_NOPE:])], axis=-1).reshape(Q_LORA, MLA_HEADS * LANES)
    wq = jnp.concatenate([wq, jnp.zeros((2 * LANES - Q_LORA, MLA_HEADS * LANES), F32)], axis=0).astype(BF16)
    wkv4 = w_ukv[i].reshape(KV_LORA, MLA_HEADS, QK_NOPE + V_HEAD)
    wk = jnp.concatenate([wkv4[..., :QK_NOPE], jnp.zeros((KV_LORA, MLA_HEADS, LANES - QK_NOPE), F32)], axis=-1)
    wkv = jnp.concatenate([wk.reshape(KV_LORA, MLA_HEADS * LANES),
                           wkv4[..., QK_NOPE:].reshape(KV_LORA, MLA_HEADS * V_HEAD)], axis=1).astype(BF16)
    in_w = (norm1_g[i][None], win, gmlp_norm_g[i][None], wcat, bfull, qng, wq, kv_norm_g[i][None], wkv)
    pw = jnp.zeros((GROUP_W, GROUP_W), F32)
    for g in range(N_GROUPS):
        pw = pw.at[g * SUB_W:(g + 1) * SUB_W, g * SUB_W:(g + 1) * SUB_W].set(pool_w[i, g])
    merge_w = (pw.astype(BF16), pool_scale[i][None], w_out[i].reshape(N_GROUPS, GROUP_W, D_MODEL).astype(BF16),
               norm2_g[i][None])
    return in_w, merge_w, fourier_w[i].astype(BF16)


def _pad_mod(m):
    m = m.reshape(m.shape[:-1] + (N_ADA, D_MODEL))
    return jnp.concatenate([m, jnp.zeros(m.shape[:-2] + (SUBLANES - N_ADA, D_MODEL), F32)], axis=-2)


def _tiles(seq_len):
    return dict(tin=min(seq_len, 512), tmg=min(seq_len, 1024), tq=min(seq_len, 512), tr=min(seq_len, 1024))


def kernel(x, c, ctx, c_ctx, ada_w, ada_b, norm1_g, w_in, gmlp_norm_g, spatial_w, spatial_b, pool_w, pool_scale,
           q_norm_g, w_uq, kv_norm_g, w_ukv, qk_q_g, qk_k_g, fourier_w, w_out, norm2_g, ffn_w_gate, ffn_w_up,
           ffn_w_down, router_w, moe_w_gate, moe_w_up, moe_w_down):
    bsz, seq, d = x.shape
    lc = ctx.shape[1]
    depth = ada_w.shape[0]
    n_tok = bsz * seq
    tl, tcx = _tiles(seq), _tiles(lc)
    ffn_tm, ffn_tmc, ffn_tf = min(seq, 1024), min(bsz * lc, 1024), 1792
    moe_tm = 1024

    c_all = jnp.concatenate([c, c_ctx[None], jnp.zeros((SUBLANES - 1, d), F32)], axis=0)
    mods = _ada_call(c_all, ada_w, ada_b)
    cos, sin = _rope_angles(seq)
    ones, zeros = jnp.ones((lc, QK_ROPE), F32), jnp.zeros((lc, QK_ROPE), F32)
    dft_l = _dft_tables(seq, tl["tr"])
    dft_c = _dft_tables(lc, tcx["tr"])

    xc = ctx
    for i in range(depth):
        last = i == depth - 1
        moe = i % 2 == 1
        mod_l = _pad_mod(mods[i, :bsz])
        mod_c = _pad_mod(mods[i, bsz:bsz + 1])
        in_w, merge_w, wf = _layer_weights(i, w_in, gmlp_norm_g, spatial_w, spatial_b, pool_w, pool_scale,
                                           q_norm_g, w_uq, kv_norm_g, w_ukv, fourier_w, w_out, norm1_g, norm2_g)
        scale = QK_HEAD ** -0.5 * LOG2_E
        tabs_l = _rope_tables(qk_q_g[i], seq, cos, sin, scale) + _rope_tables(qk_k_g[i], seq, cos, sin, 1.0)
        tabs_c = _rope_tables(qk_q_g[i], lc, ones, zeros, scale) + _rope_tables(qk_k_g[i], lc, ones, zeros, 1.0)

        ya, zb, zd, q, k, v = _inproj_call(x, mod_l, False, in_w, tabs_l, tl["tin"])
        yac, zbc, zdc, qc, kc, vc = _inproj_call(xc, mod_c, True, in_w, tabs_c, tcx["tin"])
        ffn_w = ((moe_w_gate, moe_w_up, moe_w_down) if moe else (ffn_w_gate, ffn_w_up, ffn_w_down))
        ffn_w = [w[i // 2] for w in ffn_w]
        attn, *ffn_w16 = _attn_call(q, [kc, k], [vc, v], tl["tq"],
                                    cast=[w.reshape(-1, w.shape[-1]) for w in ffn_w])
        wg, wu, wd = [w16.reshape((-1,) + w.shape[-2:]) for w16, w in zip(ffn_w16, ffn_w)]
        yd = _fourier_call(zd, *dft_l, wf, tl["tr"])

        if not moe:
            j = i // 2
            xn, h2 = _merge_call(x, ya, zb, attn, yd, mod_l, False, merge_w, tl["tmg"], False)
            n_tiles = n_tok // ffn_tm
            x = _ffn_call(h2.reshape(n_tok, d), jnp.zeros((n_tiles,), jnp.int32), jnp.ones((n_tiles,), jnp.int32),
                          wg, wu, wd, ffn_tm, ffn_tf, False, False, res=xn.reshape(n_tok, d), mod=mod_l,
                          tiles_per_mod=seq // ffn_tm).reshape(bsz, seq, d)
            if not last:
                attn_c = _attn_call(qc, [kc], [vc], tcx["tq"])
                ydc = _fourier_call(zdc, *dft_c, wf, tcx["tr"])
                xnc, h2c = _merge_call(xc, yac, zbc, attn_c, ydc, mod_c, True, merge_w, tcx["tmg"], False)
                nc_tok = bsz * lc
                nct = nc_tok // ffn_tmc
                xc = _ffn_call(h2c.reshape(nc_tok, d), jnp.zeros((nct,), jnp.int32), jnp.ones((nct,), jnp.int32),
                               wg, wu, wd, ffn_tmc, ffn_tf, False, False, res=xnc.reshape(nc_tok, d), mod=mod_c,
                               tiles_per_mod=nct).reshape(bsz, lc, d)
        else:
            j = i // 2
            rw32 = jnp.concatenate([router_w[j], jnp.zeros((d, LANES - N_EXPERTS), F32)], axis=1)
            rw_hi = rw32.astype(BF16)
            rw = jnp.stack([rw_hi, (rw32 - rw_hi.astype(F32)).astype(BF16)])

            def moe_ffn(xs, ya_, zb_, attn_, yd_, mod_, shared, tms):
                b_, l_, _ = xs.shape
                n_ = b_ * l_
                xn, h_tok, logits = _merge_call(xs, ya_, zb_, attn_, yd_, mod_, shared, merge_w, tms, True, rw)
                route, counts = _route_call(logits.reshape(n_, LANES), min(n_, 512))
                eidx = route[:, 0:TOP_K].astype(jnp.int32)
                rank = route[:, 4:4 + TOP_K].astype(jnp.int32)
                cnt = counts[0, :N_EXPERTS].astype(jnp.int32)
                padded = (cnt + moe_tm - 1) // moe_tm * moe_tm
                ends = jnp.cumsum(padded)
                starts = ends - padded
                onehot = eidx[..., None] == jnp.arange(N_EXPERTS, dtype=jnp.int32)
                pos = jnp.sum(jnp.where(onehot, starts, 0), axis=-1) + rank
                n_tiles = TOP_K * n_ // moe_tm + N_EXPERTS
                tile_row = jnp.arange(n_tiles, dtype=jnp.int32) * moe_tm
                tile_e = jnp.minimum(jnp.sum(tile_row[:, None] >= ends[None, :], axis=1), N_EXPERTS - 1)
                tile_v = (tile_row < ends[-1]).astype(jnp.int32)
                tail = jnp.stack([ends[-1], (n_tiles * moe_tm - ends[-1]) // (moe_tm // 2)])
                pad_info = jnp.concatenate([starts + cnt, padded - cnt, tail]).astype(jnp.int32)
                xs_tok = _dispatch_call(pos, pad_info, h_tok, n_tiles * moe_tm, min(n_, 512), moe_tm)
                y_tok = _ffn_call(xs_tok, tile_e.astype(jnp.int32), tile_v, wg, wu, wd, moe_tm, ffn_tf, True, True)
                mod_full = jnp.broadcast_to(mod_, (b_, SUBLANES, d)) if shared else mod_
                out = _combine_call(pos, y_tok, xn.reshape(n_, d), route, mod_full, l_, min(l_, 256))
                return out.reshape(b_, l_, d)

            x_new = moe_ffn(x, ya, zb, attn, yd, mod_l, False, tl["tmg"])
            if not last:
                attn_c = _attn_call(qc, [kc], [vc], tcx["tq"])
                ydc = _fourier_call(zdc, *dft_c, wf, tcx["tr"])
                xc = moe_ffn(xc, yac, zbc, attn_c, ydc, mod_c, True, tcx["tmg"])
            x = x_new
    return x
```

```python
import functools

import jax
import jax.numpy as jnp
from jax import lax
from jax.experimental import pallas as pl
from jax.experimental.pallas import tpu as pltpu

F32 = jnp.float32
BF16 = jnp.bfloat16

D_MODEL = 1024
GROUP_W = 256
SUB_W = 64
N_GROUPS = 4
CHUNK = 128
POOL_WINDOWS = (2, 4, 8, 16)
POOL_HALO = 16
MLA_HEADS = 4
QK_NOPE = 64
QK_ROPE = 32
V_HEAD = 64
QK_HEAD = QK_NOPE + QK_ROPE
Q_LORA = 192
KV_LORA = 128
GRID_W = 64
ROPE_BASE = 10000.0
D_FF = 3584
N_EXPERTS = 8
TOP_K = 2
N_ADA = 6
EPS = 1e-6
LOG2_E = 1.4426950408889634

OFF_A, OFF_B, OFF_CQ, OFF_CKV, OFF_CKR, OFF_D, IN_W = 0, 512, 768, 960, 1088, 1120, 1376
P_A, P_B, P_D, P_CKV, P_CQ, P_W = 0, 512, 768, 1024, 1152, 1408

LANES = 128
SUBLANES = 8
TOK_ROWS = D_MODEL // LANES
VMEM_LIMIT = 56 * 1024 * 1024


def _cparams(*sem):
    return pltpu.CompilerParams(dimension_semantics=sem, vmem_limit_bytes=VMEM_LIMIT)


def _rms(x):
    return lax.rsqrt(jnp.mean(x * x, axis=-1, keepdims=True) + EPS)


def _ada_kernel(c_ref, w_ref, b_ref, o_ref):
    ca = jax.nn.silu(c_ref[...])
    o_ref[...] = jnp.dot(ca, w_ref[...], preferred_element_type=F32, precision=lax.Precision.HIGHEST) + b_ref[...]


def _ada_call(c_all, ada_w, ada_b):
    depth, d, n = ada_w.shape
    rows = c_all.shape[0]
    tn = 1024
    return pl.pallas_call(
        _ada_kernel,
        grid=(depth, n // tn),
        in_specs=[
            pl.BlockSpec((rows, d), lambda i, j: (0, 0)),
            pl.BlockSpec((None, d, tn), lambda i, j: (i, 0, j)),
            pl.BlockSpec((None, 1, tn), lambda i, j: (i, 0, j)),
        ],
        out_specs=pl.BlockSpec((None, rows, tn), lambda i, j: (i, 0, j)),
        out_shape=jax.ShapeDtypeStruct((depth, rows, n), F32),
        compiler_params=_cparams("arbitrary", "arbitrary"),
        name="ada_mod",
    )(c_all, ada_w, ada_b.reshape(depth, 1, n))


INPROJ_SUB = 512


def _inproj_kernel(x_ref, mod_ref, n1g_ref, win_ref, gg_ref, wcat_ref, bfull_ref, qng_ref, wq_ref, kvng_ref,
                   wkv_ref, caq_ref, cbq_ref, cak_ref, cbk_ref,
                   ya_ref, zb_ref, zd_ref, q_ref, k_ref, v_ref, *, sub):
    tm = x_ref.shape[0]
    gain1 = n1g_ref[...] * (1.0 + mod_ref[1:2, :])
    sh1 = mod_ref[0:1, :]

    def project(rows):
        xt = x_ref[rows, :]
        h = xt * _rms(xt) * gain1 + sh1
        return jnp.dot(h.astype(BF16), win_ref[...], preferred_element_type=F32)

    grp = lax.broadcasted_iota(jnp.int32, (CHUNK, GROUP_W), 1) // SUB_W
    lane2 = lax.broadcasted_iota(jnp.int32, (sub, 2 * LANES), 1)
    lane = lax.broadcasted_iota(jnp.int32, (sub, LANES), 1)
    head_rows = lax.broadcasted_iota(jnp.int32, (LANES, LANES), 0) < QK_HEAD
    head_ones = jnp.where(head_rows, 1.0, 0.0).astype(BF16)

    def head_rms(blk):
        ssq = jnp.dot((blk * blk).astype(BF16), head_ones, preferred_element_type=F32)
        return lax.rsqrt(ssq / QK_HEAD + EPS)

    def mixers(p, r0):
        rows = slice(r0, r0 + sub)
        uv = jax.nn.gelu(p[:, P_A:P_A + 2 * GROUP_W])
        u = uv[:, :GROUP_W]
        vv = uv[:, GROUP_W:]
        vn = vv * _rms(vv) * gg_ref[...]
        for c in range(sub // CHUNK):
            crow = slice(c * CHUNK, (c + 1) * CHUNK)
            vc = vn[crow]
            vstack = jnp.concatenate([jnp.where(grp == g, vc, 0.0).astype(BF16) for g in range(N_GROUPS)], axis=0)
            mixed = jnp.dot(wcat_ref[...], vstack, preferred_element_type=F32) + bfull_ref[...]
            ya_ref[r0 + c * CHUNK:r0 + (c + 1) * CHUNK, :] = (u[crow] * mixed).astype(BF16)

        zb_ref[rows, :] = p[:, P_B:P_B + GROUP_W].astype(BF16)
        zd_ref[rows, :] = p[:, P_D:P_D + GROUP_W].astype(BF16)

        cq = p[:, P_CQ:P_CQ + 2 * LANES]
        cq_ss = jnp.sum(jnp.where(lane2 < Q_LORA, cq * cq, 0.0), axis=-1, keepdims=True)
        xq = cq * lax.rsqrt(cq_ss / Q_LORA + EPS) * qng_ref[...]
        qraw = jnp.dot(xq.astype(BF16), wq_ref[...], preferred_element_type=F32)
        ckv = p[:, P_CKV:P_CKV + KV_LORA]
        xkv = ckv * _rms(ckv) * kvng_ref[...]
        kvraw = jnp.dot(xkv.astype(BF16), wkv_ref[...], preferred_element_type=F32)
        krope = p[:, P_CQ + LANES:P_CQ + 2 * LANES]
        caq, cbq, cak, cbk = caq_ref[rows, :], cbq_ref[rows, :], cak_ref[rows, :], cbk_ref[rows, :]
        for hd in range(MLA_HEADS):
            cols = slice(hd * LANES, (hd + 1) * LANES)
            qb = qraw[:, cols]
            rq = head_rms(qb)
            q_ref[rows, cols] = ((qb * caq + pltpu.roll(qb, LANES - QK_ROPE, 1) * cbq) * rq).astype(BF16)
            kb = jnp.where(lane < QK_NOPE, kvraw[:, cols], krope)
            rk = head_rms(kb)
            k_ref[rows, cols] = ((kb * cak + pltpu.roll(kb, LANES - QK_ROPE, 1) * cbk) * rk).astype(BF16)
        v_ref[rows, :] = kvraw[:, MLA_HEADS * LANES:].astype(BF16)

    starts = range(0, tm, sub)
    ps = [project(slice(r0, r0 + sub)) for r0 in starts]
    for p, r0 in zip(ps, starts):
        mixers(p, r0)


def _inproj_call(x, mod, shared_mod, wts, tabs, tm):
    bs, ls, d = x.shape
    full = lambda a: pl.BlockSpec(a.shape, lambda b, j: (0,) * a.ndim)
    mod_map = (lambda b, j: (0, 0, 0)) if shared_mod else (lambda b, j: (b, 0, 0))
    tab_spec = pl.BlockSpec((tm, LANES), lambda b, j: (j, 0))
    tok = lambda w: pl.BlockSpec((None, tm, w), lambda b, j: (b, j, 0))
    widths = (GROUP_W, GROUP_W, GROUP_W, MLA_HEADS * LANES, MLA_HEADS * LANES, MLA_HEADS * V_HEAD)
    return pl.pallas_call(
        functools.partial(_inproj_kernel, sub=min(tm, INPROJ_SUB)),
        grid=(bs, ls // tm),
        in_specs=[tok(d), pl.BlockSpec((None, SUBLANES, d), mod_map)] + [full(w) for w in wts] + [tab_spec] * 4,
        out_specs=[tok(w) for w in widths],
        out_shape=[jax.ShapeDtypeStruct((bs, ls, w), BF16) for w in widths],
        compiler_params=_cparams("parallel", "parallel"),
        name="inproj",
    )(x, mod, *wts, *tabs)


SCORES_AHEAD = 1
KEY_CHUNK = 256


def _attn_kernel(*refs, n_kv, n_cast):
    q_ref = refs[0]
    k_refs = refs[1:1 + n_kv]
    v_refs = refs[1 + n_kv:1 + 2 * n_kv]
    cast_in = refs[1 + 2 * n_kv:1 + 2 * n_kv + n_cast]
    o_ref = refs[1 + 2 * n_kv + n_cast]
    cast_out = refs[2 + 2 * n_kv + n_cast:]
    for src, dst in zip(cast_in, cast_out):
        dst[...] = src[...].astype(BF16)
    tq = q_ref.shape[0]
    head_of_lane = lax.broadcasted_iota(jnp.int32, (tq, MLA_HEADS * V_HEAD), 1) // V_HEAD
    out = jnp.zeros((tq, MLA_HEADS * V_HEAD), F32)

    def scores(hd):
        cols = slice(hd * LANES, (hd + 1) * LANES)
        qh = q_ref[:, cols]
        return [lax.dot_general(qh, kr[:, cols], (((1,), (1,)), ((), ())), preferred_element_type=F32)
                for kr in k_refs]

    ahead = [scores(hd) for hd in range(min(SCORES_AHEAD, MLA_HEADS))]
    for hd in range(MLA_HEADS):
        ss = ahead.pop(0)
        if hd + SCORES_AHEAD < MLA_HEADS:
            ahead.append(scores(hd + SCORES_AHEAD))
        m = functools.reduce(jnp.maximum, [jnp.max(s, axis=-1, keepdims=True) for s in ss])
        den = jnp.zeros((tq, 1), F32)
        o = jnp.zeros((tq, MLA_HEADS * V_HEAD), F32)
        for s, vr in zip(ss, v_refs):
            for k0 in range(0, s.shape[1], KEY_CHUNK):
                keys = slice(k0, min(k0 + KEY_CHUNK, s.shape[1]))
                pp = jnp.exp2(s[:, keys] - m)
                den = den + jnp.sum(pp, axis=-1, keepdims=True)
                o = o + jnp.dot(pp.astype(BF16), vr[keys, :], preferred_element_type=F32)
        out = jnp.where(head_of_lane == hd, o / den, out)
    o_ref[...] = out.astype(BF16)


BF16_ROWS = 16


def _cast_blocks(rows, steps):
    n = steps
    while rows % n or (rows // n) % BF16_ROWS:
        n //= 2
    return n


def _attn_call(q, ks, vs, tq, cast=()):
    bs, lq, _ = q.shape
    n_kv = len(ks)
    nj = lq // tq
    steps = bs * nj
    kv_spec = lambda a: pl.BlockSpec((None,) + a.shape[1:], lambda b, j: (b, 0, 0))

    def cast_spec(a):
        n = _cast_blocks(a.shape[0], steps)
        return pl.BlockSpec((a.shape[0] // n, a.shape[1]), lambda b, j: ((b * nj + j) * n // steps, 0))

    out = pl.pallas_call(
        functools.partial(_attn_kernel, n_kv=n_kv, n_cast=len(cast)),
        grid=(bs, nj),
        in_specs=[pl.BlockSpec((None, tq, q.shape[2]), lambda b, j: (b, j, 0))]
        + [kv_spec(a) for a in ks] + [kv_spec(a) for a in vs] + [cast_spec(a) for a in cast],
        out_specs=[pl.BlockSpec((None, tq, MLA_HEADS * V_HEAD), lambda b, j: (b, j, 0))]
        + [cast_spec(a) for a in cast],
        out_shape=[jax.ShapeDtypeStruct((bs, lq, MLA_HEADS * V_HEAD), BF16)]
        + [jax.ShapeDtypeStruct(a.shape, BF16) for a in cast],
        compiler_params=_cparams("arbitrary", "arbitrary"),
        name="attention",
    )(q, *ks, *vs, *cast)
    return out if cast else out[0]


FOURIER_SUB = 512


def _fourier_kernel(cs_ref, z_ref, cc_ref, scn_ref, wf_ref, o_ref):
    tr = o_ref.shape[0]
    sub = min(tr, FOURIER_SUB)
    starts = range(0, tr, sub)
    tcs = [(jnp.dot(cs_ref[r0:r0 + sub, :], z_ref[...], preferred_element_type=F32),
            jnp.dot(cs_ref[tr + r0:tr + r0 + sub, :], z_ref[...], preferred_element_type=F32))
           for r0 in starts]
    for (tc, ts), r0 in zip(tcs, starts):
        y = (jnp.dot(tc.astype(BF16), cc_ref[...], preferred_element_type=F32)
             + jnp.dot(ts.astype(BF16), scn_ref[...], preferred_element_type=F32))
        o_ref[r0:r0 + sub, :] = jnp.dot(y.astype(BF16), wf_ref[...], preferred_element_type=F32).astype(BF16)


def _fourier_call(zd, cs, cc, scn, wf, tr):
    bs, ls, w = zd.shape
    full = lambda a: pl.BlockSpec(a.shape, lambda r, b: (0,) * a.ndim)
    return pl.pallas_call(
        _fourier_kernel,
        grid=(ls // tr, bs),
        in_specs=[pl.BlockSpec((None, 2 * tr, ls), lambda r, b: (r, 0, 0)),
                  pl.BlockSpec((None, ls, w), lambda r, b: (b, 0, 0)), full(cc), full(scn), full(wf)],
        out_specs=pl.BlockSpec((None, tr, w), lambda r, b: (b, r, 0)),
        out_shape=jax.ShapeDtypeStruct((bs, ls, w), BF16),
        compiler_params=_cparams("parallel", "arbitrary"),
        name="fourier",
    )(cs, zd, cc, scn, wf)


MERGE_SUB = 512


def _merge_kernel(*refs, seq_len, tm, sub, tok_layout, router):
    (x_ref, ya_ref, zb_ref, zbp_ref, zbn_ref, at_ref, yd_ref, mod_ref, pw_ref, ps_ref, wo_ref, n2g_ref) = refs[:12]
    rest = refs[12:]
    if router:
        rw_ref, rest = rest[0], rest[1:]
    xn_ref, h_ref = rest[0], rest[1]
    j = pl.program_id(1)
    nj = pl.num_programs(1)

    starts = range(0, tm, sub)
    parts = []
    for r0 in starts:
        rows = slice(r0, r0 + sub)
        part = jnp.dot(ya_ref[rows, :], wo_ref[0], preferred_element_type=F32)
        part += jnp.dot(at_ref[rows, :], wo_ref[2], preferred_element_type=F32)
        part += jnp.dot(yd_ref[rows, :], wo_ref[3], preferred_element_type=F32)
        parts.append(part)

    zm = zb_ref[...].astype(F32)
    zp = jnp.where(j > 0, zbp_ref[...].astype(F32), 0.0)
    zn = jnp.where(j < nj - 1, zbn_ref[...].astype(F32), 0.0)
    ext = jnp.concatenate([zp, zm, zn], axis=0)
    n = tm + 2 * POOL_HALO

    def ahead(a, k):
        return pltpu.roll(a, n - k, 0)

    tg = j * tm + lax.broadcasted_iota(jnp.int32, (tm, 1), 0)
    low_group = lax.broadcasted_iota(jnp.int32, (tm, LANES), 1) < SUB_W

    def window_mean(d, w):
        cnt = jnp.minimum(tg - w // 2 + w, seq_len) - jnp.maximum(tg - w // 2, 0)
        return ahead(d, POOL_HALO - w // 2)[:tm] * (1.0 / cnt.astype(F32))

    halves = []
    for half in range(GROUP_W // LANES):
        w_lo, w_hi = POOL_WINDOWS[2 * half], POOL_WINDOWS[2 * half + 1]
        d, width, means = ext[:, half * LANES:(half + 1) * LANES], 1, {}
        while width < w_hi:
            d = d + ahead(d, width)
            width *= 2
            if width in (w_lo, w_hi):
                means[width] = window_mean(d, width)
        halves.append(jnp.where(low_group, means[w_lo], means[w_hi]))
    diff = jnp.concatenate(halves, axis=1) - zm
    yb = jnp.dot(diff.astype(BF16), pw_ref[...], preferred_element_type=F32) * ps_ref[...]

    yb = yb.astype(BF16)
    g1 = mod_ref[2:3, :]
    sh2 = mod_ref[3:4, :]
    gain2 = n2g_ref[...] * (1.0 + mod_ref[4:5, :])

    def project(part, rows):
        acc = part + jnp.dot(yb[rows], wo_ref[1], preferred_element_type=F32)
        return x_ref[rows, :] + g1 * acc

    def modulate(xn, r0):
        rows = slice(r0, r0 + sub)
        xn_ref[rows, :] = xn
        h2 = xn * _rms(xn) * gain2 + sh2
        if tok_layout:
            for jj in range(TOK_ROWS):
                h_ref[pl.ds(r0 * TOK_ROWS + jj, sub, stride=TOK_ROWS), :] = h2[:, jj * LANES:(jj + 1) * LANES]
        else:
            h_ref[rows, :] = h2.astype(BF16)
        if router:
            lg_ref = rest[2]
            h_hi = h2.astype(BF16)
            h_lo = (h2 - h_hi.astype(F32)).astype(BF16)
            lg_ref[rows, :] = (jnp.dot(h_hi, rw_ref[0], preferred_element_type=F32)
                               + (jnp.dot(h_lo, rw_ref[0], preferred_element_type=F32)
                                  + jnp.dot(h_hi, rw_ref[1], preferred_element_type=F32)))

    xns = [project(part, slice(r0, r0 + sub)) for part, r0 in zip(parts, starts)]
    for xn, r0 in zip(xns, starts):
        modulate(xn, r0)


def _merge_call(x, ya, zb, attn, yd, mod, shared_mod, wts, tm, tok_layout, router_w=None):
    bs, ls, d = x.shape
    nj = ls // tm
    hb = tm // POOL_HALO
    full = lambda a: pl.BlockSpec(a.shape, lambda b, j: (0,) * a.ndim)
    mod_map = (lambda b, j: (0, 0, 0)) if shared_mod else (lambda b, j: (b, 0, 0))
    tok = lambda w: pl.BlockSpec((None, tm, w), lambda b, j: (b, j, 0))
    in_specs = [
        tok(d), tok(GROUP_W), tok(GROUP_W),
        pl.BlockSpec((None, POOL_HALO, GROUP_W), lambda b, j: (b, jnp.maximum(j * hb - 1, 0), 0)),
        pl.BlockSpec((None, POOL_HALO, GROUP_W), lambda b, j: (b, jnp.minimum((j + 1) * hb, nj * hb - 1), 0)),
        tok(GROUP_W), tok(GROUP_W),
        pl.BlockSpec((None, SUBLANES, d), mod_map),
    ] + [full(w) for w in wts]
    args = [x, ya, zb, zb, zb, attn, yd, mod, *wts]
    out_specs = [tok(d)]
    out_shape = [jax.ShapeDtypeStruct((bs, ls, d), F32)]
    if tok_layout:
        out_specs.append(pl.BlockSpec((tm * TOK_ROWS, LANES), lambda b, j: (b * nj + j, 0)))
        out_shape.append(jax.ShapeDtypeStruct((bs * ls * TOK_ROWS, LANES), F32))
    else:
        out_specs.append(tok(d))
        out_shape.append(jax.ShapeDtypeStruct((bs, ls, d), BF16))
    if router_w is not None:
        in_specs.append(full(router_w))
        args.append(router_w)
        out_specs.append(tok(LANES))
        out_shape.append(jax.ShapeDtypeStruct((bs, ls, LANES), F32))
    return pl.pallas_call(
        functools.partial(_merge_kernel, seq_len=ls, tm=tm, sub=min(tm, MERGE_SUB), tok_layout=tok_layout,
                          router=router_w is not None),
        grid=(bs, nj),
        in_specs=in_specs,
        out_specs=out_specs,
        out_shape=out_shape,
        compiler_params=_cparams("parallel", "parallel"),
        name="merge",
    )(*args)


ACC_COLS = 256
FFN_SUB = 512


def _ffn_kernel(te_ref, tv_ref, *refs, tm, tok_in, tok_out, residual, row_options):
    x_ref, wg_ref, wu_ref, wd_ref = refs[:4]
    rest = refs[4:]
    if residual:
        res_ref, mod_ref, rest = rest[0], rest[1], rest[2:]
    o_ref, acc_ref = rest[0], rest[1]
    i = pl.program_id(0)
    f = pl.program_id(1)
    nf = pl.num_programs(1)

    def store(y, r0, rows):
        if tok_out:
            for jj in range(TOK_ROWS):
                o_ref[pl.ds(r0 * TOK_ROWS + jj, rows, stride=TOK_ROWS), :] = y[:, jj * LANES:(jj + 1) * LANES]
        else:
            o_ref[r0:r0 + rows, :] = y

    def compute(m):
        @pl.when(f == 0)
        def _():
            acc_ref[:m, :] = jnp.zeros((m, D_MODEL), F32)
            if tok_in:
                for jj in range(TOK_ROWS):
                    rest[2][:m, jj * LANES:(jj + 1) * LANES] = x_ref[pl.ds(jj, m, stride=TOK_ROWS), :].astype(BF16)

        xb = rest[2][:m, :] if tok_in else x_ref[:m, :]
        tf = wg_ref.shape[1]
        acts = []
        for lo in range(0, tf, FFN_SUB):
            sub = slice(lo, min(lo + FFN_SUB, tf))
            gate = jnp.dot(xb, wg_ref[:, sub], preferred_element_type=F32)
            up = jnp.dot(xb, wu_ref[:, sub], preferred_element_type=F32)
            acts.append((jax.nn.silu(gate) * up).astype(BF16))
        act = jnp.concatenate(acts, axis=1)
        for cb in range(D_MODEL // ACC_COLS):
            cols = slice(cb * ACC_COLS, (cb + 1) * ACC_COLS)
            acc_ref[:m, cols] += jnp.dot(act, wd_ref[:, cols], preferred_element_type=F32)

        @pl.when(f == nf - 1)
        def _():
            y = acc_ref[:m, :]
            if residual:
                y = res_ref[:m, :] + mod_ref[5:6, :] * y
            store(y, 0, m)
            if m < tm:
                store(jnp.zeros((tm - m, D_MODEL), F32), m, tm - m)

    for m in row_options:
        pl.when(tv_ref[i] == m)(functools.partial(compute, m))

    @pl.when(jnp.logical_and(tv_ref[i] == 0, f == nf - 1))
    def _():
        store(jnp.zeros((tm, D_MODEL), F32), 0, tm)


def _ffn_call(x, tile_e, tile_v, wg, wu, wd, tm, tf, tok_in, tok_out, res=None, mod=None, tiles_per_mod=None,
              row_options=None):
    n_tiles = tile_e.shape[0]
    d, dff = wg.shape[1], wg.shape[2]
    nf = dff // tf
    last = nf - 1
    fsel = lambda i, f, te, tv: jnp.where(tv[i] > 0, f, last)
    xsel = lambda i, f, te, tv: (jnp.where(tv[i] > 0, i, 0), 0)
    x_spec = pl.BlockSpec((tm * TOK_ROWS, LANES), xsel) if tok_in else pl.BlockSpec((tm, d), xsel)
    in_specs = [
        x_spec,
        pl.BlockSpec((None, d, tf), lambda i, f, te, tv: (te[i], 0, fsel(i, f, te, tv))),
        pl.BlockSpec((None, d, tf), lambda i, f, te, tv: (te[i], 0, fsel(i, f, te, tv))),
        pl.BlockSpec((None, tf, d), lambda i, f, te, tv: (te[i], fsel(i, f, te, tv), 0)),
    ]
    args = [x, wg, wu, wd]
    residual = res is not None
    if residual:
        in_specs.append(pl.BlockSpec((tm, d), lambda i, f, te, tv: (i, 0)))
        in_specs.append(pl.BlockSpec((None, SUBLANES, d), lambda i, f, te, tv: (i // tiles_per_mod, 0, 0)))
        args += [res, mod]
    if tok_out:
        out_spec = pl.BlockSpec((tm * TOK_ROWS, LANES), lambda i, f, te, tv: (i, 0))
        out_shape = jax.ShapeDtypeStruct((n_tiles * tm * TOK_ROWS, LANES), F32)
    else:
        out_spec = pl.BlockSpec((tm, d), lambda i, f, te, tv: (i, 0))
        out_shape = jax.ShapeDtypeStruct((n_tiles * tm, d), F32)
    scratch = [pltpu.VMEM((tm, d), F32)]
    if tok_in:
        scratch.append(pltpu.VMEM((tm, d), BF16))
    return pl.pallas_call(
        functools.partial(_ffn_kernel, tm=tm, tok_in=tok_in, tok_out=tok_out, residual=residual,
                          row_options=row_options or (tm,)),
        grid_spec=pltpu.PrefetchScalarGridSpec(
            num_scalar_prefetch=2, grid=(n_tiles, nf), in_specs=in_specs, out_specs=out_spec,
            scratch_shapes=scratch),
        out_shape=out_shape,
        compiler_params=_cparams("parallel", "arbitrary"),
        name="ffn",
    )(tile_e, tile_v, *args)


def _route_kernel(lg_ref, o_ref, cnt_ref, carry_ref):
    tm = lg_ref.shape[0]
    i = pl.program_id(0)

    @pl.when(i == 0)
    def _():
        carry_ref[...] = jnp.zeros_like(carry_ref)

    lane_i = lax.broadcasted_iota(jnp.int32, (tm, LANES), 1)
    lane = lane_i.astype(F32)
    neg = jnp.float32(-jnp.inf)
    lg = jnp.where(lane_i < N_EXPERTS, lg_ref[...], neg)
    m1 = jnp.max(lg, axis=-1, keepdims=True)
    i1 = jnp.min(jnp.where(lg == m1, lane, float(LANES)), axis=-1, keepdims=True)
    lg2 = jnp.where(lane == i1, neg, lg)
    m2 = jnp.max(lg2, axis=-1, keepdims=True)
    i2 = jnp.min(jnp.where(lg2 == m2, lane, float(LANES)), axis=-1, keepdims=True)
    e2 = jnp.exp(m2 - m1)
    w1 = 1.0 / (1.0 + e2)
    w2 = e2 / (1.0 + e2)
    hit = jnp.logical_or(lane == i1, lane == i2)
    onehot = jnp.where(hit, 1.0, 0.0).astype(BF16)
    r = lax.broadcasted_iota(jnp.int32, (tm, tm), 0)
    c = lax.broadcasted_iota(jnp.int32, (tm, tm), 1)
    before = jnp.where(c < r, 1.0, 0.0).astype(BF16)
    carry = carry_ref[0:1, :]
    cum = jnp.dot(before, onehot, preferred_element_type=F32) + carry
    rank1 = jnp.sum(jnp.where(lane == i1, cum, 0.0), axis=-1, keepdims=True)
    rank2 = jnp.sum(jnp.where(lane == i2, cum, 0.0), axis=-1, keepdims=True)
    total = carry + jnp.sum(onehot.astype(F32), axis=0, keepdims=True)
    carry_ref[...] = jnp.broadcast_to(total, carry_ref.shape)
    cnt_ref[...] = jnp.broadcast_to(total, cnt_ref.shape)
    out = jnp.zeros((tm, LANES), F32)
    for col, val in enumerate((i1, i2, w1, w2, rank1, rank2)):
        out = jnp.where(lane_i == col, val, out)
    o_ref[...] = out


def _route_call(logits, tm):
    n = logits.shape[0]
    return pl.pallas_call(
        _route_kernel,
        grid=(n // tm,),
        in_specs=[pl.BlockSpec((tm, LANES), lambda i: (i, 0))],
        out_specs=[pl.BlockSpec((tm, LANES), lambda i: (i, 0)), pl.BlockSpec((SUBLANES, LANES), lambda i: (0, 0))],
        out_shape=[jax.ShapeDtypeStruct((n, LANES), F32), jax.ShapeDtypeStruct((SUBLANES, LANES), F32)],
        scratch_shapes=[pltpu.VMEM((SUBLANES, LANES), F32)],
        compiler_params=_cparams("arbitrary"),
        name="route",
    )(logits)


ISSUE_UNROLL = 8


def _token_rows(tok):
    start = tok * TOK_ROWS
    return pl.ds(start if isinstance(start, int) else pl.multiple_of(start, TOK_ROWS), TOK_ROWS)


def _token_copy(src, src_tok, dst, dst_tok, sem):
    return pltpu.make_async_copy(src.at[_token_rows(src_tok), :], dst.at[_token_rows(dst_tok), :], sem)


def _for_each_token(n_tok, body):
    def group(g, carry):
        for u in range(ISSUE_UNROLL):
            body(g * ISSUE_UNROLL + u)
        return carry

    lax.fori_loop(0, n_tok // ISSUE_UNROLL, group, 0)


def _dispatch_kernel(pad_ref, pos_ref, h_ref, xs_hbm, zbuf, sem, zsem, *, td, pad_bits, max_tail):
    i = pl.program_id(0)

    def pad_copies():
        for e in range(N_EXPERTS):
            first, length = pad_ref[e], pad_ref[N_EXPERTS + e]
            for b in range(pad_bits):
                size = 1 << b
                tok = first + jnp.bitwise_and(length, size - 1)
                copy = pltpu.make_async_copy(
                    zbuf.at[pl.ds(0, size * TOK_ROWS), :],
                    xs_hbm.at[pl.ds(pl.multiple_of(tok * TOK_ROWS, TOK_ROWS), size * TOK_ROWS), :], zsem)
                yield jnp.bitwise_and(length, size) != 0, copy
        first, pieces = pad_ref[2 * N_EXPERTS], pad_ref[2 * N_EXPERTS + 1]
        piece = zbuf.shape[0]
        for k in range(max_tail):
            copy = pltpu.make_async_copy(
                zbuf, xs_hbm.at[pl.ds(pl.multiple_of(first * TOK_ROWS + k * piece, TOK_ROWS), piece), :], zsem)
            yield k < pieces, copy

    @pl.when(i == 0)
    def _():
        zbuf[...] = jnp.zeros_like(zbuf)
        for needed, copy in pad_copies():
            pl.when(needed)(copy.start)

    def issue(t):
        for s in range(TOP_K):
            _token_copy(h_ref, t, xs_hbm, pos_ref[0, TOP_K * t + s], sem).start(priority=s % 2)

    _for_each_token(td, issue)
    for _ in range(TOP_K):
        pltpu.make_async_copy(h_ref, xs_hbm.at[pl.ds(0, td * TOK_ROWS), :], sem).wait()

    @pl.when(i == 0)
    def _():
        for needed, copy in pad_copies():
            pl.when(needed)(copy.wait)


def _dispatch_call(pos, pad_info, h_tok, n_slots, td, tile):
    n = pos.shape[0]
    pad_bits = tile.bit_length() - 1
    return pl.pallas_call(
        functools.partial(_dispatch_kernel, td=td, pad_bits=pad_bits, max_tail=2 * N_EXPERTS),
        grid_spec=pltpu.PrefetchScalarGridSpec(
            num_scalar_prefetch=1, grid=(n // td,),
            in_specs=[pl.BlockSpec((None, 1, TOP_K * td), lambda i, pad: (i, 0, 0), memory_space=pltpu.SMEM),
                      pl.BlockSpec((td * TOK_ROWS, LANES), lambda i, pad: (i, 0))],
            out_specs=pl.BlockSpec(memory_space=pl.ANY),
            scratch_shapes=[pltpu.VMEM((tile // 2 * TOK_ROWS, LANES), F32), pltpu.SemaphoreType.DMA,
                            pltpu.SemaphoreType.DMA]),
        out_shape=jax.ShapeDtypeStruct((n_slots * TOK_ROWS, LANES), F32),
        compiler_params=_cparams("arbitrary"),
        name="dispatch",
    )(pad_info, pos.reshape(n // td, 1, TOP_K * td), h_tok)


COMBINE_AHEAD = 2


def _combine_kernel(*refs, tc):
    pos_refs = refs[:COMBINE_AHEAD + 1]
    y_hbm, xn_ref, route_ref, mod_ref, o_ref = refs[COMBINE_AHEAD + 1:COMBINE_AHEAD + 6]
    bufs = refs[COMBINE_AHEAD + 6:-1]
    sems = refs[-1]
    n_buf = len(bufs)
    i = pl.program_id(0)
    n = pl.num_programs(0)

    def issue(p_ref, slot, t):
        for s in range(TOP_K):
            _token_copy(y_hbm, p_ref[0, TOP_K * t + s], bufs[slot], TOP_K * t + s,
                        sems.at[slot]).start(priority=s % 2)

    def blend(slot):
        buf = bufs[slot]
        g2 = mod_ref[5:6, :]
        w1 = route_ref[:, 2:3]
        w2 = route_ref[:, 3:4]
        for jj in range(TOK_ROWS):
            cols = slice(jj * LANES, (jj + 1) * LANES)
            ya = buf[pl.ds(jj, tc, stride=TOP_K * TOK_ROWS), :]
            yb = buf[pl.ds(TOK_ROWS + jj, tc, stride=TOP_K * TOK_ROWS), :]
            o_ref[:, cols] = xn_ref[:, cols] + g2[:, cols] * (w1 * ya + w2 * yb)

    @pl.when(i == 0)
    def _():
        for ahead in range(COMBINE_AHEAD):
            @pl.when(ahead < n)
            def _(ahead=ahead):
                _for_each_token(tc, functools.partial(issue, pos_refs[ahead], ahead))

    for slot in range(n_buf):
        @pl.when(i % n_buf == slot)
        def _(slot=slot):
            pltpu.make_async_copy(y_hbm.at[pl.ds(0, TOP_K * tc * TOK_ROWS), :], bufs[slot], sems.at[slot]).wait()

            @pl.when(i + COMBINE_AHEAD < n)
            def _():
                for t in range(tc):
                    issue(pos_refs[COMBINE_AHEAD], (slot + COMBINE_AHEAD) % n_buf, t)
                blend(slot)

            @pl.when(i + COMBINE_AHEAD >= n)
            def _():
                blend(slot)


def _combine_call(pos, y_tok, xn, route, mod, seq_len, tc):
    n, d = xn.shape
    steps = n // tc
    pos3 = pos.reshape(steps, 1, TOP_K * tc)
    pos_spec = lambda k: pl.BlockSpec((None, 1, TOP_K * tc), lambda i: (jnp.minimum(i + k, steps - 1), 0, 0),
                                      memory_space=pltpu.SMEM)
    n_buf = COMBINE_AHEAD + 1
    return pl.pallas_call(
        functools.partial(_combine_kernel, tc=tc),
        grid=(steps,),
        in_specs=[pos_spec(k) for k in range(n_buf)]
        + [pl.BlockSpec(memory_space=pl.ANY),
           pl.BlockSpec((tc, d), lambda i: (i, 0)),
           pl.BlockSpec((tc, LANES), lambda i: (i, 0)),
           pl.BlockSpec((None, SUBLANES, d), lambda i: (i * tc // seq_len, 0, 0))],
        out_specs=pl.BlockSpec((tc, d), lambda i: (i, 0)),
        out_shape=jax.ShapeDtypeStruct((n, d), F32),
        scratch_shapes=[pltpu.VMEM((TOP_K * tc * TOK_ROWS, LANES), F32)] * n_buf
        + [pltpu.SemaphoreType.DMA((n_buf,))],
        compiler_params=_cparams("arbitrary"),
        name="combine",
    )(*([pos3] * n_buf), y_tok, xn, route, mod)


def _rot(w):
    return jnp.concatenate([-w[..., 8:16], w[..., 0:8], -w[..., 24:32], w[..., 16:24]], axis=-1)


def _swap(g):
    return jnp.concatenate([g[..., 8:16], g[..., 0:8], g[..., 24:32], g[..., 16:24]], axis=-1)


def _rope_angles(length):
    rows = length // GRID_W
    row = jnp.repeat(jnp.arange(rows, dtype=F32), GRID_W)
    col = jnp.tile(jnp.arange(GRID_W, dtype=F32), rows)
    n_freq = QK_ROPE // 4
    inv_freq = ROPE_BASE ** (-jnp.arange(n_freq, dtype=F32) / n_freq)
    ar = row[:, None] * inv_freq
    ac = col[:, None] * inv_freq
    cos = jnp.concatenate([jnp.cos(ar), jnp.cos(ar), jnp.cos(ac), jnp.cos(ac)], axis=1)
    sin = jnp.concatenate([jnp.sin(ar), jnp.sin(ar), jnp.sin(ac), jnp.sin(ac)], axis=1)
    return cos, sin


def _rope_tables(gain, length, cos, sin, scale):
    g_nope = jnp.broadcast_to(gain[:QK_NOPE], (length, QK_NOPE))
    g_rope = gain[QK_NOPE:]
    pad = jnp.zeros((length, LANES - QK_HEAD), F32)
    ca = jnp.concatenate([g_nope, g_rope * cos, pad], axis=1) * scale
    cb = jnp.concatenate([jnp.zeros((length, QK_NOPE), F32), _swap(g_rope) * sin, pad], axis=1) * scale
    return ca, cb


def _dft_table_kernel(ca_ref, sa_ref, cb_ref, sb_ref, o_ref):
    tr = ca_ref.shape[0]
    first = lax.broadcasted_iota(jnp.int32, (tr, LANES), 1) < GRID_W
    cb, sb = cb_ref[...], sb_ref[...]
    per_tile = LANES // GRID_W
    for t in range(ca_ref.shape[1] // per_tile):
        ca = jnp.where(first, ca_ref[:, per_tile * t:per_tile * t + 1], ca_ref[:, per_tile * t + 1:per_tile * t + 2])
        sa = jnp.where(first, sa_ref[:, per_tile * t:per_tile * t + 1], sa_ref[:, per_tile * t + 1:per_tile * t + 2])
        cols = slice(t * LANES, (t + 1) * LANES)
        o_ref[:tr, cols] = (ca * cb - sa * sb).astype(BF16)
        o_ref[tr:, cols] = (sa * cb + ca * sb).astype(BF16)


def _dft_tables(length, tr):
    k = jnp.arange(length, dtype=jnp.int32)[:, None]
    n1 = jnp.arange(length // GRID_W, dtype=jnp.int32)[None, :]
    n2 = jnp.arange(GRID_W, dtype=jnp.int32)[None, :]
    coarse = ((k * n1 * GRID_W) % length).astype(F32) * (2.0 * jnp.pi / length)
    fine = ((k * n2) % length).astype(F32) * (2.0 * jnp.pi / length)
    norm = length ** -0.5
    fine2 = jnp.concatenate([fine, fine], axis=1)
    cs = pl.pallas_call(
        _dft_table_kernel,
        grid=(length // tr,),
        in_specs=[pl.BlockSpec((tr, length // GRID_W), lambda r: (r, 0))] * 2
        + [pl.BlockSpec((tr, LANES), lambda r: (r, 0))] * 2,
        out_specs=pl.BlockSpec((None, 2 * tr, length), lambda r: (r, 0, 0)),
        out_shape=jax.ShapeDtypeStruct((length // tr, 2 * tr, length), BF16),
        compiler_params=_cparams("parallel"),
        name="dft_table",
    )(jnp.cos(coarse) * norm, jnp.sin(coarse) * norm, jnp.cos(fine2), jnp.sin(fine2))
    kc = jnp.arange(SUB_W, dtype=jnp.int32)
    angc = ((kc[:, None] * kc[None, :]) % SUB_W).astype(F32) * (2.0 * jnp.pi / SUB_W)
    eye = jnp.eye(N_GROUPS, dtype=F32)
    cc = jnp.kron(eye, jnp.cos(angc) * SUB_W ** -0.5).astype(BF16)
    scn = jnp.kron(eye, -jnp.sin(angc) * SUB_W ** -0.5).astype(BF16)
    return cs, cc, scn


def _layer_weights(i, w_in, gmlp_norm_g, spatial_w, spatial_b, pool_w, pool_scale, q_norm_g, w_uq, kv_norm_g,
                   w_ukv, fourier_w, w_out, norm1_g, norm2_g):
    wi = w_in[i]
    ckr = wi[:, OFF_CKR:OFF_D]
    win = jnp.concatenate([wi[:, OFF_A:OFF_B], wi[:, OFF_B:OFF_CQ], wi[:, OFF_D:IN_W], wi[:, OFF_CKV:OFF_CKR],
                           wi[:, OFF_CQ:OFF_CKV], ckr, _rot(ckr)], axis=1).astype(BF16)
    wcat = spatial_w[i].transpose(1, 0, 2).reshape(CHUNK, N_GROUPS * CHUNK).astype(BF16)
    bfull = jnp.repeat(spatial_b[i].T, SUB_W, axis=1)
    qng = jnp.concatenate([q_norm_g[i], jnp.zeros((2 * LANES - Q_LORA,), F32)])[None]
    wq4 = w_uq[i].reshape(Q_LORA, MLA_HEADS, QK_HEAD)
    wq = jnp.concatenate([wq4, _rot(wq4[..., QK_NOPE:])], axis=-1).reshape(Q_LORA, MLA_HEADS * LANES)
    wq = jnp.concatenate([wq, jnp.zeros((2 * LANES - Q_LORA, MLA_HEADS * LANES), F32)], axis=0).astype(BF16)
    wkv4 = w_ukv[i].reshape(KV_LORA, MLA_HEADS, QK_NOPE + V_HEAD)
    wk = jnp.concatenate([wkv4[..., :QK_NOPE], jnp.zeros((KV_LORA, MLA_HEADS, LANES - QK_NOPE), F32)], axis=-1)
    wkv = jnp.concatenate([wk.reshape(KV_LORA, MLA_HEADS * LANES),
                           wkv4[..., QK_NOPE:].reshape(KV_LORA, MLA_HEADS * V_HEAD)], axis=1).astype(BF16)
    in_w = (norm1_g[i][None], win, gmlp_norm_g[i][None], wcat, bfull, qng, wq, kv_norm_g[i][None], wkv)
    pw = jnp.zeros((GROUP_W, GROUP_W), F32)
    for g in range(N_GROUPS):
        pw = pw.at[g * SUB_W:(g + 1) * SUB_W, g * SUB_W:(g + 1) * SUB_W].set(pool_w[i, g])
    merge_w = (pw.astype(BF16), pool_scale[i][None], w_out[i].reshape(N_GROUPS, GROUP_W, D_MODEL).astype(BF16),
               norm2_g[i][None])
    return in_w, merge_w, fourier_w[i].astype(BF16)


def _pad_mod(m):
    m = m.reshape(m.shape[:-1] + (N_ADA, D_MODEL))
    return jnp.concatenate([m, jnp.zeros(m.shape[:-2] + (SUBLANES - N_ADA, D_MODEL), F32)], axis=-2)


def _tiles(seq_len):
    return dict(tin=min(seq_len, 512), tmg=min(seq_len, 1024), tq=min(seq_len, 512), tr=min(seq_len, 1024))


def kernel(x, c, ctx, c_ctx, ada_w, ada_b, norm1_g, w_in, gmlp_norm_g, spatial_w, spatial_b, pool_w, pool_scale,
           q_norm_g, w_uq, kv_norm_g, w_ukv, qk_q_g, qk_k_g, fourier_w, w_out, norm2_g, ffn_w_gate, ffn_w_up,
           ffn_w_down, router_w, moe_w_gate, moe_w_up, moe_w_down):
    bsz, seq, d = x.shape
    lc = ctx.shape[1]
    depth = ada_w.shape[0]
    n_tok = bsz * seq
    tl, tcx = _tiles(seq), _tiles(lc)
    ffn_tm, ffn_tmc, ffn_tf = min(seq, 1024), min(bsz * lc, 1024), 1792
    moe_tm = 1024

    c_all = jnp.concatenate([c, c_ctx[None], jnp.zeros((SUBLANES - 1, d), F32)], axis=0)
    mods = _ada_call(c_all, ada_w, ada_b)
    cos, sin = _rope_angles(seq)
    ones, zeros = jnp.ones((lc, QK_ROPE), F32), jnp.zeros((lc, QK_ROPE), F32)
    dft_l = _dft_tables(seq, tl["tr"])
    dft_c = _dft_tables(lc, tcx["tr"])

    xc = ctx
    for i in range(depth):
        last = i == depth - 1
        moe = i % 2 == 1
        mod_l = _pad_mod(mods[i, :bsz])
        mod_c = _pad_mod(mods[i, bsz:bsz + 1])
        in_w, merge_w, wf = _layer_weights(i, w_in, gmlp_norm_g, spatial_w, spatial_b, pool_w, pool_scale,
                                           q_norm_g, w_uq, kv_norm_g, w_ukv, fourier_w, w_out, norm1_g, norm2_g)
        scale = QK_HEAD ** -0.5 * LOG2_E
        tabs_l = _rope_tables(qk_q_g[i], seq, cos, sin, scale) + _rope_tables(qk_k_g[i], seq, cos, sin, 1.0)
        tabs_c = _rope_tables(qk_q_g[i], lc, ones, zeros, scale) + _rope_tables(qk_k_g[i], lc, ones, zeros, 1.0)

        ya, zb, zd, q, k, v = _inproj_call(x, mod_l, False, in_w, tabs_l, tl["tin"])
        yac, zbc, zdc, qc, kc, vc = _inproj_call(xc, mod_c, True, in_w, tabs_c, tcx["tin"])
        ffn_w = ((moe_w_gate, moe_w_up, moe_w_down) if moe else (ffn_w_gate, ffn_w_up, ffn_w_down))
        ffn_w = [w[i // 2] for w in ffn_w]
        attn, *ffn_w16 = _attn_call(q, [kc, k], [vc, v], tl["tq"],
                                    cast=[w.reshape(-1, w.shape[-1]) for w in ffn_w])
        wg, wu, wd = [w16.reshape((-1,) + w.shape[-2:]) for w16, w in zip(ffn_w16, ffn_w)]
        yd = _fourier_call(zd, *dft_l, wf, tl["tr"])

        if not moe:
            j = i // 2
            xn, h2 = _merge_call(x, ya, zb, attn, yd, mod_l, False, merge_w, tl["tmg"], False)
            n_tiles = n_tok // ffn_tm
            x = _ffn_call(h2.reshape(n_tok, d), jnp.zeros((n_tiles,), jnp.int32),
                          jnp.full((n_tiles,), ffn_tm, jnp.int32),
                          wg, wu, wd, ffn_tm, ffn_tf, False, False, res=xn.reshape(n_tok, d), mod=mod_l,
                          tiles_per_mod=seq // ffn_tm).reshape(bsz, seq, d)
            if not last:
                attn_c = _attn_call(qc, [kc], [vc], tcx["tq"])
                ydc = _fourier_call(zdc, *dft_c, wf, tcx["tr"])
                xnc, h2c = _merge_call(xc, yac, zbc, attn_c, ydc, mod_c, True, merge_w, tcx["tmg"], False)
                nc_tok = bsz * lc
                nct = nc_tok // ffn_tmc
                xc = _ffn_call(h2c.reshape(nc_tok, d), jnp.zeros((nct,), jnp.int32),
                               jnp.full((nct,), ffn_tmc, jnp.int32),
                               wg, wu, wd, ffn_tmc, ffn_tf, False, False, res=xnc.reshape(nc_tok, d), mod=mod_c,
                               tiles_per_mod=nct).reshape(bsz, lc, d)
        else:
            j = i // 2
            rw32 = jnp.concatenate([router_w[j], jnp.zeros((d, LANES - N_EXPERTS), F32)], axis=1)
            rw_hi = rw32.astype(BF16)
            rw = jnp.stack([rw_hi, (rw32 - rw_hi.astype(F32)).astype(BF16)])

            def moe_ffn(xs, ya_, zb_, attn_, yd_, mod_, shared, tms):
                b_, l_, _ = xs.shape
                n_ = b_ * l_
                xn, h_tok, logits = _merge_call(xs, ya_, zb_, attn_, yd_, mod_, shared, merge_w, tms, True, rw)
                route, counts = _route_call(logits.reshape(n_, LANES), min(n_, 512))
                eidx = route[:, 0:TOP_K].astype(jnp.int32)
                rank = route[:, 4:4 + TOP_K].astype(jnp.int32)
                cnt = counts[0, :N_EXPERTS].astype(jnp.int32)
                padded = (cnt + moe_tm - 1) // moe_tm * moe_tm
                ends = jnp.cumsum(padded)
                starts = ends - padded
                onehot = eidx[..., None] == jnp.arange(N_EXPERTS, dtype=jnp.int32)
                pos = jnp.sum(jnp.where(onehot, starts, 0), axis=-1) + rank
                n_tiles = TOP_K * n_ // moe_tm + N_EXPERTS
                tile_row = jnp.arange(n_tiles, dtype=jnp.int32) * moe_tm
                tile_e = jnp.minimum(jnp.sum(tile_row[:, None] >= ends[None, :], axis=1), N_EXPERTS - 1)
                used = jnp.clip((starts + cnt)[tile_e] - tile_row, 0, moe_tm)
                half = moe_tm // 2
                tile_v = jnp.where(used > half, moe_tm, jnp.where(used > 0, half, 0)).astype(jnp.int32)
                tail = jnp.stack([ends[-1], (n_tiles * moe_tm - ends[-1]) // (moe_tm // 2)])
                pad_info = jnp.concatenate([starts + cnt, padded - cnt, tail]).astype(jnp.int32)
                xs_tok = _dispatch_call(pos, pad_info, h_tok, n_tiles * moe_tm, min(n_, 1024), moe_tm)
                y_tok = _ffn_call(xs_tok, tile_e.astype(jnp.int32), tile_v, wg, wu, wd, moe_tm, ffn_tf, True, True,
                                  row_options=(moe_tm, half))
                mod_full = jnp.broadcast_to(mod_, (b_, SUBLANES, d)) if shared else mod_
                out = _combine_call(pos, y_tok, xn.reshape(n_, d), route, mod_full, l_, min(l_, 256))
                return out.reshape(b_, l_, d)

            x_new = moe_ffn(x, ya, zb, attn, yd, mod_l, False, tl["tmg"])
            if not last:
                attn_c = _attn_call(qc, [kc], [vc], tcx["tq"])
                ydc = _fourier_call(zdc, *dft_c, wf, tcx["tr"])
                xc = moe_ffn(xc, yac, zbc, attn_c, ydc, mod_c, True, tcx["tmg"])
            x = x_new
    return x
```

```python
import functools

import jax
import jax.numpy as jnp
from jax import lax
from jax.experimental import pallas as pl
from jax.experimental.pallas import tpu as pltpu

F32 = jnp.float32
BF16 = jnp.bfloat16

D_MODEL = 1024
GROUP_W = 256
SUB_W = 64
N_GROUPS = 4
CHUNK = 128
POOL_WINDOWS = (2, 4, 8, 16)
POOL_HALO = 16
MLA_HEADS = 4
QK_NOPE = 64
QK_ROPE = 32
V_HEAD = 64
QK_HEAD = QK_NOPE + QK_ROPE
Q_LORA = 192
KV_LORA = 128
GRID_W = 64
ROPE_BASE = 10000.0
D_FF = 3584
N_EXPERTS = 8
TOP_K = 2
N_ADA = 6
EPS = 1e-6
LOG2_E = 1.4426950408889634

OFF_A, OFF_B, OFF_CQ, OFF_CKV, OFF_CKR, OFF_D, IN_W = 0, 512, 768, 960, 1088, 1120, 1376
P_A, P_B, P_D, P_CKV, P_CQ, P_W = 0, 512, 768, 1024, 1152, 1408

LANES = 128
SUBLANES = 8
TOK_ROWS = D_MODEL // LANES
VMEM_LIMIT = 56 * 1024 * 1024


def _cparams(*sem):
    return pltpu.CompilerParams(dimension_semantics=sem, vmem_limit_bytes=VMEM_LIMIT)


def _rms(x):
    return lax.rsqrt(jnp.mean(x * x, axis=-1, keepdims=True) + EPS)


def _ada_kernel(c_ref, w_ref, b_ref, o_ref):
    ca = jax.nn.silu(c_ref[...])
    o_ref[...] = jnp.dot(ca, w_ref[...], preferred_element_type=F32, precision=lax.Precision.HIGHEST) + b_ref[...]


def _ada_call(c_all, ada_w, ada_b):
    depth, d, n = ada_w.shape
    rows = c_all.shape[0]
    tn = 1024
    return pl.pallas_call(
        _ada_kernel,
        grid=(depth, n // tn),
        in_specs=[
            pl.BlockSpec((rows, d), lambda i, j: (0, 0)),
            pl.BlockSpec((None, d, tn), lambda i, j: (i, 0, j)),
            pl.BlockSpec((None, 1, tn), lambda i, j: (i, 0, j)),
        ],
        out_specs=pl.BlockSpec((None, rows, tn), lambda i, j: (i, 0, j)),
        out_shape=jax.ShapeDtypeStruct((depth, rows, n), F32),
        compiler_params=_cparams("arbitrary", "arbitrary"),
        name="ada_mod",
    )(c_all, ada_w, ada_b.reshape(depth, 1, n))


INPROJ_SUB = 512


def _inproj_kernel(x_ref, mod_ref, n1g_ref, win_ref, gg_ref, wcat_ref, bfull_ref, qng_ref, wq_ref, kvng_ref,
                   wkv_ref, caq_ref, cbq_ref, cak_ref, cbk_ref,
                   ya_ref, zb_ref, zd_ref, q_ref, k_ref, v_ref, *, sub):
    tm = x_ref.shape[0]
    gain1 = n1g_ref[...] * (1.0 + mod_ref[1:2, :])
    sh1 = mod_ref[0:1, :]

    def project(rows):
        xt = x_ref[rows, :]
        h = xt * _rms(xt) * gain1 + sh1
        return jnp.dot(h.astype(BF16), win_ref[...], preferred_element_type=F32)

    grp = lax.broadcasted_iota(jnp.int32, (CHUNK, GROUP_W), 1) // SUB_W
    lane2 = lax.broadcasted_iota(jnp.int32, (sub, 2 * LANES), 1)
    lane = lax.broadcasted_iota(jnp.int32, (sub, LANES), 1)
    head_rows = lax.broadcasted_iota(jnp.int32, (LANES, LANES), 0) < QK_HEAD
    head_ones = jnp.where(head_rows, 1.0, 0.0).astype(BF16)

    def head_rms(blk):
        ssq = jnp.dot((blk * blk).astype(BF16), head_ones, preferred_element_type=F32)
        return lax.rsqrt(ssq / QK_HEAD + EPS)

    def mixers(p, r0):
        rows = slice(r0, r0 + sub)
        uv = jax.nn.gelu(p[:, P_A:P_A + 2 * GROUP_W])
        u = uv[:, :GROUP_W]
        vv = uv[:, GROUP_W:]
        vn = vv * _rms(vv) * gg_ref[...]
        for c in range(sub // CHUNK):
            crow = slice(c * CHUNK, (c + 1) * CHUNK)
            vc = vn[crow]
            vstack = jnp.concatenate([jnp.where(grp == g, vc, 0.0).astype(BF16) for g in range(N_GROUPS)], axis=0)
            mixed = jnp.dot(wcat_ref[...], vstack, preferred_element_type=F32) + bfull_ref[...]
            ya_ref[r0 + c * CHUNK:r0 + (c + 1) * CHUNK, :] = (u[crow] * mixed).astype(BF16)

        zb_ref[rows, :] = p[:, P_B:P_B + GROUP_W].astype(BF16)
        zd_ref[rows, :] = p[:, P_D:P_D + GROUP_W].astype(BF16)

        cq = p[:, P_CQ:P_CQ + 2 * LANES]
        cq_ss = jnp.sum(jnp.where(lane2 < Q_LORA, cq * cq, 0.0), axis=-1, keepdims=True)
        xq = cq * lax.rsqrt(cq_ss / Q_LORA + EPS) * qng_ref[...]
        qraw = jnp.dot(xq.astype(BF16), wq_ref[...], preferred_element_type=F32)
        ckv = p[:, P_CKV:P_CKV + KV_LORA]
        xkv = ckv * _rms(ckv) * kvng_ref[...]
        kvraw = jnp.dot(xkv.astype(BF16), wkv_ref[...], preferred_element_type=F32)
        krope = p[:, P_CQ + LANES:P_CQ + 2 * LANES]
        caq, cbq, cak, cbk = caq_ref[rows, :], cbq_ref[rows, :], cak_ref[rows, :], cbk_ref[rows, :]
        for hd in range(MLA_HEADS):
            cols = slice(hd * LANES, (hd + 1) * LANES)
            qb = qraw[:, cols]
            rq = head_rms(qb)
            q_ref[rows, cols] = ((qb * caq + pltpu.roll(qb, LANES - QK_ROPE, 1) * cbq) * rq).astype(BF16)
            kb = jnp.where(lane < QK_NOPE, kvraw[:, cols], krope)
            rk = head_rms(kb)
            k_ref[rows, cols] = ((kb * cak + pltpu.roll(kb, LANES - QK_ROPE, 1) * cbk) * rk).astype(BF16)
        v_ref[rows, :] = kvraw[:, MLA_HEADS * LANES:].astype(BF16)

    starts = range(0, tm, sub)
    ps = [project(slice(r0, r0 + sub)) for r0 in starts]
    for p, r0 in zip(ps, starts):
        mixers(p, r0)


def _inproj_call(x, mod, shared_mod, wts, tabs, tm):
    bs, ls, d = x.shape
    full = lambda a: pl.BlockSpec(a.shape, lambda b, j: (0,) * a.ndim)
    mod_map = (lambda b, j: (0, 0, 0)) if shared_mod else (lambda b, j: (b, 0, 0))
    tab_spec = pl.BlockSpec((tm, LANES), lambda b, j: (j, 0))
    tok = lambda w: pl.BlockSpec((None, tm, w), lambda b, j: (b, j, 0))
    widths = (GROUP_W, GROUP_W, GROUP_W, MLA_HEADS * LANES, MLA_HEADS * LANES, MLA_HEADS * V_HEAD)
    return pl.pallas_call(
        functools.partial(_inproj_kernel, sub=min(tm, INPROJ_SUB)),
        grid=(bs, ls // tm),
        in_specs=[tok(d), pl.BlockSpec((None, SUBLANES, d), mod_map)] + [full(w) for w in wts] + [tab_spec] * 4,
        out_specs=[tok(w) for w in widths],
        out_shape=[jax.ShapeDtypeStruct((bs, ls, w), BF16) for w in widths],
        compiler_params=_cparams("parallel", "parallel"),
        name="inproj",
    )(x, mod, *wts, *tabs)


SCORES_AHEAD = 1
KEY_CHUNK = 256


def _attn_kernel(*refs, n_kv, n_cast):
    q_ref = refs[0]
    k_refs = refs[1:1 + n_kv]
    v_refs = refs[1 + n_kv:1 + 2 * n_kv]
    cast_in = refs[1 + 2 * n_kv:1 + 2 * n_kv + n_cast]
    o_ref = refs[1 + 2 * n_kv + n_cast]
    cast_out = refs[2 + 2 * n_kv + n_cast:]
    for src, dst in zip(cast_in, cast_out):
        dst[...] = src[...].astype(BF16)
    tq = q_ref.shape[0]
    head_of_lane = lax.broadcasted_iota(jnp.int32, (tq, MLA_HEADS * V_HEAD), 1) // V_HEAD
    out = jnp.zeros((tq, MLA_HEADS * V_HEAD), F32)

    def scores(hd):
        cols = slice(hd * LANES, (hd + 1) * LANES)
        qh = q_ref[:, cols]
        return [lax.dot_general(qh, kr[:, cols], (((1,), (1,)), ((), ())), preferred_element_type=F32)
                for kr in k_refs]

    ahead = [scores(hd) for hd in range(min(SCORES_AHEAD, MLA_HEADS))]
    for hd in range(MLA_HEADS):
        ss = ahead.pop(0)
        if hd + SCORES_AHEAD < MLA_HEADS:
            ahead.append(scores(hd + SCORES_AHEAD))
        m = functools.reduce(jnp.maximum, [jnp.max(s, axis=-1, keepdims=True) for s in ss])
        den = jnp.zeros((tq, 1), F32)
        o = jnp.zeros((tq, MLA_HEADS * V_HEAD), F32)
        for s, vr in zip(ss, v_refs):
            for k0 in range(0, s.shape[1], KEY_CHUNK):
                keys = slice(k0, min(k0 + KEY_CHUNK, s.shape[1]))
                pp = jnp.exp2(s[:, keys] - m)
                den = den + jnp.sum(pp, axis=-1, keepdims=True)
                o = o + jnp.dot(pp.astype(BF16), vr[keys, :], preferred_element_type=F32)
        out = jnp.where(head_of_lane == hd, o / den, out)
    o_ref[...] = out.astype(BF16)


BF16_ROWS = 16


def _cast_blocks(rows, steps):
    n = steps
    while rows % n or (rows // n) % BF16_ROWS:
        n //= 2
    return n


def _attn_call(q, ks, vs, tq, cast=()):
    bs, lq, _ = q.shape
    n_kv = len(ks)
    nj = lq // tq
    steps = bs * nj
    kv_spec = lambda a: pl.BlockSpec((None,) + a.shape[1:], lambda b, j: (b, 0, 0))

    def cast_spec(a):
        n = _cast_blocks(a.shape[0], steps)
        return pl.BlockSpec((a.shape[0] // n, a.shape[1]), lambda b, j: ((b * nj + j) * n // steps, 0))

    out = pl.pallas_call(
        functools.partial(_attn_kernel, n_kv=n_kv, n_cast=len(cast)),
        grid=(bs, nj),
        in_specs=[pl.BlockSpec((None, tq, q.shape[2]), lambda b, j: (b, j, 0))]
        + [kv_spec(a) for a in ks] + [kv_spec(a) for a in vs] + [cast_spec(a) for a in cast],
        out_specs=[pl.BlockSpec((None, tq, MLA_HEADS * V_HEAD), lambda b, j: (b, j, 0))]
        + [cast_spec(a) for a in cast],
        out_shape=[jax.ShapeDtypeStruct((bs, lq, MLA_HEADS * V_HEAD), BF16)]
        + [jax.ShapeDtypeStruct(a.shape, BF16) for a in cast],
        compiler_params=_cparams("arbitrary", "arbitrary"),
        name="attention",
    )(q, *ks, *vs, *cast)
    return out if cast else out[0]


FOURIER_SUB = 512


def _fourier_kernel(cs_ref, z_ref, cc_ref, scn_ref, wf_ref, o_ref):
    tr = o_ref.shape[0]
    sub = min(tr, FOURIER_SUB)
    starts = range(0, tr, sub)
    tcs = [(jnp.dot(cs_ref[r0:r0 + sub, :], z_ref[...], preferred_element_type=F32),
            jnp.dot(cs_ref[tr + r0:tr + r0 + sub, :], z_ref[...], preferred_element_type=F32))
           for r0 in starts]
    for (tc, ts), r0 in zip(tcs, starts):
        y = (jnp.dot(tc.astype(BF16), cc_ref[...], preferred_element_type=F32)
             + jnp.dot(ts.astype(BF16), scn_ref[...], preferred_element_type=F32))
        o_ref[r0:r0 + sub, :] = jnp.dot(y.astype(BF16), wf_ref[...], preferred_element_type=F32).astype(BF16)


def _fourier_call(zd, cs, cc, scn, wf, tr):
    bs, ls, w = zd.shape
    full = lambda a: pl.BlockSpec(a.shape, lambda r, b: (0,) * a.ndim)
    return pl.pallas_call(
        _fourier_kernel,
        grid=(ls // tr, bs),
        in_specs=[pl.BlockSpec((None, 2 * tr, ls), lambda r, b: (r, 0, 0)),
                  pl.BlockSpec((None, ls, w), lambda r, b: (b, 0, 0)), full(cc), full(scn), full(wf)],
        out_specs=pl.BlockSpec((None, tr, w), lambda r, b: (b, r, 0)),
        out_shape=jax.ShapeDtypeStruct((bs, ls, w), BF16),
        compiler_params=_cparams("parallel", "arbitrary"),
        name="fourier",
    )(cs, zd, cc, scn, wf)


MERGE_SUB = 512


def _merge_kernel(*refs, seq_len, tm, sub, tok_layout, router):
    (x_ref, ya_ref, zb_ref, zbp_ref, zbn_ref, at_ref, yd_ref, mod_ref, pw_ref, ps_ref, wo_ref, n2g_ref) = refs[:12]
    rest = refs[12:]
    if router:
        rw_ref, rest = rest[0], rest[1:]
    xn_ref, h_ref = rest[0], rest[1]
    j = pl.program_id(1)
    nj = pl.num_programs(1)

    starts = range(0, tm, sub)
    parts = []
    for r0 in starts:
        rows = slice(r0, r0 + sub)
        part = jnp.dot(ya_ref[rows, :], wo_ref[0], preferred_element_type=F32)
        part += jnp.dot(at_ref[rows, :], wo_ref[2], preferred_element_type=F32)
        part += jnp.dot(yd_ref[rows, :], wo_ref[3], preferred_element_type=F32)
        parts.append(part)

    zm = zb_ref[...].astype(F32)
    zp = jnp.where(j > 0, zbp_ref[...].astype(F32), 0.0)
    zn = jnp.where(j < nj - 1, zbn_ref[...].astype(F32), 0.0)
    ext = jnp.concatenate([zp, zm, zn], axis=0)
    n = tm + 2 * POOL_HALO

    def ahead(a, k):
        return pltpu.roll(a, n - k, 0)

    tg = j * tm + lax.broadcasted_iota(jnp.int32, (tm, 1), 0)
    low_group = lax.broadcasted_iota(jnp.int32, (tm, LANES), 1) < SUB_W

    def window_mean(d, w):
        cnt = jnp.minimum(tg - w // 2 + w, seq_len) - jnp.maximum(tg - w // 2, 0)
        return ahead(d, POOL_HALO - w // 2)[:tm] * (1.0 / cnt.astype(F32))

    halves = []
    for half in range(GROUP_W // LANES):
        w_lo, w_hi = POOL_WINDOWS[2 * half], POOL_WINDOWS[2 * half + 1]
        d, width, means = ext[:, half * LANES:(half + 1) * LANES], 1, {}
        while width < w_hi:
            d = d + ahead(d, width)
            width *= 2
            if width in (w_lo, w_hi):
                means[width] = window_mean(d, width)
        halves.append(jnp.where(low_group, means[w_lo], means[w_hi]))
    diff = jnp.concatenate(halves, axis=1) - zm
    yb = jnp.dot(diff.astype(BF16), pw_ref[...], preferred_element_type=F32) * ps_ref[...]

    yb = yb.astype(BF16)
    g1 = mod_ref[2:3, :]
    sh2 = mod_ref[3:4, :]
    gain2 = n2g_ref[...] * (1.0 + mod_ref[4:5, :])

    def project(part, rows):
        acc = part + jnp.dot(yb[rows], wo_ref[1], preferred_element_type=F32)
        return x_ref[rows, :] + g1 * acc

    def modulate(xn, r0):
        rows = slice(r0, r0 + sub)
        xn_ref[rows, :] = xn
        h2 = xn * _rms(xn) * gain2 + sh2
        if tok_layout:
            for jj in range(TOK_ROWS):
                h_ref[pl.ds(r0 * TOK_ROWS + jj, sub, stride=TOK_ROWS), :] = h2[:, jj * LANES:(jj + 1) * LANES]
        else:
            h_ref[rows, :] = h2.astype(BF16)
        if router:
            lg_ref = rest[2]
            h_hi = h2.astype(BF16)
            h_lo = (h2 - h_hi.astype(F32)).astype(BF16)
            lg_ref[rows, :] = (jnp.dot(h_hi, rw_ref[0], preferred_element_type=F32)
                               + (jnp.dot(h_lo, rw_ref[0], preferred_element_type=F32)
                                  + jnp.dot(h_hi, rw_ref[1], preferred_element_type=F32)))

    xns = [project(part, slice(r0, r0 + sub)) for part, r0 in zip(parts, starts)]
    for xn, r0 in zip(xns, starts):
        modulate(xn, r0)


def _merge_call(x, ya, zb, attn, yd, mod, shared_mod, wts, tm, tok_layout, router_w=None):
    bs, ls, d = x.shape
    nj = ls // tm
    hb = tm // POOL_HALO
    full = lambda a: pl.BlockSpec(a.shape, lambda b, j: (0,) * a.ndim)
    mod_map = (lambda b, j: (0, 0, 0)) if shared_mod else (lambda b, j: (b, 0, 0))
    tok = lambda w: pl.BlockSpec((None, tm, w), lambda b, j: (b, j, 0))
    in_specs = [
        tok(d), tok(GROUP_W), tok(GROUP_W),
        pl.BlockSpec((None, POOL_HALO, GROUP_W), lambda b, j: (b, jnp.maximum(j * hb - 1, 0), 0)),
        pl.BlockSpec((None, POOL_HALO, GROUP_W), lambda b, j: (b, jnp.minimum((j + 1) * hb, nj * hb - 1), 0)),
        tok(GROUP_W), tok(GROUP_W),
        pl.BlockSpec((None, SUBLANES, d), mod_map),
    ] + [full(w) for w in wts]
    args = [x, ya, zb, zb, zb, attn, yd, mod, *wts]
    out_specs = [tok(d)]
    out_shape = [jax.ShapeDtypeStruct((bs, ls, d), F32)]
    if tok_layout:
        out_specs.append(pl.BlockSpec((tm * TOK_ROWS, LANES), lambda b, j: (b * nj + j, 0)))
        out_shape.append(jax.ShapeDtypeStruct((bs * ls * TOK_ROWS, LANES), F32))
    else:
        out_specs.append(tok(d))
        out_shape.append(jax.ShapeDtypeStruct((bs, ls, d), BF16))
    if router_w is not None:
        in_specs.append(full(router_w))
        args.append(router_w)
        out_specs.append(tok(LANES))
        out_shape.append(jax.ShapeDtypeStruct((bs, ls, LANES), F32))
    return pl.pallas_call(
        functools.partial(_merge_kernel, seq_len=ls, tm=tm, sub=min(tm, MERGE_SUB), tok_layout=tok_layout,
                          router=router_w is not None),
        grid=(bs, nj),
        in_specs=in_specs,
        out_specs=out_specs,
        out_shape=out_shape,
        compiler_params=_cparams("parallel", "parallel"),
        name="merge",
    )(*args)


ACC_COLS = 256
FFN_SUB = 512


def _ffn_kernel(te_ref, tv_ref, *refs, tm, tok_in, tok_out, residual, row_options):
    x_ref, wg_ref, wu_ref, wd_ref = refs[:4]
    rest = refs[4:]
    if residual:
        res_ref, mod_ref, rest = rest[0], rest[1], rest[2:]
    o_ref, acc_ref = rest[0], rest[1]
    i = pl.program_id(0)
    f = pl.program_id(1)
    nf = pl.num_programs(1)

    def store(y, r0, rows):
        if tok_out:
            for jj in range(TOK_ROWS):
                o_ref[pl.ds(r0 * TOK_ROWS + jj, rows, stride=TOK_ROWS), :] = y[:, jj * LANES:(jj + 1) * LANES]
        else:
            o_ref[r0:r0 + rows, :] = y

    def compute(m):
        @pl.when(f == 0)
        def _():
            acc_ref[:m, :] = jnp.zeros((m, D_MODEL), F32)
            if tok_in:
                for jj in range(TOK_ROWS):
                    rest[2][:m, jj * LANES:(jj + 1) * LANES] = x_ref[pl.ds(jj, m, stride=TOK_ROWS), :].astype(BF16)

        xb = rest[2][:m, :] if tok_in else x_ref[:m, :]
        tf = wg_ref.shape[1]
        acts = []
        for lo in range(0, tf, FFN_SUB):
            sub = slice(lo, min(lo + FFN_SUB, tf))
            gate = jnp.dot(xb, wg_ref[:, sub], preferred_element_type=F32)
            up = jnp.dot(xb, wu_ref[:, sub], preferred_element_type=F32)
            acts.append((jax.nn.silu(gate) * up).astype(BF16))
        act = jnp.concatenate(acts, axis=1)
        for cb in range(D_MODEL // ACC_COLS):
            cols = slice(cb * ACC_COLS, (cb + 1) * ACC_COLS)
            acc_ref[:m, cols] += jnp.dot(act, wd_ref[:, cols], preferred_element_type=F32)

        @pl.when(f == nf - 1)
        def _():
            y = acc_ref[:m, :]
            if residual:
                y = res_ref[:m, :] + mod_ref[5:6, :] * y
            store(y, 0, m)
            if m < tm:
                store(jnp.zeros((tm - m, D_MODEL), F32), m, tm - m)

    for m in row_options:
        pl.when(tv_ref[i] == m)(functools.partial(compute, m))

    @pl.when(jnp.logical_and(tv_ref[i] == 0, f == nf - 1))
    def _():
        store(jnp.zeros((tm, D_MODEL), F32), 0, tm)


def _ffn_call(x, tile_e, tile_v, wg, wu, wd, tm, tf, tok_in, tok_out, res=None, mod=None, tiles_per_mod=None,
              row_options=None):
    n_tiles = tile_e.shape[0]
    d, dff = wg.shape[1], wg.shape[2]
    nf = dff // tf
    last = nf - 1
    fsel = lambda i, f, te, tv: jnp.where(tv[i] > 0, f, last)
    xsel = lambda i, f, te, tv: (jnp.where(tv[i] > 0, i, 0), 0)
    x_spec = pl.BlockSpec((tm * TOK_ROWS, LANES), xsel) if tok_in else pl.BlockSpec((tm, d), xsel)
    in_specs = [
        x_spec,
        pl.BlockSpec((None, d, tf), lambda i, f, te, tv: (te[i], 0, fsel(i, f, te, tv))),
        pl.BlockSpec((None, d, tf), lambda i, f, te, tv: (te[i], 0, fsel(i, f, te, tv))),
        pl.BlockSpec((None, tf, d), lambda i, f, te, tv: (te[i], fsel(i, f, te, tv), 0)),
    ]
    args = [x, wg, wu, wd]
    residual = res is not None
    if residual:
        in_specs.append(pl.BlockSpec((tm, d), lambda i, f, te, tv: (i, 0)))
        in_specs.append(pl.BlockSpec((None, SUBLANES, d), lambda i, f, te, tv: (i // tiles_per_mod, 0, 0)))
        args += [res, mod]
    if tok_out:
        out_spec = pl.BlockSpec((tm * TOK_ROWS, LANES), lambda i, f, te, tv: (i, 0))
        out_shape = jax.ShapeDtypeStruct((n_tiles * tm * TOK_ROWS, LANES), F32)
    else:
        out_spec = pl.BlockSpec((tm, d), lambda i, f, te, tv: (i, 0))
        out_shape = jax.ShapeDtypeStruct((n_tiles * tm, d), F32)
    scratch = [pltpu.VMEM((tm, d), F32)]
    if tok_in:
        scratch.append(pltpu.VMEM((tm, d), BF16))
    return pl.pallas_call(
        functools.partial(_ffn_kernel, tm=tm, tok_in=tok_in, tok_out=tok_out, residual=residual,
                          row_options=row_options or (tm,)),
        grid_spec=pltpu.PrefetchScalarGridSpec(
            num_scalar_prefetch=2, grid=(n_tiles, nf), in_specs=in_specs, out_specs=out_spec,
            scratch_shapes=scratch),
        out_shape=out_shape,
        compiler_params=_cparams("parallel", "arbitrary"),
        name="ffn",
    )(tile_e, tile_v, *args)


def _route_kernel(lg_ref, o_ref, cnt_ref, carry_ref):
    tm = lg_ref.shape[0]
    i = pl.program_id(0)

    @pl.when(i == 0)
    def _():
        carry_ref[...] = jnp.zeros_like(carry_ref)

    lane_i = lax.broadcasted_iota(jnp.int32, (tm, LANES), 1)
    lane = lane_i.astype(F32)
    neg = jnp.float32(-jnp.inf)
    lg = jnp.where(lane_i < N_EXPERTS, lg_ref[...], neg)
    m1 = jnp.max(lg, axis=-1, keepdims=True)
    i1 = jnp.min(jnp.where(lg == m1, lane, float(LANES)), axis=-1, keepdims=True)
    lg2 = jnp.where(lane == i1, neg, lg)
    m2 = jnp.max(lg2, axis=-1, keepdims=True)
    i2 = jnp.min(jnp.where(lg2 == m2, lane, float(LANES)), axis=-1, keepdims=True)
    e2 = jnp.exp(m2 - m1)
    w1 = 1.0 / (1.0 + e2)
    w2 = e2 / (1.0 + e2)
    hit = jnp.logical_or(lane == i1, lane == i2)
    onehot = jnp.where(hit, 1.0, 0.0).astype(BF16)
    r = lax.broadcasted_iota(jnp.int32, (tm, tm), 0)
    c = lax.broadcasted_iota(jnp.int32, (tm, tm), 1)
    before = jnp.where(c < r, 1.0, 0.0).astype(BF16)
    carry = carry_ref[0:1, :]
    cum = jnp.dot(before, onehot, preferred_element_type=F32) + carry
    rank1 = jnp.sum(jnp.where(lane == i1, cum, 0.0), axis=-1, keepdims=True)
    rank2 = jnp.sum(jnp.where(lane == i2, cum, 0.0), axis=-1, keepdims=True)
    total = carry + jnp.sum(onehot.astype(F32), axis=0, keepdims=True)
    carry_ref[...] = jnp.broadcast_to(total, carry_ref.shape)
    cnt_ref[...] = jnp.broadcast_to(total, cnt_ref.shape)
    out = jnp.zeros((tm, LANES), F32)
    for col, val in enumerate((i1, i2, w1, w2, rank1, rank2)):
        out = jnp.where(lane_i == col, val, out)
    o_ref[...] = out


def _route_call(logits, tm):
    n = logits.shape[0]
    return pl.pallas_call(
        _route_kernel,
        grid=(n // tm,),
        in_specs=[pl.BlockSpec((tm, LANES), lambda i: (i, 0))],
        out_specs=[pl.BlockSpec((tm, LANES), lambda i: (i, 0)), pl.BlockSpec((SUBLANES, LANES), lambda i: (0, 0))],
        out_shape=[jax.ShapeDtypeStruct((n, LANES), F32), jax.ShapeDtypeStruct((SUBLANES, LANES), F32)],
        scratch_shapes=[pltpu.VMEM((SUBLANES, LANES), F32)],
        compiler_params=_cparams("arbitrary"),
        name="route",
    )(logits)


ISSUE_UNROLL = 8


def _token_rows(tok):
    start = tok * TOK_ROWS
    return pl.ds(start if isinstance(start, int) else pl.multiple_of(start, TOK_ROWS), TOK_ROWS)


def _token_copy(src, src_tok, dst, dst_tok, sem):
    return pltpu.make_async_copy(src.at[_token_rows(src_tok), :], dst.at[_token_rows(dst_tok), :], sem)


def _for_each_token(n_tok, body):
    def group(g, carry):
        for u in range(ISSUE_UNROLL):
            body(g * ISSUE_UNROLL + u)
        return carry

    lax.fori_loop(0, n_tok // ISSUE_UNROLL, group, 0)


def _dispatch_kernel(pad_ref, pos_ref, h_ref, xs_hbm, zbuf, sem, zsem, *, td, pad_bits, max_tail):
    i = pl.program_id(0)

    def pad_copies():
        for e in range(N_EXPERTS):
            first, length = pad_ref[e], pad_ref[N_EXPERTS + e]
            for b in range(pad_bits):
                size = 1 << b
                tok = first + jnp.bitwise_and(length, size - 1)
                copy = pltpu.make_async_copy(
                    zbuf.at[pl.ds(0, size * TOK_ROWS), :],
                    xs_hbm.at[pl.ds(pl.multiple_of(tok * TOK_ROWS, TOK_ROWS), size * TOK_ROWS), :], zsem)
                yield jnp.bitwise_and(length, size) != 0, copy
        first, pieces = pad_ref[2 * N_EXPERTS], pad_ref[2 * N_EXPERTS + 1]
        piece = zbuf.shape[0]
        for k in range(max_tail):
            copy = pltpu.make_async_copy(
                zbuf, xs_hbm.at[pl.ds(pl.multiple_of(first * TOK_ROWS + k * piece, TOK_ROWS), piece), :], zsem)
            yield k < pieces, copy

    @pl.when(i == 0)
    def _():
        zbuf[...] = jnp.zeros_like(zbuf)
        for needed, copy in pad_copies():
            pl.when(needed)(copy.start)

    def issue(t):
        for s in range(TOP_K):
            _token_copy(h_ref, t, xs_hbm, pos_ref[0, TOP_K * t + s], sem).start(priority=s % 2)

    _for_each_token(td, issue)
    for _ in range(TOP_K):
        pltpu.make_async_copy(h_ref, xs_hbm.at[pl.ds(0, td * TOK_ROWS), :], sem).wait()

    @pl.when(i == 0)
    def _():
        for needed, copy in pad_copies():
            pl.when(needed)(copy.wait)


def _dispatch_call(pos, pad_info, h_tok, n_slots, td, tile):
    n = pos.shape[0]
    pad_bits = tile.bit_length() - 1
    return pl.pallas_call(
        functools.partial(_dispatch_kernel, td=td, pad_bits=pad_bits, max_tail=2 * N_EXPERTS),
        grid_spec=pltpu.PrefetchScalarGridSpec(
            num_scalar_prefetch=1, grid=(n // td,),
            in_specs=[pl.BlockSpec((None, 1, TOP_K * td), lambda i, pad: (i, 0, 0), memory_space=pltpu.SMEM),
                      pl.BlockSpec((td * TOK_ROWS, LANES), lambda i, pad: (i, 0))],
            out_specs=pl.BlockSpec(memory_space=pl.ANY),
            scratch_shapes=[pltpu.VMEM((tile // 2 * TOK_ROWS, LANES), F32), pltpu.SemaphoreType.DMA,
                            pltpu.SemaphoreType.DMA]),
        out_shape=jax.ShapeDtypeStruct((n_slots * TOK_ROWS, LANES), F32),
        compiler_params=_cparams("arbitrary"),
        name="dispatch",
    )(pad_info, pos.reshape(n // td, 1, TOP_K * td), h_tok)


COMBINE_AHEAD = 2


def _combine_kernel(*refs, tc):
    pos_refs = refs[:COMBINE_AHEAD + 1]
    y_hbm, xn_ref, route_ref, mod_ref, o_ref = refs[COMBINE_AHEAD + 1:COMBINE_AHEAD + 6]
    bufs = refs[COMBINE_AHEAD + 6:-1]
    sems = refs[-1]
    n_buf = len(bufs)
    i = pl.program_id(0)
    n = pl.num_programs(0)

    def issue(p_ref, slot, t):
        for s in range(TOP_K):
            _token_copy(y_hbm, p_ref[0, TOP_K * t + s], bufs[slot], TOP_K * t + s,
                        sems.at[slot]).start(priority=s % 2)

    def blend(slot):
        buf = bufs[slot]
        g2 = mod_ref[5:6, :]
        w1 = route_ref[:, 2:3]
        w2 = route_ref[:, 3:4]
        for jj in range(TOK_ROWS):
            cols = slice(jj * LANES, (jj + 1) * LANES)
            ya = buf[pl.ds(jj, tc, stride=TOP_K * TOK_ROWS), :]
            yb = buf[pl.ds(TOK_ROWS + jj, tc, stride=TOP_K * TOK_ROWS), :]
            o_ref[:, cols] = xn_ref[:, cols] + g2[:, cols] * (w1 * ya + w2 * yb)

    @pl.when(i == 0)
    def _():
        for ahead in range(COMBINE_AHEAD):
            @pl.when(ahead < n)
            def _(ahead=ahead):
                _for_each_token(tc, functools.partial(issue, pos_refs[ahead], ahead))

    for slot in range(n_buf):
        @pl.when(i % n_buf == slot)
        def _(slot=slot):
            pltpu.make_async_copy(y_hbm.at[pl.ds(0, TOP_K * tc * TOK_ROWS), :], bufs[slot], sems.at[slot]).wait()

            @pl.when(i + COMBINE_AHEAD < n)
            def _():
                for t in range(tc):
                    issue(pos_refs[COMBINE_AHEAD], (slot + COMBINE_AHEAD) % n_buf, t)
                blend(slot)

            @pl.when(i + COMBINE_AHEAD >= n)
            def _():
                blend(slot)


def _combine_call(pos, y_tok, xn, route, mod, seq_len, tc):
    n, d = xn.shape
    steps = n // tc
    pos3 = pos.reshape(steps, 1, TOP_K * tc)
    pos_spec = lambda k: pl.BlockSpec((None, 1, TOP_K * tc), lambda i: (jnp.minimum(i + k, steps - 1), 0, 0),
                                      memory_space=pltpu.SMEM)
    n_buf = COMBINE_AHEAD + 1
    return pl.pallas_call(
        functools.partial(_combine_kernel, tc=tc),
        grid=(steps,),
        in_specs=[pos_spec(k) for k in range(n_buf)]
        + [pl.BlockSpec(memory_space=pl.ANY),
           pl.BlockSpec((tc, d), lambda i: (i, 0)),
           pl.BlockSpec((tc, LANES), lambda i: (i, 0)),
           pl.BlockSpec((None, SUBLANES, d), lambda i: (i * tc // seq_len, 0, 0))],
        out_specs=pl.BlockSpec((tc, d), lambda i: (i, 0)),
        out_shape=jax.ShapeDtypeStruct((n, d), F32),
        scratch_shapes=[pltpu.VMEM((TOP_K * tc * TOK_ROWS, LANES), F32)] * n_buf
        + [pltpu.SemaphoreType.DMA((n_buf,))],
        compiler_params=_cparams("arbitrary"),
        name="combine",
    )(*([pos3] * n_buf), y_tok, xn, route, mod)


def _rot(w):
    return jnp.concatenate([-w[..., 8:16], w[..., 0:8], -w[..., 24:32], w[..., 16:24]], axis=-1)


def _swap(g):
    return jnp.concatenate([g[..., 8:16], g[..., 0:8], g[..., 24:32], g[..., 16:24]], axis=-1)


def _rope_angles(length):
    rows = length // GRID_W
    row = jnp.repeat(jnp.arange(rows, dtype=F32), GRID_W)
    col = jnp.tile(jnp.arange(GRID_W, dtype=F32), rows)
    n_freq = QK_ROPE // 4
    inv_freq = ROPE_BASE ** (-jnp.arange(n_freq, dtype=F32) / n_freq)
    ar = row[:, None] * inv_freq
    ac = col[:, None] * inv_freq
    cos = jnp.concatenate([jnp.cos(ar), jnp.cos(ar), jnp.cos(ac), jnp.cos(ac)], axis=1)
    sin = jnp.concatenate([jnp.sin(ar), jnp.sin(ar), jnp.sin(ac), jnp.sin(ac)], axis=1)
    return cos, sin


def _rope_tables(gain, length, cos, sin, scale):
    g_nope = jnp.broadcast_to(gain[:QK_NOPE], (length, QK_NOPE))
    g_rope = gain[QK_NOPE:]
    pad = jnp.zeros((length, LANES - QK_HEAD), F32)
    ca = jnp.concatenate([g_nope, g_rope * cos, pad], axis=1) * scale
    cb = jnp.concatenate([jnp.zeros((length, QK_NOPE), F32), _swap(g_rope) * sin, pad], axis=1) * scale
    return ca, cb


def _dft_table_kernel(ca_ref, sa_ref, cb_ref, sb_ref, o_ref):
    tr = ca_ref.shape[0]
    first = lax.broadcasted_iota(jnp.int32, (tr, LANES), 1) < GRID_W
    cb, sb = cb_ref[...], sb_ref[...]
    per_tile = LANES // GRID_W
    for t in range(ca_ref.shape[1] // per_tile):
        ca = jnp.where(first, ca_ref[:, per_tile * t:per_tile * t + 1], ca_ref[:, per_tile * t + 1:per_tile * t + 2])
        sa = jnp.where(first, sa_ref[:, per_tile * t:per_tile * t + 1], sa_ref[:, per_tile * t + 1:per_tile * t + 2])
        cols = slice(t * LANES, (t + 1) * LANES)
        o_ref[:tr, cols] = (ca * cb - sa * sb).astype(BF16)
        o_ref[tr:, cols] = (sa * cb + ca * sb).astype(BF16)


def _dft_tables(length, tr):
    k = jnp.arange(length, dtype=jnp.int32)[:, None]
    n1 = jnp.arange(length // GRID_W, dtype=jnp.int32)[None, :]
    n2 = jnp.arange(GRID_W, dtype=jnp.int32)[None, :]
    coarse = ((k * n1 * GRID_W) % length).astype(F32) * (2.0 * jnp.pi / length)
    fine = ((k * n2) % length).astype(F32) * (2.0 * jnp.pi / length)
    norm = length ** -0.5
    fine2 = jnp.concatenate([fine, fine], axis=1)
    cs = pl.pallas_call(
        _dft_table_kernel,
        grid=(length // tr,),
        in_specs=[pl.BlockSpec((tr, length // GRID_W), lambda r: (r, 0))] * 2
        + [pl.BlockSpec((tr, LANES), lambda r: (r, 0))] * 2,
        out_specs=pl.BlockSpec((None, 2 * tr, length), lambda r: (r, 0, 0)),
        out_shape=jax.ShapeDtypeStruct((length // tr, 2 * tr, length), BF16),
        compiler_params=_cparams("parallel"),
        name="dft_table",
    )(jnp.cos(coarse) * norm, jnp.sin(coarse) * norm, jnp.cos(fine2), jnp.sin(fine2))
    kc = jnp.arange(SUB_W, dtype=jnp.int32)
    angc = ((kc[:, None] * kc[None, :]) % SUB_W).astype(F32) * (2.0 * jnp.pi / SUB_W)
    eye = jnp.eye(N_GROUPS, dtype=F32)
    cc = jnp.kron(eye, jnp.cos(angc) * SUB_W ** -0.5).astype(BF16)
    scn = jnp.kron(eye, -jnp.sin(angc) * SUB_W ** -0.5).astype(BF16)
    return cs, cc, scn


def _layer_weights(i, w_in, gmlp_norm_g, spatial_w, spatial_b, pool_w, pool_scale, q_norm_g, w_uq, kv_norm_g,
                   w_ukv, fourier_w, w_out, norm1_g, norm2_g):
    wi = w_in[i]
    ckr = wi[:, OFF_CKR:OFF_D]
    win = jnp.concatenate([wi[:, OFF_A:OFF_B], wi[:, OFF_B:OFF_CQ], wi[:, OFF_D:IN_W], wi[:, OFF_CKV:OFF_CKR],
                           wi[:, OFF_CQ:OFF_CKV], ckr, _rot(ckr)], axis=1).astype(BF16)
    wcat = spatial_w[i].transpose(1, 0, 2).reshape(CHUNK, N_GROUPS * CHUNK).astype(BF16)
    bfull = jnp.repeat(spatial_b[i].T, SUB_W, axis=1)
    qng = jnp.concatenate([q_norm_g[i], jnp.zeros((2 * LANES - Q_LORA,), F32)])[None]
    wq4 = w_uq[i].reshape(Q_LORA, MLA_HEADS, QK_HEAD)
    wq = jnp.concatenate([wq4, _rot(wq4[..., QK_NOPE:])], axis=-1).reshape(Q_LORA, MLA_HEADS * LANES)
    wq = jnp.concatenate([wq, jnp.zeros((2 * LANES - Q_LORA, MLA_HEADS * LANES), F32)], axis=0).astype(BF16)
    wkv4 = w_ukv[i].reshape(KV_LORA, MLA_HEADS, QK_NOPE + V_HEAD)
    wk = jnp.concatenate([wkv4[..., :QK_NOPE], jnp.zeros((KV_LORA, MLA_HEADS, LANES - QK_NOPE), F32)], axis=-1)
    wkv = jnp.concatenate([wk.reshape(KV_LORA, MLA_HEADS * LANES),
                           wkv4[..., QK_NOPE:].reshape(KV_LORA, MLA_HEADS * V_HEAD)], axis=1).astype(BF16)
    in_w = (norm1_g[i][None], win, gmlp_norm_g[i][None], wcat, bfull, qng, wq, kv_norm_g[i][None], wkv)
    pw = jnp.zeros((GROUP_W, GROUP_W), F32)
    for g in range(N_GROUPS):
        pw = pw.at[g * SUB_W:(g + 1) * SUB_W, g * SUB_W:(g + 1) * SUB_W].set(pool_w[i, g])
    merge_w = (pw.astype(BF16), pool_scale[i][None], w_out[i].reshape(N_GROUPS, GROUP_W, D_MODEL).astype(BF16),
               norm2_g[i][None])
    return in_w, merge_w, fourier_w[i].astype(BF16)


def _pad_mod(m):
    m = m.reshape(m.shape[:-1] + (N_ADA, D_MODEL))
    return jnp.concatenate([m, jnp.zeros(m.shape[:-2] + (SUBLANES - N_ADA, D_MODEL), F32)], axis=-2)


def _tiles(seq_len):
    return dict(tin=min(seq_len, 512), tmg=min(seq_len, 1024), tq=min(seq_len, 512), tr=min(seq_len, 1024))


def kernel(x, c, ctx, c_ctx, ada_w, ada_b, norm1_g, w_in, gmlp_norm_g, spatial_w, spatial_b, pool_w, pool_scale,
           q_norm_g, w_uq, kv_norm_g, w_ukv, qk_q_g, qk_k_g, fourier_w, w_out, norm2_g, ffn_w_gate, ffn_w_up,
           ffn_w_down, router_w, moe_w_gate, moe_w_up, moe_w_down):
    bsz, seq, d = x.shape
    lc = ctx.shape[1]
    depth = ada_w.shape[0]
    n_tok = bsz * seq
    tl, tcx = _tiles(seq), _tiles(lc)
    ffn_tm, ffn_tmc, ffn_tf = min(seq, 1024), min(bsz * lc, 1024), 1792
    moe_tm = 1024

    c_all = jnp.concatenate([c, c_ctx[None], jnp.zeros((SUBLANES - 1, d), F32)], axis=0)
    mods = _ada_call(c_all, ada_w, ada_b)
    cos, sin = _rope_angles(seq)
    ones, zeros = jnp.ones((lc, QK_ROPE), F32), jnp.zeros((lc, QK_ROPE), F32)
    dft_l = _dft_tables(seq, tl["tr"])
    dft_c = _dft_tables(lc, tcx["tr"])

    xc = ctx
    for i in range(depth):
        last = i == depth - 1
        moe = i % 2 == 1
        mod_l = _pad_mod(mods[i, :bsz])
        mod_c = _pad_mod(mods[i, bsz:bsz + 1])
        in_w, merge_w, wf = _layer_weights(i, w_in, gmlp_norm_g, spatial_w, spatial_b, pool_w, pool_scale,
                                           q_norm_g, w_uq, kv_norm_g, w_ukv, fourier_w, w_out, norm1_g, norm2_g)
        scale = QK_HEAD ** -0.5 * LOG2_E
        tabs_l = _rope_tables(qk_q_g[i], seq, cos, sin, scale) + _rope_tables(qk_k_g[i], seq, cos, sin, 1.0)
        tabs_c = _rope_tables(qk_q_g[i], lc, ones, zeros, scale) + _rope_tables(qk_k_g[i], lc, ones, zeros, 1.0)

        ya, zb, zd, q, k, v = _inproj_call(x, mod_l, False, in_w, tabs_l, tl["tin"])
        yac, zbc, zdc, qc, kc, vc = _inproj_call(xc, mod_c, True, in_w, tabs_c, tcx["tin"])
        ffn_w = ((moe_w_gate, moe_w_up, moe_w_down) if moe else (ffn_w_gate, ffn_w_up, ffn_w_down))
        ffn_w = [w[i // 2] for w in ffn_w]
        attn, *ffn_w16 = _attn_call(q, [kc, k], [vc, v], tl["tq"],
                                    cast=[w.reshape(-1, w.shape[-1]) for w in ffn_w])
        wg, wu, wd = [w16.reshape((-1,) + w.shape[-2:]) for w16, w in zip(ffn_w16, ffn_w)]
        yd = _fourier_call(zd, *dft_l, wf, tl["tr"])

        if not moe:
            j = i // 2
            xn, h2 = _merge_call(x, ya, zb, attn, yd, mod_l, False, merge_w, tl["tmg"], False)
            n_tiles = n_tok // ffn_tm
            x = _ffn_call(h2.reshape(n_tok, d), jnp.zeros((n_tiles,), jnp.int32),
                          jnp.full((n_tiles,), ffn_tm, jnp.int32),
                          wg, wu, wd, ffn_tm, ffn_tf, False, False, res=xn.reshape(n_tok, d), mod=mod_l,
                          tiles_per_mod=seq // ffn_tm).reshape(bsz, seq, d)
            if not last:
                attn_c = _attn_call(qc, [kc], [vc], tcx["tq"])
                ydc = _fourier_call(zdc, *dft_c, wf, tcx["tr"])
                xnc, h2c = _merge_call(xc, yac, zbc, attn_c, ydc, mod_c, True, merge_w, tcx["tmg"], False)
                nc_tok = bsz * lc
                nct = nc_tok // ffn_tmc
                xc = _ffn_call(h2c.reshape(nc_tok, d), jnp.zeros((nct,), jnp.int32),
                               jnp.full((nct,), ffn_tmc, jnp.int32),
                               wg, wu, wd, ffn_tmc, ffn_tf, False, False, res=xnc.reshape(nc_tok, d), mod=mod_c,
                               tiles_per_mod=nct).reshape(bsz, lc, d)
        else:
            j = i // 2
            rw32 = jnp.concatenate([router_w[j], jnp.zeros((d, LANES - N_EXPERTS), F32)], axis=1)
            rw_hi = rw32.astype(BF16)
            rw = jnp.stack([rw_hi, (rw32 - rw_hi.astype(F32)).astype(BF16)])

            def moe_ffn(xs, ya_, zb_, attn_, yd_, mod_, shared, tms):
                b_, l_, _ = xs.shape
                n_ = b_ * l_
                xn, h_tok, logits = _merge_call(xs, ya_, zb_, attn_, yd_, mod_, shared, merge_w, tms, True, rw)
                route, counts = _route_call(logits.reshape(n_, LANES), min(n_, 512))
                eidx = route[:, 0:TOP_K].astype(jnp.int32)
                rank = route[:, 4:4 + TOP_K].astype(jnp.int32)
                cnt = counts[0, :N_EXPERTS].astype(jnp.int32)
                padded = (cnt + moe_tm - 1) // moe_tm * moe_tm
                ends = jnp.cumsum(padded)
                starts = ends - padded
                onehot = eidx[..., None] == jnp.arange(N_EXPERTS, dtype=jnp.int32)
                pos = jnp.sum(jnp.where(onehot, starts, 0), axis=-1) + rank
                n_tiles = TOP_K * n_ // moe_tm + N_EXPERTS
                tile_row = jnp.arange(n_tiles, dtype=jnp.int32) * moe_tm
                tile_e = jnp.minimum(jnp.sum(tile_row[:, None] >= ends[None, :], axis=1), N_EXPERTS - 1)
                used = jnp.clip((starts + cnt)[tile_e] - tile_row, 0, moe_tm)
                quarter = moe_tm // 4
                tile_v = ((used + quarter - 1) // quarter * quarter).astype(jnp.int32)
                tail = jnp.stack([ends[-1], (n_tiles * moe_tm - ends[-1]) // (moe_tm // 2)])
                pad_info = jnp.concatenate([starts + cnt, padded - cnt, tail]).astype(jnp.int32)
                xs_tok = _dispatch_call(pos, pad_info, h_tok, n_tiles * moe_tm, min(n_, 1024), moe_tm)
                y_tok = _ffn_call(xs_tok, tile_e.astype(jnp.int32), tile_v, wg, wu, wd, moe_tm, ffn_tf, True, True,
                                  row_options=tuple(quarter * k for k in (4, 3, 2, 1)))
                mod_full = jnp.broadcast_to(mod_, (b_, SUBLANES, d)) if shared else mod_
                out = _combine_call(pos, y_tok, xn.reshape(n_, d), route, mod_full, l_, min(l_, 256))
                return out.reshape(b_, l_, d)

            x_new = moe_ffn(x, ya, zb, attn, yd, mod_l, False, tl["tmg"])
            if not last:
                attn_c = _attn_call(qc, [kc], [vc], tcx["tq"])
                ydc = _fourier_call(zdc, *dft_c, wf, tcx["tr"])
                xc = moe_ffn(xc, yac, zbc, attn_c, ydc, mod_c, True, tcx["tmg"])
            x = x_new
    return x
```

```python
import functools

import jax
import jax.numpy as jnp
from jax import lax
from jax.experimental import pallas as pl
from jax.experimental.pallas import tpu as pltpu

F32 = jnp.float32
BF16 = jnp.bfloat16

D_MODEL = 1024
GROUP_W = 256
SUB_W = 64
N_GROUPS = 4
CHUNK = 128
POOL_WINDOWS = (2, 4, 8, 16)
POOL_HALO = 16
MLA_HEADS = 4
QK_NOPE = 64
QK_ROPE = 32
V_HEAD = 64
QK_HEAD = QK_NOPE + QK_ROPE
Q_LORA = 192
KV_LORA = 128
GRID_W = 64
ROPE_BASE = 10000.0
D_FF = 3584
N_EXPERTS = 8
TOP_K = 2
N_ADA = 6
EPS = 1e-6
LOG2_E = 1.4426950408889634

OFF_A, OFF_B, OFF_CQ, OFF_CKV, OFF_CKR, OFF_D, IN_W = 0, 512, 768, 960, 1088, 1120, 1376
P_A, P_B, P_D, P_CKV, P_CQ, P_W = 0, 512, 768, 1024, 1152, 1408

LANES = 128
SUBLANES = 8
TOK_ROWS = D_MODEL // LANES
VMEM_LIMIT = 56 * 1024 * 1024


def _cparams(*sem):
    return pltpu.CompilerParams(dimension_semantics=sem, vmem_limit_bytes=VMEM_LIMIT)


def _rms(x):
    return lax.rsqrt(jnp.mean(x * x, axis=-1, keepdims=True) + EPS)


def _ada_kernel(c_ref, w_ref, b_ref, o_ref):
    ca = jax.nn.silu(c_ref[...])
    o_ref[...] = jnp.dot(ca, w_ref[...], preferred_element_type=F32, precision=lax.Precision.HIGHEST) + b_ref[...]


def _ada_call(c_all, ada_w, ada_b):
    depth, d, n = ada_w.shape
    rows = c_all.shape[0]
    tn = 1024
    return pl.pallas_call(
        _ada_kernel,
        grid=(depth, n // tn),
        in_specs=[
            pl.BlockSpec((rows, d), lambda i, j: (0, 0)),
            pl.BlockSpec((None, d, tn), lambda i, j: (i, 0, j)),
            pl.BlockSpec((None, 1, tn), lambda i, j: (i, 0, j)),
        ],
        out_specs=pl.BlockSpec((None, rows, tn), lambda i, j: (i, 0, j)),
        out_shape=jax.ShapeDtypeStruct((depth, rows, n), F32),
        compiler_params=_cparams("arbitrary", "arbitrary"),
        name="ada_mod",
    )(c_all, ada_w, ada_b.reshape(depth, 1, n))


INPROJ_SUB = 512


def _inproj_kernel(x_ref, mod_ref, n1g_ref, win_ref, gg_ref, wcat_ref, bfull_ref, qng_ref, wq_ref, kvng_ref,
                   wkv_ref, caq_ref, cbq_ref, cak_ref, cbk_ref,
                   ya_ref, zb_ref, zd_ref, q_ref, k_ref, v_ref, *, sub):
    tm = x_ref.shape[0]
    gain1 = n1g_ref[...] * (1.0 + mod_ref[1:2, :])
    sh1 = mod_ref[0:1, :]

    def project(rows):
        xt = x_ref[rows, :]
        h = xt * _rms(xt) * gain1 + sh1
        return jnp.dot(h.astype(BF16), win_ref[...], preferred_element_type=F32)

    grp = lax.broadcasted_iota(jnp.int32, (CHUNK, GROUP_W), 1) // SUB_W
    lane2 = lax.broadcasted_iota(jnp.int32, (sub, 2 * LANES), 1)
    lane = lax.broadcasted_iota(jnp.int32, (sub, LANES), 1)
    head_rows = lax.broadcasted_iota(jnp.int32, (LANES, LANES), 0) < QK_HEAD
    head_ones = jnp.where(head_rows, 1.0, 0.0).astype(BF16)

    def head_rms(blk):
        ssq = jnp.dot((blk * blk).astype(BF16), head_ones, preferred_element_type=F32)
        return lax.rsqrt(ssq / QK_HEAD + EPS)

    def mixers(p, r0):
        rows = slice(r0, r0 + sub)
        uv = jax.nn.gelu(p[:, P_A:P_A + 2 * GROUP_W])
        u = uv[:, :GROUP_W]
        vv = uv[:, GROUP_W:]
        vn = vv * _rms(vv) * gg_ref[...]
        for c in range(sub // CHUNK):
            crow = slice(c * CHUNK, (c + 1) * CHUNK)
            vc = vn[crow]
            vstack = jnp.concatenate([jnp.where(grp == g, vc, 0.0).astype(BF16) for g in range(N_GROUPS)], axis=0)
            mixed = jnp.dot(wcat_ref[...], vstack, preferred_element_type=F32) + bfull_ref[...]
            ya_ref[r0 + c * CHUNK:r0 + (c + 1) * CHUNK, :] = (u[crow] * mixed).astype(BF16)

        zb_ref[rows, :] = p[:, P_B:P_B + GROUP_W].astype(BF16)
        zd_ref[rows, :] = p[:, P_D:P_D + GROUP_W].astype(BF16)

        cq = p[:, P_CQ:P_CQ + 2 * LANES]
        cq_ss = jnp.sum(jnp.where(lane2 < Q_LORA, cq * cq, 0.0), axis=-1, keepdims=True)
        xq = cq * lax.rsqrt(cq_ss / Q_LORA + EPS) * qng_ref[...]
        qraw = jnp.dot(xq.astype(BF16), wq_ref[...], preferred_element_type=F32)
        ckv = p[:, P_CKV:P_CKV + KV_LORA]
        xkv = ckv * _rms(ckv) * kvng_ref[...]
        kvraw = jnp.dot(xkv.astype(BF16), wkv_ref[...], preferred_element_type=F32)
        krope = p[:, P_CQ + LANES:P_CQ + 2 * LANES]
        caq, cbq, cak, cbk = caq_ref[rows, :], cbq_ref[rows, :], cak_ref[rows, :], cbk_ref[rows, :]
        for hd in range(MLA_HEADS):
            cols = slice(hd * LANES, (hd + 1) * LANES)
            qb = qraw[:, cols]
            rq = head_rms(qb)
            q_ref[rows, cols] = ((qb * caq + pltpu.roll(qb, LANES - QK_ROPE, 1) * cbq) * rq).astype(BF16)
            kb = jnp.where(lane < QK_NOPE, kvraw[:, cols], krope)
            rk = head_rms(kb)
            k_ref[rows, cols] = ((kb * cak + pltpu.roll(kb, LANES - QK_ROPE, 1) * cbk) * rk).astype(BF16)
        v_ref[rows, :] = kvraw[:, MLA_HEADS * LANES:].astype(BF16)

    starts = range(0, tm, sub)
    ps = [project(slice(r0, r0 + sub)) for r0 in starts]
    for p, r0 in zip(ps, starts):
        mixers(p, r0)


def _inproj_call(x, mod, shared_mod, wts, tabs, tm):
    bs, ls, d = x.shape
    full = lambda a: pl.BlockSpec(a.shape, lambda b, j: (0,) * a.ndim)
    mod_map = (lambda b, j: (0, 0, 0)) if shared_mod else (lambda b, j: (b, 0, 0))
    tab_spec = pl.BlockSpec((tm, LANES), lambda b, j: (j, 0))
    tok = lambda w: pl.BlockSpec((None, tm, w), lambda b, j: (b, j, 0))
    widths = (GROUP_W, GROUP_W, GROUP_W, MLA_HEADS * LANES, MLA_HEADS * LANES, MLA_HEADS * V_HEAD)
    return pl.pallas_call(
        functools.partial(_inproj_kernel, sub=min(tm, INPROJ_SUB)),
        grid=(bs, ls // tm),
        in_specs=[tok(d), pl.BlockSpec((None, SUBLANES, d), mod_map)] + [full(w) for w in wts] + [tab_spec] * 4,
        out_specs=[tok(w) for w in widths],
        out_shape=[jax.ShapeDtypeStruct((bs, ls, w), BF16) for w in widths],
        compiler_params=_cparams("parallel", "parallel"),
        name="inproj",
    )(x, mod, *wts, *tabs)


SCORES_AHEAD = 1
KEY_CHUNK = 256


def _attn_kernel(*refs, n_kv, n_cast):
    q_ref = refs[0]
    k_refs = refs[1:1 + n_kv]
    v_refs = refs[1 + n_kv:1 + 2 * n_kv]
    cast_in = refs[1 + 2 * n_kv:1 + 2 * n_kv + n_cast]
    o_ref = refs[1 + 2 * n_kv + n_cast]
    cast_out = refs[2 + 2 * n_kv + n_cast:]
    for src, dst in zip(cast_in, cast_out):
        dst[...] = src[...].astype(BF16)
    tq = q_ref.shape[0]
    head_of_lane = lax.broadcasted_iota(jnp.int32, (tq, MLA_HEADS * V_HEAD), 1) // V_HEAD
    out = jnp.zeros((tq, MLA_HEADS * V_HEAD), F32)

    def scores(hd):
        cols = slice(hd * LANES, (hd + 1) * LANES)
        qh = q_ref[:, cols]
        return [lax.dot_general(qh, kr[:, cols], (((1,), (1,)), ((), ())), preferred_element_type=F32)
                for kr in k_refs]

    ahead = [scores(hd) for hd in range(min(SCORES_AHEAD, MLA_HEADS))]
    for hd in range(MLA_HEADS):
        ss = ahead.pop(0)
        if hd + SCORES_AHEAD < MLA_HEADS:
            ahead.append(scores(hd + SCORES_AHEAD))
        m = functools.reduce(jnp.maximum, [jnp.max(s, axis=-1, keepdims=True) for s in ss])
        den = jnp.zeros((tq, 1), F32)
        o = jnp.zeros((tq, MLA_HEADS * V_HEAD), F32)
        for s, vr in zip(ss, v_refs):
            for k0 in range(0, s.shape[1], KEY_CHUNK):
                keys = slice(k0, min(k0 + KEY_CHUNK, s.shape[1]))
                pp = jnp.exp2(s[:, keys] - m)
                den = den + jnp.sum(pp, axis=-1, keepdims=True)
                o = o + jnp.dot(pp.astype(BF16), vr[keys, :], preferred_element_type=F32)
        out = jnp.where(head_of_lane == hd, o / den, out)
    o_ref[...] = out.astype(BF16)


BF16_ROWS = 16


def _cast_blocks(rows, steps):
    n = steps
    while rows % n or (rows // n) % BF16_ROWS:
        n //= 2
    return n


def _attn_call(q, ks, vs, tq, cast=()):
    bs, lq, _ = q.shape
    n_kv = len(ks)
    nj = lq // tq
    steps = bs * nj
    kv_spec = lambda a: pl.BlockSpec((None,) + a.shape[1:], lambda b, j: (b, 0, 0))

    def cast_spec(a):
        n = _cast_blocks(a.shape[0], steps)
        return pl.BlockSpec((a.shape[0] // n, a.shape[1]), lambda b, j: ((b * nj + j) * n // steps, 0))

    out = pl.pallas_call(
        functools.partial(_attn_kernel, n_kv=n_kv, n_cast=len(cast)),
        grid=(bs, nj),
        in_specs=[pl.BlockSpec((None, tq, q.shape[2]), lambda b, j: (b, j, 0))]
        + [kv_spec(a) for a in ks] + [kv_spec(a) for a in vs] + [cast_spec(a) for a in cast],
        out_specs=[pl.BlockSpec((None, tq, MLA_HEADS * V_HEAD), lambda b, j: (b, j, 0))]
        + [cast_spec(a) for a in cast],
        out_shape=[jax.ShapeDtypeStruct((bs, lq, MLA_HEADS * V_HEAD), BF16)]
        + [jax.ShapeDtypeStruct(a.shape, BF16) for a in cast],
        compiler_params=_cparams("arbitrary", "arbitrary"),
        name="attention",
    )(q, *ks, *vs, *cast)
    return out if cast else out[0]


FOURIER_SUB = 512


def _fourier_kernel(cs_ref, z_ref, cc_ref, scn_ref, wf_ref, o_ref):
    tr = o_ref.shape[0]
    sub = min(tr, FOURIER_SUB)
    starts = range(0, tr, sub)
    tcs = [(jnp.dot(cs_ref[r0:r0 + sub, :], z_ref[...], preferred_element_type=F32),
            jnp.dot(cs_ref[tr + r0:tr + r0 + sub, :], z_ref[...], preferred_element_type=F32))
           for r0 in starts]
    for (tc, ts), r0 in zip(tcs, starts):
        y = (jnp.dot(tc.astype(BF16), cc_ref[...], preferred_element_type=F32)
             + jnp.dot(ts.astype(BF16), scn_ref[...], preferred_element_type=F32))
        o_ref[r0:r0 + sub, :] = jnp.dot(y.astype(BF16), wf_ref[...], preferred_element_type=F32).astype(BF16)


def _fourier_call(zd, cs, cc, scn, wf, tr):
    bs, ls, w = zd.shape
    full = lambda a: pl.BlockSpec(a.shape, lambda r, b: (0,) * a.ndim)
    return pl.pallas_call(
        _fourier_kernel,
        grid=(ls // tr, bs),
        in_specs=[pl.BlockSpec((None, 2 * tr, ls), lambda r, b: (r, 0, 0)),
                  pl.BlockSpec((None, ls, w), lambda r, b: (b, 0, 0)), full(cc), full(scn), full(wf)],
        out_specs=pl.BlockSpec((None, tr, w), lambda r, b: (b, r, 0)),
        out_shape=jax.ShapeDtypeStruct((bs, ls, w), BF16),
        compiler_params=_cparams("parallel", "arbitrary"),
        name="fourier",
    )(cs, zd, cc, scn, wf)


MERGE_SUB = 512


def _merge_kernel(*refs, seq_len, tm, sub, tok_layout, router):
    (x_ref, ya_ref, zb_ref, zbp_ref, zbn_ref, at_ref, yd_ref, mod_ref, pw_ref, ps_ref, wo_ref, n2g_ref) = refs[:12]
    rest = refs[12:]
    if router:
        rw_ref, rest = rest[0], rest[1:]
    xn_ref, h_ref = rest[0], rest[1]
    j = pl.program_id(1)
    nj = pl.num_programs(1)

    starts = range(0, tm, sub)
    parts = []
    for r0 in starts:
        rows = slice(r0, r0 + sub)
        part = jnp.dot(ya_ref[rows, :], wo_ref[0], preferred_element_type=F32)
        part += jnp.dot(at_ref[rows, :], wo_ref[2], preferred_element_type=F32)
        part += jnp.dot(yd_ref[rows, :], wo_ref[3], preferred_element_type=F32)
        parts.append(part)

    zm = zb_ref[...].astype(F32)
    zp = jnp.where(j > 0, zbp_ref[...].astype(F32), 0.0)
    zn = jnp.where(j < nj - 1, zbn_ref[...].astype(F32), 0.0)
    ext = jnp.concatenate([zp, zm, zn], axis=0)
    n = tm + 2 * POOL_HALO

    def ahead(a, k):
        return pltpu.roll(a, n - k, 0)

    tg = j * tm + lax.broadcasted_iota(jnp.int32, (tm, 1), 0)
    low_group = lax.broadcasted_iota(jnp.int32, (tm, LANES), 1) < SUB_W

    def window_mean(d, w):
        cnt = jnp.minimum(tg - w // 2 + w, seq_len) - jnp.maximum(tg - w // 2, 0)
        return ahead(d, POOL_HALO - w // 2)[:tm] * (1.0 / cnt.astype(F32))

    halves = []
    for half in range(GROUP_W // LANES):
        w_lo, w_hi = POOL_WINDOWS[2 * half], POOL_WINDOWS[2 * half + 1]
        d, width, means = ext[:, half * LANES:(half + 1) * LANES], 1, {}
        while width < w_hi:
            d = d + ahead(d, width)
            width *= 2
            if width in (w_lo, w_hi):
                means[width] = window_mean(d, width)
        halves.append(jnp.where(low_group, means[w_lo], means[w_hi]))
    diff = jnp.concatenate(halves, axis=1) - zm
    yb = jnp.dot(diff.astype(BF16), pw_ref[...], preferred_element_type=F32) * ps_ref[...]

    yb = yb.astype(BF16)
    g1 = mod_ref[2:3, :]
    sh2 = mod_ref[3:4, :]
    gain2 = n2g_ref[...] * (1.0 + mod_ref[4:5, :])

    def project(part, rows):
        acc = part + jnp.dot(yb[rows], wo_ref[1], preferred_element_type=F32)
        return x_ref[rows, :] + g1 * acc

    def modulate(xn, r0):
        rows = slice(r0, r0 + sub)
        xn_ref[rows, :] = xn
        h2 = xn * _rms(xn) * gain2 + sh2
        if tok_layout:
            for jj in range(TOK_ROWS):
                h_ref[pl.ds(r0 * TOK_ROWS + jj, sub, stride=TOK_ROWS), :] = h2[:, jj * LANES:(jj + 1) * LANES]
        else:
            h_ref[rows, :] = h2.astype(BF16)
        if router:
            lg_ref = rest[2]
            h_hi = h2.astype(BF16)
            h_lo = (h2 - h_hi.astype(F32)).astype(BF16)
            both = jnp.dot(h_hi, rw_ref[...], preferred_element_type=F32)
            lg_ref[rows, :] = both[:, :LANES] + (jnp.dot(h_lo, rw_ref[:, :LANES], preferred_element_type=F32)
                                                 + both[:, LANES:])

    xns = [project(part, slice(r0, r0 + sub)) for part, r0 in zip(parts, starts)]
    for xn, r0 in zip(xns, starts):
        modulate(xn, r0)


def _merge_call(x, ya, zb, attn, yd, mod, shared_mod, wts, tm, tok_layout, router_w=None):
    bs, ls, d = x.shape
    nj = ls // tm
    hb = tm // POOL_HALO
    full = lambda a: pl.BlockSpec(a.shape, lambda b, j: (0,) * a.ndim)
    mod_map = (lambda b, j: (0, 0, 0)) if shared_mod else (lambda b, j: (b, 0, 0))
    tok = lambda w: pl.BlockSpec((None, tm, w), lambda b, j: (b, j, 0))
    in_specs = [
        tok(d), tok(GROUP_W), tok(GROUP_W),
        pl.BlockSpec((None, POOL_HALO, GROUP_W), lambda b, j: (b, jnp.maximum(j * hb - 1, 0), 0)),
        pl.BlockSpec((None, POOL_HALO, GROUP_W), lambda b, j: (b, jnp.minimum((j + 1) * hb, nj * hb - 1), 0)),
        tok(GROUP_W), tok(GROUP_W),
        pl.BlockSpec((None, SUBLANES, d), mod_map),
    ] + [full(w) for w in wts]
    args = [x, ya, zb, zb, zb, attn, yd, mod, *wts]
    out_specs = [tok(d)]
    out_shape = [jax.ShapeDtypeStruct((bs, ls, d), F32)]
    if tok_layout:
        out_specs.append(pl.BlockSpec((tm * TOK_ROWS, LANES), lambda b, j: (b * nj + j, 0)))
        out_shape.append(jax.ShapeDtypeStruct((bs * ls * TOK_ROWS, LANES), F32))
    else:
        out_specs.append(tok(d))
        out_shape.append(jax.ShapeDtypeStruct((bs, ls, d), BF16))
    if router_w is not None:
        in_specs.append(full(router_w))
        args.append(router_w)
        out_specs.append(tok(LANES))
        out_shape.append(jax.ShapeDtypeStruct((bs, ls, LANES), F32))
    return pl.pallas_call(
        functools.partial(_merge_kernel, seq_len=ls, tm=tm, sub=min(tm, MERGE_SUB), tok_layout=tok_layout,
                          router=router_w is not None),
        grid=(bs, nj),
        in_specs=in_specs,
        out_specs=out_specs,
        out_shape=out_shape,
        compiler_params=_cparams("parallel", "parallel"),
        name="merge",
    )(*args)


ACC_COLS = 256
FFN_SUB = 512


def _ffn_kernel(te_ref, tv_ref, *refs, tm, tok_in, tok_out, residual, row_options):
    x_ref, wg_ref, wu_ref, wd_ref = refs[:4]
    rest = refs[4:]
    if residual:
        res_ref, mod_ref, rest = rest[0], rest[1], rest[2:]
    o_ref, acc_ref = rest[0], rest[1]
    i = pl.program_id(0)
    f = pl.program_id(1)
    nf = pl.num_programs(1)

    def store(y, r0, rows):
        if tok_out:
            for jj in range(TOK_ROWS):
                o_ref[pl.ds(r0 * TOK_ROWS + jj, rows, stride=TOK_ROWS), :] = y[:, jj * LANES:(jj + 1) * LANES]
        else:
            o_ref[r0:r0 + rows, :] = y

    def compute(m):
        @pl.when(f == 0)
        def _():
            acc_ref[:m, :] = jnp.zeros((m, D_MODEL), F32)
            if tok_in:
                for jj in range(TOK_ROWS):
                    rest[2][:m, jj * LANES:(jj + 1) * LANES] = x_ref[pl.ds(jj, m, stride=TOK_ROWS), :].astype(BF16)

        xb = rest[2][:m, :] if tok_in else x_ref[:m, :]
        tf = wg_ref.shape[1]
        acts = []
        for lo in range(0, tf, FFN_SUB):
            sub = slice(lo, min(lo + FFN_SUB, tf))
            gate = jnp.dot(xb, wg_ref[:, sub], preferred_element_type=F32)
            up = jnp.dot(xb, wu_ref[:, sub], preferred_element_type=F32)
            acts.append((jax.nn.silu(gate) * up).astype(BF16))
        act = jnp.concatenate(acts, axis=1)
        for cb in range(D_MODEL // ACC_COLS):
            cols = slice(cb * ACC_COLS, (cb + 1) * ACC_COLS)
            acc_ref[:m, cols] += jnp.dot(act, wd_ref[:, cols], preferred_element_type=F32)

        @pl.when(f == nf - 1)
        def _():
            y = acc_ref[:m, :]
            if residual:
                y = res_ref[:m, :] + mod_ref[5:6, :] * y
            store(y, 0, m)
            if m < tm:
                store(jnp.zeros((tm - m, D_MODEL), F32), m, tm - m)

    for m in row_options:
        pl.when(tv_ref[i] == m)(functools.partial(compute, m))

    @pl.when(jnp.logical_and(tv_ref[i] == 0, f == nf - 1))
    def _():
        store(jnp.zeros((tm, D_MODEL), F32), 0, tm)


def _ffn_call(x, tile_e, tile_v, wg, wu, wd, tm, tf, tok_in, tok_out, res=None, mod=None, tiles_per_mod=None,
              row_options=None):
    n_tiles = tile_e.shape[0]
    d, dff = wg.shape[1], wg.shape[2]
    nf = dff // tf
    last = nf - 1
    fsel = lambda i, f, te, tv: jnp.where(tv[i] > 0, f, last)
    xsel = lambda i, f, te, tv: (jnp.where(tv[i] > 0, i, 0), 0)
    x_spec = pl.BlockSpec((tm * TOK_ROWS, LANES), xsel) if tok_in else pl.BlockSpec((tm, d), xsel)
    in_specs = [
        x_spec,
        pl.BlockSpec((None, d, tf), lambda i, f, te, tv: (te[i], 0, fsel(i, f, te, tv))),
        pl.BlockSpec((None, d, tf), lambda i, f, te, tv: (te[i], 0, fsel(i, f, te, tv))),
        pl.BlockSpec((None, tf, d), lambda i, f, te, tv: (te[i], fsel(i, f, te, tv), 0)),
    ]
    args = [x, wg, wu, wd]
    residual = res is not None
    if residual:
        in_specs.append(pl.BlockSpec((tm, d), lambda i, f, te, tv: (i, 0)))
        in_specs.append(pl.BlockSpec((None, SUBLANES, d), lambda i, f, te, tv: (i // tiles_per_mod, 0, 0)))
        args += [res, mod]
    if tok_out:
        out_spec = pl.BlockSpec((tm * TOK_ROWS, LANES), lambda i, f, te, tv: (i, 0))
        out_shape = jax.ShapeDtypeStruct((n_tiles * tm * TOK_ROWS, LANES), F32)
    else:
        out_spec = pl.BlockSpec((tm, d), lambda i, f, te, tv: (i, 0))
        out_shape = jax.ShapeDtypeStruct((n_tiles * tm, d), F32)
    scratch = [pltpu.VMEM((tm, d), F32)]
    if tok_in:
        scratch.append(pltpu.VMEM((tm, d), BF16))
    return pl.pallas_call(
        functools.partial(_ffn_kernel, tm=tm, tok_in=tok_in, tok_out=tok_out, residual=residual,
                          row_options=row_options or (tm,)),
        grid_spec=pltpu.PrefetchScalarGridSpec(
            num_scalar_prefetch=2, grid=(n_tiles, nf), in_specs=in_specs, out_specs=out_spec,
            scratch_shapes=scratch),
        out_shape=out_shape,
        compiler_params=_cparams("parallel", "arbitrary"),
        name="ffn",
    )(tile_e, tile_v, *args)


def _route_kernel(lg_ref, o_ref, cnt_ref, carry_ref):
    tm = lg_ref.shape[0]
    i = pl.program_id(0)

    @pl.when(i == 0)
    def _():
        carry_ref[...] = jnp.zeros_like(carry_ref)

    lane_i = lax.broadcasted_iota(jnp.int32, (tm, LANES), 1)
    lane = lane_i.astype(F32)
    neg = jnp.float32(-jnp.inf)
    lg = jnp.where(lane_i < N_EXPERTS, lg_ref[...], neg)
    m1 = jnp.max(lg, axis=-1, keepdims=True)
    i1 = jnp.min(jnp.where(lg == m1, lane, float(LANES)), axis=-1, keepdims=True)
    lg2 = jnp.where(lane == i1, neg, lg)
    m2 = jnp.max(lg2, axis=-1, keepdims=True)
    i2 = jnp.min(jnp.where(lg2 == m2, lane, float(LANES)), axis=-1, keepdims=True)
    e2 = jnp.exp(m2 - m1)
    w1 = 1.0 / (1.0 + e2)
    w2 = e2 / (1.0 + e2)
    hit = jnp.logical_or(lane == i1, lane == i2)
    onehot = jnp.where(hit, 1.0, 0.0).astype(BF16)
    r = lax.broadcasted_iota(jnp.int32, (tm, tm), 0)
    c = lax.broadcasted_iota(jnp.int32, (tm, tm), 1)
    before = jnp.where(c < r, 1.0, 0.0).astype(BF16)
    carry = carry_ref[0:1, :]
    cum = jnp.dot(before, onehot, preferred_element_type=F32) + carry
    rank1 = jnp.sum(jnp.where(lane == i1, cum, 0.0), axis=-1, keepdims=True)
    rank2 = jnp.sum(jnp.where(lane == i2, cum, 0.0), axis=-1, keepdims=True)
    total = carry + jnp.sum(onehot.astype(F32), axis=0, keepdims=True)
    carry_ref[...] = jnp.broadcast_to(total, carry_ref.shape)
    cnt_ref[...] = jnp.broadcast_to(total, cnt_ref.shape)
    out = jnp.zeros((tm, LANES), F32)
    for col, val in enumerate((i1, i2, w1, w2, rank1, rank2)):
        out = jnp.where(lane_i == col, val, out)
    o_ref[...] = out


def _route_call(logits, tm):
    n = logits.shape[0]
    return pl.pallas_call(
        _route_kernel,
        grid=(n // tm,),
        in_specs=[pl.BlockSpec((tm, LANES), lambda i: (i, 0))],
        out_specs=[pl.BlockSpec((tm, LANES), lambda i: (i, 0)), pl.BlockSpec((SUBLANES, LANES), lambda i: (0, 0))],
        out_shape=[jax.ShapeDtypeStruct((n, LANES), F32), jax.ShapeDtypeStruct((SUBLANES, LANES), F32)],
        scratch_shapes=[pltpu.VMEM((SUBLANES, LANES), F32)],
        compiler_params=_cparams("arbitrary"),
        name="route",
    )(logits)


ISSUE_UNROLL = 8


def _token_rows(tok):
    start = tok * TOK_ROWS
    return pl.ds(start if isinstance(start, int) else pl.multiple_of(start, TOK_ROWS), TOK_ROWS)


def _token_copy(src, src_tok, dst, dst_tok, sem):
    return pltpu.make_async_copy(src.at[_token_rows(src_tok), :], dst.at[_token_rows(dst_tok), :], sem)


def _for_each_token(n_tok, body):
    def group(g, carry):
        for u in range(ISSUE_UNROLL):
            body(g * ISSUE_UNROLL + u)
        return carry

    lax.fori_loop(0, n_tok // ISSUE_UNROLL, group, 0)


def _dispatch_kernel(pad_ref, pos_ref, h_ref, xs_hbm, zbuf, sem, zsem, *, td, pad_bits, max_tail):
    i = pl.program_id(0)

    def pad_copies():
        for e in range(N_EXPERTS):
            first, length = pad_ref[e], pad_ref[N_EXPERTS + e]
            for b in range(pad_bits):
                size = 1 << b
                tok = first + jnp.bitwise_and(length, size - 1)
                copy = pltpu.make_async_copy(
                    zbuf.at[pl.ds(0, size * TOK_ROWS), :],
                    xs_hbm.at[pl.ds(pl.multiple_of(tok * TOK_ROWS, TOK_ROWS), size * TOK_ROWS), :], zsem)
                yield jnp.bitwise_and(length, size) != 0, copy
        first, pieces = pad_ref[2 * N_EXPERTS], pad_ref[2 * N_EXPERTS + 1]
        piece = zbuf.shape[0]
        for k in range(max_tail):
            copy = pltpu.make_async_copy(
                zbuf, xs_hbm.at[pl.ds(pl.multiple_of(first * TOK_ROWS + k * piece, TOK_ROWS), piece), :], zsem)
            yield k < pieces, copy

    @pl.when(i == 0)
    def _():
        zbuf[...] = jnp.zeros_like(zbuf)
        for needed, copy in pad_copies():
            pl.when(needed)(copy.start)

    def issue(t):
        for s in range(TOP_K):
            _token_copy(h_ref, t, xs_hbm, pos_ref[0, TOP_K * t + s], sem).start(priority=s % 2)

    _for_each_token(td, issue)
    for _ in range(TOP_K):
        pltpu.make_async_copy(h_ref, xs_hbm.at[pl.ds(0, td * TOK_ROWS), :], sem).wait()

    @pl.when(i == 0)
    def _():
        for needed, copy in pad_copies():
            pl.when(needed)(copy.wait)


def _dispatch_call(pos, pad_info, h_tok, n_slots, td, tile):
    n = pos.shape[0]
    pad_bits = tile.bit_length() - 1
    return pl.pallas_call(
        functools.partial(_dispatch_kernel, td=td, pad_bits=pad_bits, max_tail=2 * N_EXPERTS),
        grid_spec=pltpu.PrefetchScalarGridSpec(
            num_scalar_prefetch=1, grid=(n // td,),
            in_specs=[pl.BlockSpec((None, 1, TOP_K * td), lambda i, pad: (i, 0, 0), memory_space=pltpu.SMEM),
                      pl.BlockSpec((td * TOK_ROWS, LANES), lambda i, pad: (i, 0))],
            out_specs=pl.BlockSpec(memory_space=pl.ANY),
            scratch_shapes=[pltpu.VMEM((tile // 2 * TOK_ROWS, LANES), F32), pltpu.SemaphoreType.DMA,
                            pltpu.SemaphoreType.DMA]),
        out_shape=jax.ShapeDtypeStruct((n_slots * TOK_ROWS, LANES), F32),
        compiler_params=_cparams("arbitrary"),
        name="dispatch",
    )(pad_info, pos.reshape(n // td, 1, TOP_K * td), h_tok)


COMBINE_AHEAD = 2


def _combine_kernel(*refs, tc):
    pos_refs = refs[:COMBINE_AHEAD + 1]
    y_hbm, xn_ref, route_ref, mod_ref, o_ref = refs[COMBINE_AHEAD + 1:COMBINE_AHEAD + 6]
    bufs = refs[COMBINE_AHEAD + 6:-1]
    sems = refs[-1]
    n_buf = len(bufs)
    i = pl.program_id(0)
    n = pl.num_programs(0)

    def issue(p_ref, slot, t):
        for s in range(TOP_K):
            _token_copy(y_hbm, p_ref[0, TOP_K * t + s], bufs[slot], TOP_K * t + s,
                        sems.at[slot]).start(priority=s % 2)

    def blend(slot):
        buf = bufs[slot]
        g2 = mod_ref[5:6, :]
        w1 = route_ref[:, 2:3]
        w2 = route_ref[:, 3:4]
        for jj in range(TOK_ROWS):
            cols = slice(jj * LANES, (jj + 1) * LANES)
            ya = buf[pl.ds(jj, tc, stride=TOP_K * TOK_ROWS), :]
            yb = buf[pl.ds(TOK_ROWS + jj, tc, stride=TOP_K * TOK_ROWS), :]
            o_ref[:, cols] = xn_ref[:, cols] + g2[:, cols] * (w1 * ya + w2 * yb)

    @pl.when(i == 0)
    def _():
        for ahead in range(COMBINE_AHEAD):
            @pl.when(ahead < n)
            def _(ahead=ahead):
                _for_each_token(tc, functools.partial(issue, pos_refs[ahead], ahead))

    for slot in range(n_buf):
        @pl.when(i % n_buf == slot)
        def _(slot=slot):
            pltpu.make_async_copy(y_hbm.at[pl.ds(0, TOP_K * tc * TOK_ROWS), :], bufs[slot], sems.at[slot]).wait()

            @pl.when(i + COMBINE_AHEAD < n)
            def _():
                for t in range(tc):
                    issue(pos_refs[COMBINE_AHEAD], (slot + COMBINE_AHEAD) % n_buf, t)
                blend(slot)

            @pl.when(i + COMBINE_AHEAD >= n)
            def _():
                blend(slot)


def _combine_call(pos, y_tok, xn, route, mod, seq_len, tc):
    n, d = xn.shape
    steps = n // tc
    pos3 = pos.reshape(steps, 1, TOP_K * tc)
    pos_spec = lambda k: pl.BlockSpec((None, 1, TOP_K * tc), lambda i: (jnp.minimum(i + k, steps - 1), 0, 0),
                                      memory_space=pltpu.SMEM)
    n_buf = COMBINE_AHEAD + 1
    return pl.pallas_call(
        functools.partial(_combine_kernel, tc=tc),
        grid=(steps,),
        in_specs=[pos_spec(k) for k in range(n_buf)]
        + [pl.BlockSpec(memory_space=pl.ANY),
           pl.BlockSpec((tc, d), lambda i: (i, 0)),
           pl.BlockSpec((tc, LANES), lambda i: (i, 0)),
           pl.BlockSpec((None, SUBLANES, d), lambda i: (i * tc // seq_len, 0, 0))],
        out_specs=pl.BlockSpec((tc, d), lambda i: (i, 0)),
        out_shape=jax.ShapeDtypeStruct((n, d), F32),
        scratch_shapes=[pltpu.VMEM((TOP_K * tc * TOK_ROWS, LANES), F32)] * n_buf
        + [pltpu.SemaphoreType.DMA((n_buf,))],
        compiler_params=_cparams("arbitrary"),
        name="combine",
    )(*([pos3] * n_buf), y_tok, xn, route, mod)


def _rot(w):
    return jnp.concatenate([-w[..., 8:16], w[..., 0:8], -w[..., 24:32], w[..., 16:24]], axis=-1)


def _swap(g):
    return jnp.concatenate([g[..., 8:16], g[..., 0:8], g[..., 24:32], g[..., 16:24]], axis=-1)


def _rope_angles(length):
    rows = length // GRID_W
    row = jnp.repeat(jnp.arange(rows, dtype=F32), GRID_W)
    col = jnp.tile(jnp.arange(GRID_W, dtype=F32), rows)
    n_freq = QK_ROPE // 4
    inv_freq = ROPE_BASE ** (-jnp.arange(n_freq, dtype=F32) / n_freq)
    ar = row[:, None] * inv_freq
    ac = col[:, None] * inv_freq
    cos = jnp.concatenate([jnp.cos(ar), jnp.cos(ar), jnp.cos(ac), jnp.cos(ac)], axis=1)
    sin = jnp.concatenate([jnp.sin(ar), jnp.sin(ar), jnp.sin(ac), jnp.sin(ac)], axis=1)
    return cos, sin


def _rope_tables(gain, length, cos, sin, scale):
    g_nope = jnp.broadcast_to(gain[:QK_NOPE], (length, QK_NOPE))
    g_rope = gain[QK_NOPE:]
    pad = jnp.zeros((length, LANES - QK_HEAD), F32)
    ca = jnp.concatenate([g_nope, g_rope * cos, pad], axis=1) * scale
    cb = jnp.concatenate([jnp.zeros((length, QK_NOPE), F32), _swap(g_rope) * sin, pad], axis=1) * scale
    return ca, cb


def _dft_table_kernel(ca_ref, sa_ref, cb_ref, sb_ref, o_ref):
    tr = ca_ref.shape[0]
    first = lax.broadcasted_iota(jnp.int32, (tr, LANES), 1) < GRID_W
    cb, sb = cb_ref[...], sb_ref[...]
    per_tile = LANES // GRID_W
    for t in range(ca_ref.shape[1] // per_tile):
        ca = jnp.where(first, ca_ref[:, per_tile * t:per_tile * t + 1], ca_ref[:, per_tile * t + 1:per_tile * t + 2])
        sa = jnp.where(first, sa_ref[:, per_tile * t:per_tile * t + 1], sa_ref[:, per_tile * t + 1:per_tile * t + 2])
        cols = slice(t * LANES, (t + 1) * LANES)
        o_ref[:tr, cols] = (ca * cb - sa * sb).astype(BF16)
        o_ref[tr:, cols] = (sa * cb + ca * sb).astype(BF16)


def _dft_tables(length, tr):
    k = jnp.arange(length, dtype=jnp.int32)[:, None]
    n1 = jnp.arange(length // GRID_W, dtype=jnp.int32)[None, :]
    n2 = jnp.arange(GRID_W, dtype=jnp.int32)[None, :]
    coarse = ((k * n1 * GRID_W) % length).astype(F32) * (2.0 * jnp.pi / length)
    fine = ((k * n2) % length).astype(F32) * (2.0 * jnp.pi / length)
    norm = length ** -0.5
    fine2 = jnp.concatenate([fine, fine], axis=1)
    cs = pl.pallas_call(
        _dft_table_kernel,
        grid=(length // tr,),
        in_specs=[pl.BlockSpec((tr, length // GRID_W), lambda r: (r, 0))] * 2
        + [pl.BlockSpec((tr, LANES), lambda r: (r, 0))] * 2,
        out_specs=pl.BlockSpec((None, 2 * tr, length), lambda r: (r, 0, 0)),
        out_shape=jax.ShapeDtypeStruct((length // tr, 2 * tr, length), BF16),
        compiler_params=_cparams("parallel"),
        name="dft_table",
    )(jnp.cos(coarse) * norm, jnp.sin(coarse) * norm, jnp.cos(fine2), jnp.sin(fine2))
    kc = jnp.arange(SUB_W, dtype=jnp.int32)
    angc = ((kc[:, None] * kc[None, :]) % SUB_W).astype(F32) * (2.0 * jnp.pi / SUB_W)
    eye = jnp.eye(N_GROUPS, dtype=F32)
    cc = jnp.kron(eye, jnp.cos(angc) * SUB_W ** -0.5).astype(BF16)
    scn = jnp.kron(eye, -jnp.sin(angc) * SUB_W ** -0.5).astype(BF16)
    return cs, cc, scn


def _layer_weights(i, w_in, gmlp_norm_g, spatial_w, spatial_b, pool_w, pool_scale, q_norm_g, w_uq, kv_norm_g,
                   w_ukv, fourier_w, w_out, norm1_g, norm2_g):
    wi = w_in[i]
    ckr = wi[:, OFF_CKR:OFF_D]
    win = jnp.concatenate([wi[:, OFF_A:OFF_B], wi[:, OFF_B:OFF_CQ], wi[:, OFF_D:IN_W], wi[:, OFF_CKV:OFF_CKR],
                           wi[:, OFF_CQ:OFF_CKV], ckr, _rot(ckr)], axis=1).astype(BF16)
    wcat = spatial_w[i].transpose(1, 0, 2).reshape(CHUNK, N_GROUPS * CHUNK).astype(BF16)
    bfull = jnp.repeat(spatial_b[i].T, SUB_W, axis=1)
    qng = jnp.concatenate([q_norm_g[i], jnp.zeros((2 * LANES - Q_LORA,), F32)])[None]
    wq4 = w_uq[i].reshape(Q_LORA, MLA_HEADS, QK_HEAD)
    wq = jnp.concatenate([wq4, _rot(wq4[..., QK_NOPE:])], axis=-1).reshape(Q_LORA, MLA_HEADS * LANES)
    wq = jnp.concatenate([wq, jnp.zeros((2 * LANES - Q_LORA, MLA_HEADS * LANES), F32)], axis=0).astype(BF16)
    wkv4 = w_ukv[i].reshape(KV_LORA, MLA_HEADS, QK_NOPE + V_HEAD)
    wk = jnp.concatenate([wkv4[..., :QK_NOPE], jnp.zeros((KV_LORA, MLA_HEADS, LANES - QK_NOPE), F32)], axis=-1)
    wkv = jnp.concatenate([wk.reshape(KV_LORA, MLA_HEADS * LANES),
                           wkv4[..., QK_NOPE:].reshape(KV_LORA, MLA_HEADS * V_HEAD)], axis=1).astype(BF16)
    in_w = (norm1_g[i][None], win, gmlp_norm_g[i][None], wcat, bfull, qng, wq, kv_norm_g[i][None], wkv)
    pw = jnp.zeros((GROUP_W, GROUP_W), F32)
    for g in range(N_GROUPS):
        pw = pw.at[g * SUB_W:(g + 1) * SUB_W, g * SUB_W:(g + 1) * SUB_W].set(pool_w[i, g])
    merge_w = (pw.astype(BF16), pool_scale[i][None], w_out[i].reshape(N_GROUPS, GROUP_W, D_MODEL).astype(BF16),
               norm2_g[i][None])
    return in_w, merge_w, fourier_w[i].astype(BF16)


def _pad_mod(m):
    m = m.reshape(m.shape[:-1] + (N_ADA, D_MODEL))
    return jnp.concatenate([m, jnp.zeros(m.shape[:-2] + (SUBLANES - N_ADA, D_MODEL), F32)], axis=-2)


def _tiles(seq_len):
    return dict(tin=min(seq_len, 1024), tmg=min(seq_len, 1024), tq=min(seq_len, 512), tr=min(seq_len, 1024))


def kernel(x, c, ctx, c_ctx, ada_w, ada_b, norm1_g, w_in, gmlp_norm_g, spatial_w, spatial_b, pool_w, pool_scale,
           q_norm_g, w_uq, kv_norm_g, w_ukv, qk_q_g, qk_k_g, fourier_w, w_out, norm2_g, ffn_w_gate, ffn_w_up,
           ffn_w_down, router_w, moe_w_gate, moe_w_up, moe_w_down):
    bsz, seq, d = x.shape
    lc = ctx.shape[1]
    depth = ada_w.shape[0]
    n_tok = bsz * seq
    tl, tcx = _tiles(seq), _tiles(lc)
    ffn_tm, ffn_tmc, ffn_tf = min(seq, 1024), min(bsz * lc, 1024), 1792
    moe_tm = 1024

    c_all = jnp.concatenate([c, c_ctx[None], jnp.zeros((SUBLANES - 1, d), F32)], axis=0)
    mods = _ada_call(c_all, ada_w, ada_b)
    cos, sin = _rope_angles(seq)
    ones, zeros = jnp.ones((lc, QK_ROPE), F32), jnp.zeros((lc, QK_ROPE), F32)
    dft_l = _dft_tables(seq, tl["tr"])
    dft_c = _dft_tables(lc, tcx["tr"])

    xc = ctx
    for i in range(depth):
        last = i == depth - 1
        moe = i % 2 == 1
        mod_l = _pad_mod(mods[i, :bsz])
        mod_c = _pad_mod(mods[i, bsz:bsz + 1])
        in_w, merge_w, wf = _layer_weights(i, w_in, gmlp_norm_g, spatial_w, spatial_b, pool_w, pool_scale,
                                           q_norm_g, w_uq, kv_norm_g, w_ukv, fourier_w, w_out, norm1_g, norm2_g)
        scale = QK_HEAD ** -0.5 * LOG2_E
        tabs_l = _rope_tables(qk_q_g[i], seq, cos, sin, scale) + _rope_tables(qk_k_g[i], seq, cos, sin, 1.0)
        tabs_c = _rope_tables(qk_q_g[i], lc, ones, zeros, scale) + _rope_tables(qk_k_g[i], lc, ones, zeros, 1.0)

        ya, zb, zd, q, k, v = _inproj_call(x, mod_l, False, in_w, tabs_l, tl["tin"])
        yac, zbc, zdc, qc, kc, vc = _inproj_call(xc, mod_c, True, in_w, tabs_c, tcx["tin"])
        ffn_w = ((moe_w_gate, moe_w_up, moe_w_down) if moe else (ffn_w_gate, ffn_w_up, ffn_w_down))
        ffn_w = [w[i // 2] for w in ffn_w]
        attn, *ffn_w16 = _attn_call(q, [kc, k], [vc, v], tl["tq"],
                                    cast=[w.reshape(-1, w.shape[-1]) for w in ffn_w])
        wg, wu, wd = [w16.reshape((-1,) + w.shape[-2:]) for w16, w in zip(ffn_w16, ffn_w)]
        yd = _fourier_call(zd, *dft_l, wf, tl["tr"])

        if not moe:
            j = i // 2
            xn, h2 = _merge_call(x, ya, zb, attn, yd, mod_l, False, merge_w, tl["tmg"], False)
            n_tiles = n_tok // ffn_tm
            x = _ffn_call(h2.reshape(n_tok, d), jnp.zeros((n_tiles,), jnp.int32),
                          jnp.full((n_tiles,), ffn_tm, jnp.int32),
                          wg, wu, wd, ffn_tm, ffn_tf, False, False, res=xn.reshape(n_tok, d), mod=mod_l,
                          tiles_per_mod=seq // ffn_tm).reshape(bsz, seq, d)
            if not last:
                attn_c = _attn_call(qc, [kc], [vc], tcx["tq"])
                ydc = _fourier_call(zdc, *dft_c, wf, tcx["tr"])
                xnc, h2c = _merge_call(xc, yac, zbc, attn_c, ydc, mod_c, True, merge_w, tcx["tmg"], False)
                nc_tok = bsz * lc
                nct = nc_tok // ffn_tmc
                xc = _ffn_call(h2c.reshape(nc_tok, d), jnp.zeros((nct,), jnp.int32),
                               jnp.full((nct,), ffn_tmc, jnp.int32),
                               wg, wu, wd, ffn_tmc, ffn_tf, False, False, res=xnc.reshape(nc_tok, d), mod=mod_c,
                               tiles_per_mod=nct).reshape(bsz, lc, d)
        else:
            j = i // 2
            rw32 = jnp.concatenate([router_w[j], jnp.zeros((d, LANES - N_EXPERTS), F32)], axis=1)
            rw_hi = rw32.astype(BF16)
            rw = jnp.concatenate([rw_hi, (rw32 - rw_hi.astype(F32)).astype(BF16)], axis=1)

            def moe_ffn(xs, ya_, zb_, attn_, yd_, mod_, shared, tms):
                b_, l_, _ = xs.shape
                n_ = b_ * l_
                xn, h_tok, logits = _merge_call(xs, ya_, zb_, attn_, yd_, mod_, shared, merge_w, tms, True, rw)
                route, counts = _route_call(logits.reshape(n_, LANES), min(n_, 512))
                eidx = route[:, 0:TOP_K].astype(jnp.int32)
                rank = route[:, 4:4 + TOP_K].astype(jnp.int32)
                cnt = counts[0, :N_EXPERTS].astype(jnp.int32)
                padded = (cnt + moe_tm - 1) // moe_tm * moe_tm
                ends = jnp.cumsum(padded)
                starts = ends - padded
                onehot = eidx[..., None] == jnp.arange(N_EXPERTS, dtype=jnp.int32)
                pos = jnp.sum(jnp.where(onehot, starts, 0), axis=-1) + rank
                n_tiles = TOP_K * n_ // moe_tm + N_EXPERTS
                tile_row = jnp.arange(n_tiles, dtype=jnp.int32) * moe_tm
                tile_e = jnp.minimum(jnp.sum(tile_row[:, None] >= ends[None, :], axis=1), N_EXPERTS - 1)
                used = jnp.clip((starts + cnt)[tile_e] - tile_row, 0, moe_tm)
                half = moe_tm // 2
                tile_v = jnp.where(used > half, moe_tm, jnp.where(used > 0, half, 0)).astype(jnp.int32)
                tail = jnp.stack([ends[-1], (n_tiles * moe_tm - ends[-1]) // (moe_tm // 2)])
                pad_info = jnp.concatenate([starts + cnt, padded - cnt, tail]).astype(jnp.int32)
                xs_tok = _dispatch_call(pos, pad_info, h_tok, n_tiles * moe_tm, min(n_, 1024), moe_tm)
                y_tok = _ffn_call(xs_tok, tile_e.astype(jnp.int32), tile_v, wg, wu, wd, moe_tm, ffn_tf, True, True,
                                  row_options=(moe_tm, half))
                mod_full = jnp.broadcast_to(mod_, (b_, SUBLANES, d)) if shared else mod_
                out = _combine_call(pos, y_tok, xn.reshape(n_, d), route, mod_full, l_, min(l_, 256))
                return out.reshape(b_, l_, d)

            x_new = moe_ffn(x, ya, zb, attn, yd, mod_l, False, tl["tmg"])
            if not last:
                attn_c = _attn_call(qc, [kc], [vc], tcx["tq"])
                ydc = _fourier_call(zdc, *dft_c, wf, tcx["tr"])
                xc = moe_ffn(xc, yac, zbc, attn_c, ydc, mod_c, True, tcx["tmg"])
            x = x_new
    return x
```
